```python
import jax, jax.numpy as jnp
from jax import lax
import numpy as np

D_MODEL = 1024
BATCH = 1
SEQ = 16384
DEPTH = 4

PLE_DIM = 256
SC_WIDTH = 512
SC_CONV = 3
GLA_HEADS = 4
GLA_DK = 64
GLA_DV = 128
GLA_GATE_RANK = 16
GLA_TAU = 16.0
GLA_CHUNK = 64
MLA_HEADS = 4
MLA_NOPE = 128
MLA_ROPE = 64
MLA_V = 128
MLA_Q_RANK = 256
MLA_KV_RANK = 128
ROPE_THETA = 10000.0
Q_BLOCK = 128
N_GROUPS = 8
EXPERTS_PER_GROUP = 8
N_EXPERTS = N_GROUPS * EXPERTS_PER_GROUP
TOP_K = 2
EXPERT_HIDDEN = 256
MOE_BLOCK = 128
DN_ALPHA = (2 * DEPTH) ** 0.25
DN_BETA = (8 * DEPTH) ** -0.25
LN_EPS = 1e-5
RMS_EPS = 1e-6

GLA_QK = GLA_HEADS * GLA_DK
GLA_VW = GLA_HEADS * GLA_DV
MLA_QK = MLA_NOPE + MLA_ROPE
MLA_VW = MLA_HEADS * MLA_V
IN_SPLIT_SIZES = (SC_WIDTH, SC_WIDTH, SC_WIDTH,
                  GLA_QK, GLA_QK, GLA_VW, GLA_GATE_RANK, GLA_VW,
                  MLA_Q_RANK, MLA_KV_RANK, MLA_ROPE,
                  D_MODEL, D_MODEL, D_MODEL)
IN_WIDTH = 3 * SC_WIDTH + 2 * GLA_QK + 2 * GLA_VW + GLA_GATE_RANK + MLA_Q_RANK + MLA_KV_RANK + MLA_ROPE + 3 * D_MODEL
BR_WIDTH = SC_WIDTH + GLA_VW + MLA_VW

kernel_name = "hybrid_conv_gla_mla_hmoe_deepnorm"


def _split_points():
    pts, acc = [], 0
    for s in IN_SPLIT_SIZES[:-1]:
        acc += s
        pts.append(acc)
    return pts


def layer_norm(x, g, b):
    xf = x.astype(jnp.float32)
    mu = jnp.mean(xf, axis=-1, keepdims=True)
    xc = xf - mu
    var = jnp.mean(xc * xc, axis=-1, keepdims=True)
    return (xc * lax.rsqrt(var + LN_EPS) * g + b).astype(x.dtype)


def rms_norm(x, g):
    xf = x.astype(jnp.float32)
    return (xf * lax.rsqrt(jnp.mean(xf * xf, axis=-1, keepdims=True) + RMS_EPS) * g).astype(x.dtype)


def rope_tables(positions, dim):
    inv = 1.0 / (ROPE_THETA ** (jnp.arange(0, dim, 2, dtype=jnp.float32) / dim))
    ang = positions.astype(jnp.float32)[..., None] * inv
    return jnp.cos(ang), jnp.sin(ang)


def apply_rope(x, cos, sin):
    xf = x.astype(jnp.float32)
    x1, x2 = jnp.split(xf, 2, axis=-1)
    c, s = cos[:, :, None, :], sin[:, :, None, :]
    return jnp.concatenate([x1 * c - x2 * s, x1 * s + x2 * c], axis=-1).astype(x.dtype)


def short_gated_conv(b_gate, c_gate, xv, w_conv):
    u = c_gate * xv
    y = lax.conv_general_dilated(u, w_conv[:, None, :].astype(u.dtype), window_strides=(1,),
                                 padding=[(SC_CONV - 1, 0)],
                                 dimension_numbers=('NWC', 'WIO', 'NWC'),
                                 feature_group_count=u.shape[-1])
    return b_gate * y


def gla_chunked(q, k, v, log_a, r, norm_g):
    B, S, _ = q.shape
    C, N, H = GLA_CHUNK, S // GLA_CHUNK, GLA_HEADS
    f32 = jnp.float32
    qf = q.astype(f32).reshape(B, N, C, H, GLA_DK) * (GLA_DK ** -0.5)
    kf = k.astype(f32).reshape(B, N, C, H, GLA_DK)
    vf = v.astype(f32).reshape(B, N, C, H, GLA_DV)
    g = log_a.astype(f32).reshape(B, N, C, H, GLA_DK)
    b = jnp.cumsum(g, axis=2)
    b_last = b[:, :, -1]
    q_e = qf * jnp.exp(b)
    k_e = kf * jnp.exp(-b)
    k_t = kf * jnp.exp(b_last[:, :, None] - b)
    causal = jnp.tril(jnp.ones((C, C), dtype=bool))
    a = jnp.einsum('bnihd,bnjhd->bnhij', q_e, k_e)
    a = jnp.where(causal, a, 0.0)
    o_intra = jnp.einsum('bnhij,bnjhe->bnihe', a, vf)
    kv = jnp.einsum('bnjhd,bnjhe->bnhde', k_t, vf)
    decay = jnp.exp(b_last)

    def step(state, inp):
        kv_n, d_n = inp
        return d_n[..., None] * state + kv_n, state

    _, s_prev = lax.scan(step, jnp.zeros((B, H, GLA_DK, GLA_DV), f32),
                         (jnp.moveaxis(kv, 1, 0), jnp.moveaxis(decay, 1, 0)))
    s_prev = jnp.moveaxis(s_prev, 0, 1)
    o_inter = jnp.einsum('bnihd,bnhde->bnihe', q_e, s_prev)
    o = (o_intra + o_inter).reshape(B, S, H, GLA_DV)
    o = rms_norm(o, norm_g) * jax.nn.silu(r.astype(f32).reshape(B, S, H, GLA_DV))
    return o.reshape(B, S, GLA_VW).astype(v.dtype)


def causal_block_attention(q, k, v):
    B, S, H, Dh = q.shape
    nqb = S // Q_BLOCK
    scale = Dh ** -0.5
    qb = jnp.moveaxis(q.reshape(B, nqb, Q_BLOCK, H, Dh), 1, 0)
    k_pos = jnp.arange(S)

    def one_block(args):
        q_blk, blk = args
        s = jnp.einsum('bqhd,bkhd->bhqk', q_blk, k, preferred_element_type=jnp.float32) * scale
        q_pos = blk * Q_BLOCK + jnp.arange(Q_BLOCK)
        s = jnp.where(k_pos[None, :] <= q_pos[:, None], s, -jnp.inf)
        pr = jax.nn.softmax(s, axis=-1)
        return jnp.einsum('bhqk,bkhd->bqhd', pr.astype(v.dtype), v)

    out = lax.map(one_block, (qb, jnp.arange(nqb)))
    return jnp.moveaxis(out, 0, 1).reshape(B, S, H, v.shape[-1])


def mla(c_q, c_kv, k_rope_raw, q_norm_g, kv_norm_g, w_uq, w_ukv, cos, sin):
    B, S, _ = c_q.shape
    H = MLA_HEADS
    q = (rms_norm(c_q, q_norm_g) @ w_uq).reshape(B, S, H, MLA_QK)
    q_nope, q_rope = q[..., :MLA_NOPE], q[..., MLA_NOPE:]
    kv = (rms_norm(c_kv, kv_norm_g) @ w_ukv).reshape(B, S, H, MLA_NOPE + MLA_V)
    k_nope, v = kv[..., :MLA_NOPE], kv[..., MLA_NOPE:]
    q_rope = apply_rope(q_rope, cos, sin)
    k_rope = apply_rope(k_rope_raw[:, :, None, :], cos, sin)
    q = jnp.concatenate([q_nope, q_rope], axis=-1)
    k = jnp.concatenate([k_nope, jnp.broadcast_to(k_rope, (B, S, H, MLA_ROPE))], axis=-1)
    return causal_block_attention(q, k, v).reshape(B, S, MLA_VW)


def hier_moe(x, w_grp, b_grp, w_exp, b_exp, w_gate, w_up, w_down):
    B, S, D = x.shape
    T = B * S
    xt = x.reshape(T, D)
    grp_logits = (xt @ w_grp).astype(jnp.float32) + b_grp
    grp_prob = jax.nn.softmax(grp_logits, axis=-1)
    g_top = jnp.argmax(grp_logits, axis=-1)
    p_g = jnp.take_along_axis(grp_prob, g_top[:, None], axis=1)[:, 0]
    exp_logits = jnp.einsum('td,dge->tge', xt, w_exp).astype(jnp.float32) + b_exp
    sel = jnp.take_along_axis(exp_logits, g_top[:, None, None], axis=1)[:, 0]
    top_vals, top_idx = lax.top_k(sel, TOP_K)
    weights = p_g[:, None] * jax.nn.softmax(top_vals, axis=-1)
    expert_id = g_top[:, None] * EXPERTS_PER_GROUP + top_idx

    A = T * TOP_K
    flat_e = expert_id.reshape(A).astype(jnp.int32)
    order = jnp.argsort(flat_e)
    sorted_e = flat_e[order]
    counts = jnp.zeros((N_EXPERTS,), jnp.int32).at[flat_e].add(1)
    padded = (counts + MOE_BLOCK - 1) // MOE_BLOCK * MOE_BLOCK
    pad_end = jnp.cumsum(padded)
    pad_start = pad_end - padded
    start = jnp.cumsum(counts) - counts
    dest = pad_start[sorted_e] + jnp.arange(A, dtype=jnp.int32) - start[sorted_e]
    nb = -(-A // MOE_BLOCK) + N_EXPERTS
    row_token = jnp.full((nb * MOE_BLOCK,), T, jnp.int32).at[dest].set((order // TOP_K).astype(jnp.int32))
    block_expert = jnp.minimum(jnp.searchsorted(pad_end, jnp.arange(nb) * MOE_BLOCK, side='right'),
                               N_EXPERTS - 1)
    x_pad = jnp.concatenate([xt, jnp.zeros((1, D), xt.dtype)], axis=0)

    def run_block(args):
        tok, e = args
        xb = x_pad[tok]
        h = jax.nn.silu(xb @ w_gate[e]) * (xb @ w_up[e])
        return h @ w_down[e]

    y_rows = lax.map(run_block, (row_token.reshape(nb, MOE_BLOCK), block_expert))
    y_rows = y_rows.reshape(nb * MOE_BLOCK, D)
    y_assign = jnp.zeros((A, D), y_rows.dtype).at[order].set(y_rows[dest])
    y = jnp.sum(y_assign.reshape(T, TOP_K, D).astype(jnp.float32) * weights[..., None], axis=1)
    return y.reshape(B, S, D).astype(x.dtype)


def setup_inputs(seed: int = 0) -> dict:
    key = jax.random.key(seed)
    ks = iter(jax.random.split(key, 32))
    f32 = jnp.float32

    def nrm(shape, fan_in, scale=1.0):
        return jax.random.normal(next(ks), shape, f32) * (scale * fan_in ** -0.5)

    def gain(shape):
        return 1.0 + 0.02 * jax.random.normal(next(ks), shape, f32)

    def bias(shape):
        return 0.01 * jax.random.normal(next(ks), shape, f32)

    L = DEPTH
    return {
        "x": jax.random.normal(next(ks), (BATCH, SEQ, D_MODEL), f32),
        "p": jax.random.normal(next(ks), (DEPTH, BATCH, SEQ, PLE_DIM), f32),
        "positions": jnp.broadcast_to(jnp.arange(SEQ, dtype=jnp.int32)[None, :], (BATCH, SEQ)),
        "ln0_g": gain((D_MODEL,)),
        "ln0_b": bias((D_MODEL,)),
        "w_in": nrm((L, D_MODEL, IN_WIDTH), D_MODEL),
        "w_conv": nrm((L, SC_CONV, SC_WIDTH), SC_CONV),
        "w_gla_gate": nrm((L, GLA_GATE_RANK, GLA_QK), GLA_GATE_RANK),
        "b_gla_gate": bias((L, GLA_QK)),
        "gla_norm_g": gain((L, GLA_DV)),
        "mla_q_norm_g": gain((L, MLA_Q_RANK)),
        "mla_kv_norm_g": gain((L, MLA_KV_RANK)),
        "w_uq": nrm((L, MLA_Q_RANK, MLA_HEADS * MLA_QK), MLA_Q_RANK),
        "w_ukv": nrm((L, MLA_KV_RANK, MLA_HEADS * (MLA_NOPE + MLA_V)), MLA_KV_RANK),
        "w_br": nrm((L, BR_WIDTH, D_MODEL), SC_WIDTH, DN_BETA),
        "w_o": nrm((L, D_MODEL, D_MODEL), D_MODEL, DN_BETA),
        "ln1_g": gain((L, D_MODEL)),
        "ln1_b": bias((L, D_MODEL)),
        "w_grp": nrm((L, D_MODEL, N_GROUPS), D_MODEL),
        "b_grp": bias((L, N_GROUPS)),
        "w_exp": nrm((L, D_MODEL, N_GROUPS, EXPERTS_PER_GROUP), D_MODEL),
        "b_exp": bias((L, N_GROUPS, EXPERTS_PER_GROUP)),
        "w_gate": nrm((L, N_EXPERTS, D_MODEL, EXPERT_HIDDEN), D_MODEL),
        "w_up": nrm((L, N_EXPERTS, D_MODEL, EXPERT_HIDDEN), D_MODEL, DN_BETA),
        "w_down": nrm((L, N_EXPERTS, EXPERT_HIDDEN, D_MODEL), EXPERT_HIDDEN, DN_BETA),
        "ln2_g": gain((L, D_MODEL)),
        "ln2_b": bias((L, D_MODEL)),
        "w_ple_gate": nrm((L, D_MODEL, D_MODEL), D_MODEL),
        "b_ple_gate": bias((L, D_MODEL)),
        "w_ple_up": nrm((L, PLE_DIM, D_MODEL), PLE_DIM, DN_BETA),
        "ln3_g": gain((L, D_MODEL)),
        "ln3_b": bias((L, D_MODEL)),
    }


def reference(x, p, positions, ln0_g, ln0_b, w_in, w_conv, w_gla_gate, b_gla_gate, gla_norm_g,
              mla_q_norm_g, mla_kv_norm_g, w_uq, w_ukv, w_br, w_o, ln1_g, ln1_b,
              w_grp, b_grp, w_exp, b_exp, w_gate, w_up, w_down, ln2_g, ln2_b,
              w_ple_gate, b_ple_gate, w_ple_up, ln3_g, ln3_b):
    cos, sin = rope_tables(positions, MLA_ROPE)
    split_pts = _split_points()
    x = layer_norm(x, ln0_g, ln0_b)
    for i in range(DEPTH):
        h = x @ w_in[i]
        (a_b, a_c, a_x, gq, gk, gv, g_lr, g_r, c_q, c_kv, k_rope,
         gt_a, gt_b, gt_c) = jnp.split(h, split_pts, axis=-1)
        y_a = short_gated_conv(a_b, a_c, a_x, w_conv[i])
        log_a = jax.nn.log_sigmoid((g_lr @ w_gla_gate[i]).astype(jnp.float32) + b_gla_gate[i]) / GLA_TAU
        y_b = gla_chunked(gq, gk, gv, log_a, g_r, gla_norm_g[i])
        y_c = mla(c_q, c_kv, k_rope, mla_q_norm_g[i], mla_kv_norm_g[i],
                  w_uq[i], w_ukv[i], cos, sin)
        wb = w_br[i]
        merged = (jax.nn.sigmoid(gt_a) * (y_a @ wb[:SC_WIDTH])
                  + jax.nn.sigmoid(gt_b) * (y_b @ wb[SC_WIDTH:SC_WIDTH + GLA_VW])
                  + jax.nn.sigmoid(gt_c) * (y_c @ wb[SC_WIDTH + GLA_VW:]))
        x = layer_norm(DN_ALPHA * x + merged @ w_o[i], ln1_g[i], ln1_b[i])
        y_m = hier_moe(x, w_grp[i], b_grp[i], w_exp[i], b_exp[i], w_gate[i], w_up[i], w_down[i])
        x = layer_norm(DN_ALPHA * x + y_m, ln2_g[i], ln2_b[i])
        ple = jax.nn.sigmoid(x @ w_ple_gate[i] + b_ple_gate[i]) * (p[i] @ w_ple_up[i])
        x = layer_norm(DN_ALPHA * x + ple, ln3_g[i], ln3_b[i])
    return x
```

```python
import functools

import jax
import jax.numpy as jnp
from jax import lax
from jax.experimental import pallas as pl
from jax.experimental.pallas import tpu as pltpu

D_MODEL = 1024
DEPTH = 4
PLE_DIM = 256
SC_WIDTH = 512
SC_CONV = 3
GLA_HEADS = 4
GLA_DK = 64
GLA_DV = 128
GLA_GATE_RANK = 16
GLA_TAU = 16.0
GLA_CHUNK = 64
MLA_HEADS = 4
MLA_NOPE = 128
MLA_ROPE = 64
MLA_V = 128
MLA_Q_RANK = 256
MLA_KV_RANK = 128
ROPE_THETA = 10000.0
N_GROUPS = 8
EXPERTS_PER_GROUP = 8
N_EXPERTS = N_GROUPS * EXPERTS_PER_GROUP
TOP_K = 2
EXPERT_HIDDEN = 256
DN_ALPHA = (2 * DEPTH) ** 0.25
LN_EPS = 1e-5
RMS_EPS = 1e-6

GLA_QK = GLA_HEADS * GLA_DK
GLA_VW = GLA_HEADS * GLA_DV
MLA_QK = MLA_NOPE + MLA_ROPE
MLA_VW = MLA_HEADS * MLA_V

LANES = 128
MLA_QK_PAD = 2 * LANES
VMEM_LIMIT = 56 * 1024 * 1024

ROW_TILE = 512
ATTN_TQ = 512
ATTN_TK = 512
MOE_BM = 256
COMBINE_TILE = 256

_MXU = jnp.bfloat16
_F32 = jnp.float32


def _dot(a, b):
    return jnp.dot(a, b, preferred_element_type=_F32)


def _dot_nt(a, b):
    return lax.dot_general(a, b, (((1,), (1,)), ((), ())), preferred_element_type=_F32)


def _dot_tn(a, b):
    return lax.dot_general(a, b, (((0,), (0,)), ((), ())), preferred_element_type=_F32)


def _layer_norm(x, g, b):
    mu = jnp.mean(x, axis=-1, keepdims=True)
    xc = x - mu
    var = jnp.mean(xc * xc, axis=-1, keepdims=True)
    return xc * lax.rsqrt(var + LN_EPS) * g + b


def _rms_norm(x, g):
    return x * lax.rsqrt(jnp.mean(x * x, axis=-1, keepdims=True) + RMS_EPS) * g


def _sigmoid(x):
    return 1.0 / (1.0 + jnp.exp(-x))


def _params(n_axes=1):
    return pltpu.CompilerParams(dimension_semantics=("arbitrary",) * n_axes,
                                vmem_limit_bytes=VMEM_LIMIT)


def _ln0_kernel(x_ref, g_ref, b_ref, o_ref):
    o_ref[...] = _layer_norm(x_ref[...], g_ref[...], b_ref[...])


def _entry_norm(x, g, b):
    S = x.shape[0]
    return pl.pallas_call(
        _ln0_kernel,
        name="entry_ln",
        grid=(S // ROW_TILE,),
        in_specs=[pl.BlockSpec((ROW_TILE, D_MODEL), lambda i: (i, 0)),
                  pl.BlockSpec((1, D_MODEL), lambda i: (0, 0)),
                  pl.BlockSpec((1, D_MODEL), lambda i: (0, 0))],
        out_specs=pl.BlockSpec((ROW_TILE, D_MODEL), lambda i: (i, 0)),
        out_shape=jax.ShapeDtypeStruct((S, D_MODEL), _F32),
        compiler_params=_params(),
    )(x, g.reshape(1, D_MODEL), b.reshape(1, D_MODEL))


def _rope_kernel(pos_ref, inv_ref, c_ref, s_ref):
    ang = pos_ref[...].astype(_F32) * inv_ref[...]
    lane = lax.broadcasted_iota(jnp.int32, ang.shape, 1)
    half = MLA_ROPE // 2
    cosv = jnp.cos(ang)
    sinv = jnp.sin(ang)
    c_ref[...] = jnp.where(lane < MLA_ROPE, cosv, 0.0)
    s_ref[...] = jnp.where(lane < half, -sinv, jnp.where(lane < MLA_ROPE, sinv, 0.0))


def _rope_tables(positions):
    S = positions.shape[0]
    half = MLA_ROPE // 2
    inv = 1.0 / (ROPE_THETA ** (jnp.arange(0, MLA_ROPE, 2, dtype=_F32) / MLA_ROPE))
    inv = jnp.tile(inv, LANES // half).reshape(1, LANES)
    return pl.pallas_call(
        _rope_kernel,
        name="rope_tables",
        grid=(S // ROW_TILE,),
        in_specs=[pl.BlockSpec((ROW_TILE, 1), lambda i: (i, 0)),
                  pl.BlockSpec((1, LANES), lambda i: (0, 0))],
        out_specs=[pl.BlockSpec((ROW_TILE, LANES), lambda i: (i, 0)),
                   pl.BlockSpec((ROW_TILE, LANES), lambda i: (i, 0))],
        out_shape=[jax.ShapeDtypeStruct((S, LANES), _F32)] * 2,
        compiler_params=_params(),
    )(positions.reshape(S, 1), inv)


def _conv_kernel(lid, x_ref, w_ref, wc_ref, o_ref, u_scr):
    tm = x_ref.shape[0]

    @pl.when(pl.program_id(0) == 0)
    def _():
        u_scr[0:8, :] = jnp.zeros((8, SC_WIDTH), _F32)

    h = _dot(x_ref[...].astype(_MXU), w_ref[...])
    a_b = h[:, :SC_WIDTH]
    u = h[:, SC_WIDTH:2 * SC_WIDTH] * h[:, 2 * SC_WIDTH:]
    u_scr[8:8 + tm, :] = u
    wc = wc_ref[...]
    y = wc[2:3, :] * u + wc[1:2, :] * u_scr[7:7 + tm, :] + wc[0:1, :] * u_scr[6:6 + tm, :]
    o_ref[...] = (a_b * y).astype(o_ref.dtype)
    u_scr[0:8, :] = u_scr[tm:tm + 8, :]


def _conv_branch(lid, x, w_conv_in, w_conv):
    S = x.shape[0]
    tm = ROW_TILE
    return pl.pallas_call(
        _conv_kernel,
        name="conv_branch",
        grid_spec=pltpu.PrefetchScalarGridSpec(
            num_scalar_prefetch=1,
            grid=(S // tm,),
            in_specs=[pl.BlockSpec((tm, D_MODEL), lambda i, l: (i, 0)),
                      pl.BlockSpec((None, D_MODEL, 3 * SC_WIDTH), lambda i, l: (l[0], 0, 0)),
                      pl.BlockSpec((None, SC_CONV, SC_WIDTH), lambda i, l: (l[0], 0, 0))],
            out_specs=pl.BlockSpec((tm, SC_WIDTH), lambda i, l: (i, 0)),
            scratch_shapes=[pltpu.VMEM((tm + 8, SC_WIDTH), _F32)]),
        out_shape=jax.ShapeDtypeStruct((S, SC_WIDTH), _MXU),
        compiler_params=_params(),
    )(lid, x, w_conv_in, w_conv)


_GLA_IN = 2 * GLA_QK + 2 * GLA_VW + LANES


def _gla_kernel(lid, x_ref, w_ref, wg_ref, bg_ref, ng_ref, o_ref, st_scr):
    tm = x_ref.shape[0]
    C = GLA_CHUNK

    @pl.when(pl.program_id(0) == 0)
    def _():
        st_scr[...] = jnp.zeros(st_scr.shape, _F32)

    h = _dot(x_ref[...].astype(_MXU), w_ref[...])
    q = h[:, :GLA_QK]
    k = h[:, GLA_QK:2 * GLA_QK]
    v = h[:, 2 * GLA_QK:2 * GLA_QK + GLA_VW]
    r = h[:, 2 * GLA_QK + GLA_VW:2 * GLA_QK + 2 * GLA_VW]
    g_lr = h[:, 2 * GLA_QK + 2 * GLA_VW:]
    z = _dot(g_lr.astype(_MXU), wg_ref[...]) + bg_ref[...]
    log_a = (jnp.minimum(z, 0.0) - jnp.log1p(jnp.exp(-jnp.abs(z)))) * (1.0 / GLA_TAU)

    row_in_chunk = lax.broadcasted_iota(jnp.int32, log_a.shape, 0) & (C - 1)
    b = log_a
    d = 1
    while d < C:
        b = b + jnp.where(row_in_chunk >= d, pltpu.roll(b, d, 0), 0.0)
        d *= 2

    ri = lax.broadcasted_iota(jnp.int32, (C, C), 0)
    ci = lax.broadcasted_iota(jnp.int32, (C, C), 1)
    causal = ci <= ri
    scale = GLA_DK ** -0.5
    ng = ng_ref[...]
    for c in range(tm // C):
        rows = slice(c * C, (c + 1) * C)
        bc = b[rows]
        b_last = bc[C - 1:C, :]
        kc = k[rows]
        q_e = (q[rows] * scale) * jnp.exp(bc)
        k_e = kc * jnp.exp(-bc)
        k_t = kc * jnp.exp(b_last - bc)
        decay = jnp.exp(b_last)
        for hh in range(GLA_HEADS):
            ks = slice(hh * GLA_DK, (hh + 1) * GLA_DK)
            vs = slice(hh * GLA_DV, (hh + 1) * GLA_DV)
            qh = q_e[:, ks].astype(_MXU)
            vh = v[rows, vs].astype(_MXU)
            a = jnp.where(causal, _dot_nt(qh, k_e[:, ks].astype(_MXU)), 0.0)
            st = st_scr[hh]
            o = _dot(a.astype(_MXU), vh) + _dot_nt(qh, st.astype(_MXU))
            st_scr[hh] = st * decay[:, ks] + _dot_tn(vh, k_t[:, ks].astype(_MXU))
            rr = r[rows, vs]
            o_ref[rows, vs] = (_rms_norm(o, ng) * (rr * _sigmoid(rr))).astype(o_ref.dtype)


def _gla_branch(lid, x, w_gla_in, w_gate, b_gate, norm_g):
    S = x.shape[0]
    tm = ROW_TILE
    return pl.pallas_call(
        _gla_kernel,
        name="gla_branch",
        grid_spec=pltpu.PrefetchScalarGridSpec(
            num_scalar_prefetch=1,
            grid=(S // tm,),
            in_specs=[pl.BlockSpec((tm, D_MODEL), lambda i, l: (i, 0)),
                      pl.BlockSpec((None, D_MODEL, _GLA_IN), lambda i, l: (l[0], 0, 0)),
                      pl.BlockSpec((None, LANES, GLA_QK), lambda i, l: (l[0], 0, 0)),
                      pl.BlockSpec((None, 1, GLA_QK), lambda i, l: (l[0], 0, 0)),
                      pl.BlockSpec((None, 1, GLA_DV), lambda i, l: (l[0], 0, 0))],
            out_specs=pl.BlockSpec((tm, GLA_VW), lambda i, l: (i, 0)),
            scratch_shapes=[pltpu.VMEM((GLA_HEADS, GLA_DV, GLA_DK), _F32)]),
        out_shape=jax.ShapeDtypeStruct((S, GLA_VW), _MXU),
        compiler_params=_params(),
    )(lid, x, w_gla_in, w_gate, b_gate, norm_g)


_MLA_IN = MLA_Q_RANK + MLA_KV_RANK + LANES


def _mla_prep_kernel(lid, x_ref, w_ref, qg_ref, kvg_ref, wuq_ref, wk_ref, wv_ref, ct_ref, st_ref,
                     q_o, k_o, v_o):
    h = _dot(x_ref[...].astype(_MXU), w_ref[...])
    c_q = _rms_norm(h[:, :MLA_Q_RANK], qg_ref[...]).astype(_MXU)
    c_kv = _rms_norm(h[:, MLA_Q_RANK:MLA_Q_RANK + MLA_KV_RANK], kvg_ref[...]).astype(_MXU)
    k_rope_raw = h[:, MLA_Q_RANK + MLA_KV_RANK:]
    q = _dot(c_q, wuq_ref[...])
    k_nope = _dot(c_kv, wk_ref[...])
    v = _dot(c_kv, wv_ref[...])
    cos_t = ct_ref[...]
    sin_t = st_ref[...]
    lane = lax.broadcasted_iota(jnp.int32, cos_t.shape, 1)
    half = MLA_ROPE // 2

    def rope(xr):
        rot = jnp.where(lane < half, pltpu.roll(xr, LANES - half, 1), pltpu.roll(xr, half, 1))
        return xr * cos_t + rot * sin_t

    k_rope = rope(k_rope_raw)
    scale = MLA_QK ** -0.5
    for hh in range(MLA_HEADS):
        base = hh * MLA_QK_PAD
        q_h = jnp.concatenate([q[:, base:base + MLA_NOPE], rope(q[:, base + MLA_NOPE:base + MLA_QK_PAD])],
                              axis=1)
        q_o[hh] = (q_h * scale).astype(q_o.dtype)
        k_o[hh] = jnp.concatenate([k_nope[:, hh * MLA_NOPE:(hh + 1) * MLA_NOPE], k_rope],
                                  axis=1).astype(k_o.dtype)
        v_o[hh] = v[:, hh * MLA_V:(hh + 1) * MLA_V].astype(v_o.dtype)


def _mla_prep(lid, x, w_mla_in, q_norm_g, kv_norm_g, w_uq, w_uk, w_uv, cos_t, sin_t):
    S = x.shape[0]
    tm = ROW_TILE
    H = MLA_HEADS
    return pl.pallas_call(
        _mla_prep_kernel,
        name="mla_prep",
        grid_spec=pltpu.PrefetchScalarGridSpec(
            num_scalar_prefetch=1,
            grid=(S // tm,),
            in_specs=[pl.BlockSpec((tm, D_MODEL), lambda i, l: (i, 0)),
                      pl.BlockSpec((None, D_MODEL, _MLA_IN), lambda i, l: (l[0], 0, 0)),
                      pl.BlockSpec((None, 1, MLA_Q_RANK), lambda i, l: (l[0], 0, 0)),
                      pl.BlockSpec((None, 1, MLA_KV_RANK), lambda i, l: (l[0], 0, 0)),
                      pl.BlockSpec((None, MLA_Q_RANK, H * MLA_QK_PAD), lambda i, l: (l[0], 0, 0)),
                      pl.BlockSpec((None, MLA_KV_RANK, H * MLA_NOPE), lambda i, l: (l[0], 0, 0)),
                      pl.BlockSpec((None, MLA_KV_RANK, H * MLA_V), lambda i, l: (l[0], 0, 0)),
                      pl.BlockSpec((tm, LANES), lambda i, l: (i, 0)),
                      pl.BlockSpec((tm, LANES), lambda i, l: (i, 0))],
            out_specs=[pl.BlockSpec((H, tm, MLA_QK_PAD), lambda i, l: (0, i, 0)),
                       pl.BlockSpec((H, tm, MLA_QK_PAD), lambda i, l: (0, i, 0)),
                       pl.BlockSpec((H, tm, MLA_V), lambda i, l: (0, i, 0))]),
        out_shape=[jax.ShapeDtypeStruct((H, S, MLA_QK_PAD), _MXU),
                   jax.ShapeDtypeStruct((H, S, MLA_QK_PAD), _MXU),
                   jax.ShapeDtypeStruct((H, S, MLA_V), _MXU)],
        compiler_params=_params(),
    )(lid, x, w_mla_in, q_norm_g, kv_norm_g, w_uq, w_uk, w_uv, cos_t, sin_t)


def _attn_kernel(q_ref, k_ref, v_ref, o_ref, m_scr, l_scr, acc_scr):
    tq = q_ref.shape[0]
    tk = ATTN_TK
    qi = pl.program_id(1)
    m_scr[...] = jnp.full(m_scr.shape, -jnp.inf, _F32)
    l_scr[...] = jnp.zeros(l_scr.shape, _F32)
    acc_scr[...] = jnp.zeros(acc_scr.shape, _F32)
    q = q_ref[...]

    def step(start, diag_offset):
        k = k_ref[pl.ds(start, tk), :]
        v = v_ref[pl.ds(start, tk), :]
        s = _dot_nt(q, k)
        if diag_offset is not None:
            ri = lax.broadcasted_iota(jnp.int32, s.shape, 0)
            ci = lax.broadcasted_iota(jnp.int32, s.shape, 1) + diag_offset
            s = jnp.where(ci <= ri, s, -jnp.inf)
        m_prev = m_scr[...]
        m_new = jnp.maximum(m_prev, jnp.max(s, axis=1, keepdims=True))
        alpha = jnp.exp(m_prev - m_new)
        p = jnp.exp(s - m_new[:, :1])
        l_scr[...] = alpha * l_scr[...] + jnp.sum(p, axis=1, keepdims=True)
        acc_scr[...] = alpha * acc_scr[...] + _dot(p.astype(v.dtype), v)
        m_scr[...] = m_new

    n_sub = tq // tk

    def body(j, carry):
        step(pl.multiple_of(j * tk, tk), None)
        return carry

    lax.fori_loop(0, qi * n_sub, body, 0)
    for d in range(n_sub):
        step(pl.multiple_of(qi * tq + d * tk, tk), d * tk)
    o_ref[...] = (acc_scr[...] / l_scr[...]).astype(o_ref.dtype)


def _attention(q, k, v):
    H, S, _ = q.shape
    tq = ATTN_TQ
    return pl.pallas_call(
        _attn_kernel,
        name="mla_attention",
        grid=(H, S // tq),
        in_specs=[pl.BlockSpec((None, tq, MLA_QK_PAD), lambda h, i: (h, i, 0)),
                  pl.BlockSpec((None, S, MLA_QK_PAD), lambda h, i: (h, 0, 0)),
                  pl.BlockSpec((None, S, MLA_V), lambda h, i: (h, 0, 0))],
        out_specs=pl.BlockSpec((tq, MLA_V), lambda h, i: (i, h)),
        out_shape=jax.ShapeDtypeStruct((S, MLA_VW), _MXU),
        scratch_shapes=[pltpu.VMEM((tq, LANES), _F32),
                        pltpu.VMEM((tq, LANES), _F32),
                        pltpu.VMEM((tq, MLA_V), _F32)],
        compiler_params=_params(2),
    )(q, k, v)


def _merge_kernel(lid, x_ref, ya_ref, yb_ref, yc_ref, wgt_ref, wbr_ref, wo_ref, g_ref, b_ref,
                  wr_ref, br_ref, x_o, route_o):
    x = x_ref[...]
    gates = _dot(x.astype(_MXU), wgt_ref[...])
    wbr = wbr_ref
    merged = (_sigmoid(gates[:, :D_MODEL]) * _dot(ya_ref[...], wbr[0:SC_WIDTH, :])
              + _sigmoid(gates[:, D_MODEL:2 * D_MODEL])
              * _dot(yb_ref[...], wbr[SC_WIDTH:SC_WIDTH + GLA_VW, :])
              + _sigmoid(gates[:, 2 * D_MODEL:]) * _dot(yc_ref[...], wbr[SC_WIDTH + GLA_VW:, :]))
    x1 = _layer_norm(DN_ALPHA * x + _dot(merged.astype(_MXU), wo_ref[...]), g_ref[...], b_ref[...])
    x_o[...] = x1

    logits = jnp.dot(x1, wr_ref[...], preferred_element_type=_F32,
                     precision=lax.Precision.HIGHEST) + br_ref[...]
    lane = lax.broadcasted_iota(jnp.int32, logits.shape, 1)
    neg = -jnp.inf
    gl = jnp.where(lane < N_GROUPS, logits, neg)
    g_max = jnp.max(gl, axis=1, keepdims=True)
    g_top = jnp.min(jnp.where(gl == g_max, lane, LANES), axis=1, keepdims=True)
    p_g = 1.0 / jnp.sum(jnp.where(lane < N_GROUPS, jnp.exp(logits - g_max), 0.0), axis=1, keepdims=True)
    lo = N_GROUPS + g_top * EXPERTS_PER_GROUP
    sl = jnp.where((lane >= lo) & (lane < lo + EXPERTS_PER_GROUP), logits, neg)
    v0 = jnp.max(sl, axis=1, keepdims=True)
    i0 = jnp.min(jnp.where(sl == v0, lane, LANES), axis=1, keepdims=True)
    sl = jnp.where(lane == i0, neg, sl)
    v1 = jnp.max(sl, axis=1, keepdims=True)
    i1 = jnp.min(jnp.where(sl == v1, lane, LANES), axis=1, keepdims=True)
    e1 = jnp.exp(v1 - v0)
    w0 = p_g / (1.0 + e1)
    w1 = p_g * e1 / (1.0 + e1)
    route_o[...] = jnp.where(lane == 0, (i0 - N_GROUPS).astype(_F32),
                             jnp.where(lane == 1, (i1 - N_GROUPS).astype(_F32),
                                       jnp.where(lane == 2, w0, jnp.where(lane == 3, w1, 0.0))))


def _merge(lid, x, y_a, y_b, y_c, w_gt, w_br, w_o, ln_g, ln_b, w_route, b_route):
    S = x.shape[0]
    tm = ROW_TILE
    br_w = SC_WIDTH + GLA_VW + MLA_VW
    return pl.pallas_call(
        _merge_kernel,
        name="merge_route",
        grid_spec=pltpu.PrefetchScalarGridSpec(
            num_scalar_prefetch=1,
            grid=(S // tm,),
            in_specs=[pl.BlockSpec((tm, D_MODEL), lambda i, l: (i, 0)),
                      pl.BlockSpec((tm, SC_WIDTH), lambda i, l: (i, 0)),
                      pl.BlockSpec((tm, GLA_VW), lambda i, l: (i, 0)),
                      pl.BlockSpec((tm, MLA_VW), lambda i, l: (i, 0)),
                      pl.BlockSpec((None, D_MODEL, 3 * D_MODEL), lambda i, l: (l[0], 0, 0)),
                      pl.BlockSpec((None, br_w, D_MODEL), lambda i, l: (l[0], 0, 0)),
                      pl.BlockSpec((None, D_MODEL, D_MODEL), lambda i, l: (l[0], 0, 0)),
                      pl.BlockSpec((None, 1, D_MODEL), lambda i, l: (l[0], 0, 0)),
                      pl.BlockSpec((None, 1, D_MODEL), lambda i, l: (l[0], 0, 0)),
                      pl.BlockSpec((None, D_MODEL, LANES), lambda i, l: (l[0], 0, 0)),
                      pl.BlockSpec((None, 1, LANES), lambda i, l: (l[0], 0, 0))],
            out_specs=[pl.BlockSpec((tm, D_MODEL), lambda i, l: (i, 0)),
                       pl.BlockSpec((tm, LANES), lambda i, l: (i, 0))]),
        out_shape=[jax.ShapeDtypeStruct((S, D_MODEL), _F32),
                   jax.ShapeDtypeStruct((S, LANES), _F32)],
        compiler_params=_params(),
    )(lid, x, y_a, y_b, y_c, w_gt, w_br, w_o, ln_g, ln_b, w_route, b_route)


def _dispatch_plan(expert_id, n_tokens):
    A = n_tokens * TOP_K
    flat_e = expert_id.reshape(A)
    order = jnp.argsort(flat_e)
    sorted_e = flat_e[order]
    counts = jnp.zeros((N_EXPERTS,), jnp.int32).at[flat_e].add(1)
    padded = (counts + MOE_BM - 1) // MOE_BM * MOE_BM
    pad_end = jnp.cumsum(padded)
    pad_start = pad_end - padded
    start = jnp.cumsum(counts) - counts
    dest = pad_start[sorted_e] + jnp.arange(A, dtype=jnp.int32) - start[sorted_e]
    nb = A // MOE_BM + N_EXPERTS
    row_token = jnp.full((nb * MOE_BM,), n_tokens, jnp.int32).at[dest].set((order // TOP_K).astype(jnp.int32))
    row_of_assignment = jnp.zeros((A,), jnp.int32).at[order].set(dest)
    block_expert = jnp.minimum(
        jnp.searchsorted(pad_end, jnp.arange(nb, dtype=jnp.int32) * MOE_BM, side='right'),
        N_EXPERTS - 1).astype(jnp.int32)
    n_active = (pad_end[-1] // MOE_BM).astype(jnp.int32)
    first = jnp.concatenate([jnp.ones((1,), jnp.int32),
                             (block_expert[1:] != block_expert[:-1]).astype(jnp.int32)])
    meta = jnp.concatenate([n_active.reshape(1), jnp.zeros((1,), jnp.int32)])
    return row_token.reshape(nb, 1, MOE_BM), row_of_assignment, block_expert, first, meta


def _expert_kernel(lid, bexp, first, meta, tok_ref, x_hbm, wg_ref, wu_ref, wd_ref, o_ref,
                   xg, wg_b, wu_b, wd_b, sem):
    b = pl.program_id(0)
    n_tokens = x_hbm.shape[0]

    @pl.when(b >= meta[0])
    def _():
        o_ref[...] = jnp.zeros(o_ref.shape, o_ref.dtype)

    @pl.when(b < meta[0])
    def _():
        def row_copy(r):
            tok = jnp.minimum(tok_ref[0, 0, r], n_tokens - 1)
            return pltpu.make_async_copy(x_hbm.at[pl.ds(tok, 1), :], xg.at[pl.ds(r, 1), :], sem.at[0])

        def start(r, c):
            row_copy(r).start()
            return c

        lax.fori_loop(0, MOE_BM, start, 0)

        @pl.when(first[b] == 1)
        def _():
            wg_b[...] = wg_ref[...].astype(_MXU)
            wu_b[...] = wu_ref[...].astype(_MXU)
            wd_b[...] = wd_ref[...].astype(_MXU)

        def wait(r, c):
            row_copy(r).wait()
            return c

        lax.fori_loop(0, MOE_BM, wait, 0)
        xb = xg[...].astype(_MXU)
        hg = _dot(xb, wg_b[...])
        hu = _dot(xb, wu_b[...])
        hid = (hg * _sigmoid(hg)) * hu
        o_ref[...] = _dot(hid.astype(_MXU), wd_b[...])


def _experts(lid, block_expert, first, meta, row_token, x, w_gate, w_up, w_down):
    nb = row_token.shape[0]
    return pl.pallas_call(
        _expert_kernel,
        name="moe_experts",
        grid_spec=pltpu.PrefetchScalarGridSpec(
            num_scalar_prefetch=4,
            grid=(nb,),
            in_specs=[pl.BlockSpec((1, 1, MOE_BM), lambda b, l, e, f, m: (b, 0, 0),
                                   memory_space=pltpu.SMEM),
                      pl.BlockSpec(memory_space=pl.ANY),
                      pl.BlockSpec((None, None, D_MODEL, EXPERT_HIDDEN),
                                   lambda b, l, e, f, m: (l[0], e[b], 0, 0)),
                      pl.BlockSpec((None, None, D_MODEL, EXPERT_HIDDEN),
                                   lambda b, l, e, f, m: (l[0], e[b], 0, 0)),
                      pl.BlockSpec((None, None, EXPERT_HIDDEN, D_MODEL),
                                   lambda b, l, e, f, m: (l[0], e[b], 0, 0))],
            out_specs=pl.BlockSpec((MOE_BM, D_MODEL), lambda b, l, e, f, m: (b, 0)),
            scratch_shapes=[pltpu.VMEM((MOE_BM, D_MODEL), _F32),
                            pltpu.VMEM((D_MODEL, EXPERT_HIDDEN), _MXU),
                            pltpu.VMEM((D_MODEL, EXPERT_HIDDEN), _MXU),
                            pltpu.VMEM((EXPERT_HIDDEN, D_MODEL), _MXU),
                            pltpu.SemaphoreType.DMA((1,))]),
        out_shape=jax.ShapeDtypeStruct((nb * MOE_BM, D_MODEL), _F32),
        compiler_params=_params(),
    )(lid, block_expert, first, meta, row_token, x, w_gate, w_up, w_down)


def _combine_kernel(lid, pos_ref, y_hbm, x_ref, route_ref, g2_ref, b2_ref, wpg_ref, bpg_ref,
                    p_ref, wpu_ref, g3_ref, b3_ref, o_ref, ybuf, sem):
    tc = x_ref.shape[0]

    def row_copy(r, kk):
        row = pos_ref[0, 0, TOP_K * r + kk]
        return pltpu.make_async_copy(y_hbm.at[pl.ds(row, 1), :], ybuf.at[kk, pl.ds(r, 1), :], sem.at[0])

    def start(r, c):
        for kk in range(TOP_K):
            row_copy(r, kk).start()
        return c

    def wait(r, c):
        for kk in range(TOP_K):
            row_copy(r, kk).wait()
        return c

    lax.fori_loop(0, tc, start, 0)
    lax.fori_loop(0, tc, wait, 0)
    route = route_ref[...]
    y = route[:, 2:3] * ybuf[0] + route[:, 3:4] * ybuf[1]
    x2 = _layer_norm(DN_ALPHA * x_ref[...] + y, g2_ref[...], b2_ref[...])
    gate = _sigmoid(_dot(x2.astype(_MXU), wpg_ref[...]) + bpg_ref[...])
    up = _dot(p_ref[...].astype(_MXU), wpu_ref[...])
    o_ref[...] = _layer_norm(DN_ALPHA * x2 + gate * up, g3_ref[...], b3_ref[...])


def _combine(lid, pos, y_rows, x, route, ln2_g, ln2_b, w_pg, b_pg, p, w_pu, ln3_g, ln3_b):
    S = x.shape[0]
    tc = COMBINE_TILE
    vec = lambda n: pl.BlockSpec((None, 1, n), lambda i, l: (l[0], 0, 0))
    return pl.pallas_call(
        _combine_kernel,
        name="combine_ple",
        grid_spec=pltpu.PrefetchScalarGridSpec(
            num_scalar_prefetch=1,
            grid=(S // tc,),
            in_specs=[pl.BlockSpec((1, 1, TOP_K * tc), lambda i, l: (i, 0, 0), memory_space=pltpu.SMEM),
                      pl.BlockSpec(memory_space=pl.ANY),
                      pl.BlockSpec((tc, D_MODEL), lambda i, l: (i, 0)),
                      pl.BlockSpec((tc, LANES), lambda i, l: (i, 0)),
                      vec(D_MODEL), vec(D_MODEL),
                      pl.BlockSpec((None, D_MODEL, D_MODEL), lambda i, l: (l[0], 0, 0)),
                      vec(D_MODEL),
                      pl.BlockSpec((None, tc, PLE_DIM), lambda i, l: (l[0], i, 0)),
                      pl.BlockSpec((None, PLE_DIM, D_MODEL), lambda i, l: (l[0], 0, 0)),
                      vec(D_MODEL), vec(D_MODEL)],
            out_specs=pl.BlockSpec((tc, D_MODEL), lambda i, l: (i, 0)),
            scratch_shapes=[pltpu.VMEM((TOP_K, tc, D_MODEL), _F32),
                            pltpu.SemaphoreType.DMA((1,))]),
        out_shape=jax.ShapeDtypeStruct((S, D_MODEL), _F32),
        compiler_params=_params(),
    )(lid, pos, y_rows, x, route, ln2_g, ln2_b, w_pg, b_pg, p, w_pu, ln3_g, ln3_b)


def _split_in_proj(w_in):
    sizes = (SC_WIDTH, SC_WIDTH, SC_WIDTH, GLA_QK, GLA_QK, GLA_VW, GLA_GATE_RANK, GLA_VW,
             MLA_Q_RANK, MLA_KV_RANK, MLA_ROPE, D_MODEL, D_MODEL, D_MODEL)
    offs = [0]
    for s in sizes:
        offs.append(offs[-1] + s)
    col = lambda j: w_in[:, :, offs[j]:offs[j + 1]]
    L = w_in.shape[0]
    zeros = lambda n: jnp.zeros((L, D_MODEL, n), w_in.dtype)
    w_conv_in = jnp.concatenate([col(0), col(1), col(2)], axis=-1)
    w_gla_in = jnp.concatenate([col(3), col(4), col(5), col(7), col(6), zeros(LANES - GLA_GATE_RANK)], axis=-1)
    w_mla_in = jnp.concatenate([col(8), col(9), col(10), zeros(LANES - MLA_ROPE)], axis=-1)
    w_gates = jnp.concatenate([col(11), col(12), col(13)], axis=-1)
    return [w.astype(_MXU) for w in (w_conv_in, w_gla_in, w_mla_in, w_gates)]


def kernel(x, p, positions, ln0_g, ln0_b, w_in, w_conv, w_gla_gate, b_gla_gate, gla_norm_g, mla_q_norm_g, mla_kv_norm_g, w_uq, w_ukv, w_br, w_o, ln1_g, ln1_b, w_grp, b_grp, w_exp, b_exp, w_gate, w_up, w_down, ln2_g, ln2_b, w_ple_gate, b_ple_gate, w_ple_up, ln3_g, ln3_b):
    B, S, D = x.shape
    L = w_in.shape[0]
    assert B == 1 and D == D_MODEL and S % ROW_TILE == 0 and S % ATTN_TQ == 0
    H = MLA_HEADS

    w_conv_in, w_gla_in, w_mla_in, w_gates = _split_in_proj(w_in)
    w_gg = jnp.pad(w_gla_gate, ((0, 0), (0, LANES - GLA_GATE_RANK), (0, 0))).astype(_MXU)
    w_uq_p = jnp.pad(w_uq.reshape(L, MLA_Q_RANK, H, MLA_QK),
                     ((0, 0), (0, 0), (0, 0), (0, MLA_QK_PAD - MLA_QK))
                     ).reshape(L, MLA_Q_RANK, H * MLA_QK_PAD).astype(_MXU)
    w_ukv4 = w_ukv.reshape(L, MLA_KV_RANK, H, MLA_NOPE + MLA_V)
    w_uk = w_ukv4[..., :MLA_NOPE].reshape(L, MLA_KV_RANK, H * MLA_NOPE).astype(_MXU)
    w_uv = w_ukv4[..., MLA_NOPE:].reshape(L, MLA_KV_RANK, H * MLA_V).astype(_MXU)
    w_route = jnp.concatenate(
        [w_grp, w_exp.reshape(L, D, N_EXPERTS),
         jnp.zeros((L, D, LANES - N_GROUPS - N_EXPERTS), _F32)], axis=-1)
    b_route = jnp.concatenate(
        [b_grp, b_exp.reshape(L, N_EXPERTS),
         jnp.zeros((L, LANES - N_GROUPS - N_EXPERTS), _F32)], axis=-1).reshape(L, 1, LANES)
    w_br_c = w_br.astype(_MXU)
    w_o_c = w_o.astype(_MXU)
    w_pg_c = w_ple_gate.astype(_MXU)
    w_pu_c = w_ple_up.astype(_MXU)
    row = lambda a: a.reshape(L, 1, a.shape[-1])

    cos_t, sin_t = _rope_tables(positions.reshape(S))
    x0 = _entry_norm(x.reshape(S, D), ln0_g, ln0_b)
    p3 = p.reshape(L, S, PLE_DIM)

    def layer(xc, i):
        lid = jnp.reshape(i, (1,)).astype(jnp.int32)
        y_a = _conv_branch(lid, xc, w_conv_in, w_conv)
        y_b = _gla_branch(lid, xc, w_gla_in, w_gg, row(b_gla_gate), row(gla_norm_g))
        q, k, v = _mla_prep(lid, xc, w_mla_in, row(mla_q_norm_g), row(mla_kv_norm_g),
                            w_uq_p, w_uk, w_uv, cos_t, sin_t)
        y_c = _attention(q, k, v)
        x1, route = _merge(lid, xc, y_a, y_b, y_c, w_gates, w_br_c, w_o_c, row(ln1_g), row(ln1_b),
                           w_route, b_route)
        expert_id = route[:, :TOP_K].astype(jnp.int32)
        row_token, row_of_assignment, block_expert, first, meta = _dispatch_plan(expert_id, S)
        y_rows = _experts(lid, block_expert, first, meta, row_token, x1, w_gate, w_up, w_down)
        pos = row_of_assignment.reshape(S // COMBINE_TILE, 1, TOP_K * COMBINE_TILE)
        x3 = _combine(lid, pos, y_rows, x1, route, row(ln2_g), row(ln2_b), w_pg_c, row(b_ple_gate),
                      p3, w_pu_c, row(ln3_g), row(ln3_b))
        return x3, None

    out, _ = lax.scan(layer, x0, jnp.arange(L, dtype=jnp.int32))
    return out.reshape(B, S, D)
```

```python
import functools

import jax
import jax.numpy as jnp
from jax import lax
from jax.experimental import pallas as pl
from jax.experimental.pallas import tpu as pltpu

D_MODEL = 1024
DEPTH = 4
PLE_DIM = 256
SC_WIDTH = 512
SC_CONV = 3
GLA_HEADS = 4
GLA_DK = 64
GLA_DV = 128
GLA_GATE_RANK = 16
GLA_TAU = 16.0
GLA_CHUNK = 64
MLA_HEADS = 4
MLA_NOPE = 128
MLA_ROPE = 64
MLA_V = 128
MLA_Q_RANK = 256
MLA_KV_RANK = 128
ROPE_THETA = 10000.0
N_GROUPS = 8
EXPERTS_PER_GROUP = 8
N_EXPERTS = N_GROUPS * EXPERTS_PER_GROUP
TOP_K = 2
EXPERT_HIDDEN = 256
DN_ALPHA = (2 * DEPTH) ** 0.25
LN_EPS = 1e-5
RMS_EPS = 1e-6

GLA_QK = GLA_HEADS * GLA_DK
GLA_VW = GLA_HEADS * GLA_DV
MLA_QK = MLA_NOPE + MLA_ROPE
MLA_VW = MLA_HEADS * MLA_V

LANES = 128
MLA_QK_PAD = 2 * LANES
VMEM_LIMIT = 56 * 1024 * 1024

ROW_TILE = 512
ATTN_TQ = 512
ATTN_TK = 512
MOE_BM = 256
COMBINE_TILE = 256
DISPATCH_TILE = 1024
TOKEN_ROWS = D_MODEL // LANES

_MXU = jnp.bfloat16
_F32 = jnp.float32


def _dot(a, b):
    return jnp.dot(a, b, preferred_element_type=_F32)


def _dot_nt(a, b):
    return lax.dot_general(a, b, (((1,), (1,)), ((), ())), preferred_element_type=_F32)


def _dot_tn(a, b):
    return lax.dot_general(a, b, (((0,), (0,)), ((), ())), preferred_element_type=_F32)


def _layer_norm(x, g, b):
    mu = jnp.mean(x, axis=-1, keepdims=True)
    xc = x - mu
    var = jnp.mean(xc * xc, axis=-1, keepdims=True)
    return xc * lax.rsqrt(var + LN_EPS) * g + b


def _rms_norm(x, g):
    return x * lax.rsqrt(jnp.mean(x * x, axis=-1, keepdims=True) + RMS_EPS) * g


def _sigmoid(x):
    return 1.0 / (1.0 + jnp.exp(-x))


def _params(n_axes=1):
    return pltpu.CompilerParams(dimension_semantics=("arbitrary",) * n_axes,
                                vmem_limit_bytes=VMEM_LIMIT)


def _ln0_kernel(x_ref, g_ref, b_ref, o_ref):
    o_ref[...] = _layer_norm(x_ref[...], g_ref[...], b_ref[...])


def _entry_norm(x, g, b):
    S = x.shape[0]
    return pl.pallas_call(
        _ln0_kernel,
        name="entry_ln",
        grid=(S // ROW_TILE,),
        in_specs=[pl.BlockSpec((ROW_TILE, D_MODEL), lambda i: (i, 0)),
                  pl.BlockSpec((1, D_MODEL), lambda i: (0, 0)),
                  pl.BlockSpec((1, D_MODEL), lambda i: (0, 0))],
        out_specs=pl.BlockSpec((ROW_TILE, D_MODEL), lambda i: (i, 0)),
        out_shape=jax.ShapeDtypeStruct((S, D_MODEL), _F32),
        compiler_params=_params(),
    )(x, g.reshape(1, D_MODEL), b.reshape(1, D_MODEL))


def _rope_kernel(pos_ref, inv_ref, c_ref, s_ref):
    ang = pos_ref[...].astype(_F32) * inv_ref[...]
    lane = lax.broadcasted_iota(jnp.int32, ang.shape, 1)
    half = MLA_ROPE // 2
    cosv = jnp.cos(ang)
    sinv = jnp.sin(ang)
    c_ref[...] = jnp.where(lane < MLA_ROPE, cosv, 0.0)
    s_ref[...] = jnp.where(lane < half, -sinv, jnp.where(lane < MLA_ROPE, sinv, 0.0))


def _rope_tables(positions):
    S = positions.shape[0]
    half = MLA_ROPE // 2
    inv = 1.0 / (ROPE_THETA ** (jnp.arange(0, MLA_ROPE, 2, dtype=_F32) / MLA_ROPE))
    inv = jnp.tile(inv, LANES // half).reshape(1, LANES)
    return pl.pallas_call(
        _rope_kernel,
        name="rope_tables",
        grid=(S // ROW_TILE,),
        in_specs=[pl.BlockSpec((ROW_TILE, 1), lambda i: (i, 0)),
                  pl.BlockSpec((1, LANES), lambda i: (0, 0))],
        out_specs=[pl.BlockSpec((ROW_TILE, LANES), lambda i: (i, 0)),
                   pl.BlockSpec((ROW_TILE, LANES), lambda i: (i, 0))],
        out_shape=[jax.ShapeDtypeStruct((S, LANES), _F32)] * 2,
        compiler_params=_params(),
    )(positions.reshape(S, 1), inv)


def _conv_kernel(lid, x_ref, w_ref, wc_ref, o_ref, u_scr):
    tm = x_ref.shape[0]

    @pl.when(pl.program_id(0) == 0)
    def _():
        u_scr[0:8, :] = jnp.zeros((8, SC_WIDTH), _F32)

    h = _dot(x_ref[...].astype(_MXU), w_ref[...])
    a_b = h[:, :SC_WIDTH]
    u = h[:, SC_WIDTH:2 * SC_WIDTH] * h[:, 2 * SC_WIDTH:]
    u_scr[8:8 + tm, :] = u
    wc = wc_ref[...]
    y = wc[2:3, :] * u + wc[1:2, :] * u_scr[7:7 + tm, :] + wc[0:1, :] * u_scr[6:6 + tm, :]
    o_ref[...] = (a_b * y).astype(o_ref.dtype)
    u_scr[0:8, :] = u_scr[tm:tm + 8, :]


def _conv_branch(lid, x, w_conv_in, w_conv):
    S = x.shape[0]
    tm = ROW_TILE
    return pl.pallas_call(
        _conv_kernel,
        name="conv_branch",
        grid_spec=pltpu.PrefetchScalarGridSpec(
            num_scalar_prefetch=1,
            grid=(S // tm,),
            in_specs=[pl.BlockSpec((tm, D_MODEL), lambda i, l: (i, 0)),
                      pl.BlockSpec((None, D_MODEL, 3 * SC_WIDTH), lambda i, l: (l[0], 0, 0)),
                      pl.BlockSpec((None, SC_CONV, SC_WIDTH), lambda i, l: (l[0], 0, 0))],
            out_specs=pl.BlockSpec((tm, SC_WIDTH), lambda i, l: (i, 0)),
            scratch_shapes=[pltpu.VMEM((tm + 8, SC_WIDTH), _F32)]),
        out_shape=jax.ShapeDtypeStruct((S, SC_WIDTH), _MXU),
        compiler_params=_params(),
    )(lid, x, w_conv_in, w_conv)


_GLA_IN = 2 * GLA_QK + 2 * GLA_VW + LANES


def _gla_kernel(lid, x_ref, w_ref, wg_ref, bg_ref, ng_ref, o_ref, st_scr):
    tm = x_ref.shape[0]
    C = GLA_CHUNK

    @pl.when(pl.program_id(0) == 0)
    def _():
        st_scr[...] = jnp.zeros(st_scr.shape, _F32)

    h = _dot(x_ref[...].astype(_MXU), w_ref[...])
    q = h[:, :GLA_QK]
    k = h[:, GLA_QK:2 * GLA_QK]
    v = h[:, 2 * GLA_QK:2 * GLA_QK + GLA_VW]
    r = h[:, 2 * GLA_QK + GLA_VW:2 * GLA_QK + 2 * GLA_VW]
    g_lr = h[:, 2 * GLA_QK + 2 * GLA_VW:]
    z = _dot(g_lr.astype(_MXU), wg_ref[...]) + bg_ref[...]
    log_a = (jnp.minimum(z, 0.0) - jnp.log1p(jnp.exp(-jnp.abs(z)))) * (1.0 / GLA_TAU)

    row_in_chunk = lax.broadcasted_iota(jnp.int32, log_a.shape, 0) & (C - 1)
    b = log_a
    d = 1
    while d < C:
        b = b + jnp.where(row_in_chunk >= d, pltpu.roll(b, d, 0), 0.0)
        d *= 2

    ri = lax.broadcasted_iota(jnp.int32, (C, C), 0)
    ci = lax.broadcasted_iota(jnp.int32, (C, C), 1)
    causal = ci <= ri
    scale = GLA_DK ** -0.5
    ng = ng_ref[...]
    for c in range(tm // C):
        rows = slice(c * C, (c + 1) * C)
        bc = b[rows]
        b_last = bc[C - 1:C, :]
        kc = k[rows]
        q_e = (q[rows] * scale) * jnp.exp(bc)
        k_e = kc * jnp.exp(-bc)
        k_t = kc * jnp.exp(b_last - bc)
        decay = jnp.exp(b_last)
        for hh in range(GLA_HEADS):
            ks = slice(hh * GLA_DK, (hh + 1) * GLA_DK)
            vs = slice(hh * GLA_DV, (hh + 1) * GLA_DV)
            qh = q_e[:, ks].astype(_MXU)
            vh = v[rows, vs].astype(_MXU)
            a = jnp.where(causal, _dot_nt(qh, k_e[:, ks].astype(_MXU)), 0.0)
            st = st_scr[hh]
            o = _dot(a.astype(_MXU), vh) + _dot_nt(qh, st.astype(_MXU))
            st_scr[hh] = st * decay[:, ks] + _dot_tn(vh, k_t[:, ks].astype(_MXU))
            rr = r[rows, vs]
            o_ref[rows, vs] = (_rms_norm(o, ng) * (rr * _sigmoid(rr))).astype(o_ref.dtype)


def _gla_branch(lid, x, w_gla_in, w_gate, b_gate, norm_g):
    S = x.shape[0]
    tm = ROW_TILE
    return pl.pallas_call(
        _gla_kernel,
        name="gla_branch",
        grid_spec=pltpu.PrefetchScalarGridSpec(
            num_scalar_prefetch=1,
            grid=(S // tm,),
            in_specs=[pl.BlockSpec((tm, D_MODEL), lambda i, l: (i, 0)),
                      pl.BlockSpec((None, D_MODEL, _GLA_IN), lambda i, l: (l[0], 0, 0)),
                      pl.BlockSpec((None, LANES, GLA_QK), lambda i, l: (l[0], 0, 0)),
                      pl.BlockSpec((None, 1, GLA_QK), lambda i, l: (l[0], 0, 0)),
                      pl.BlockSpec((None, 1, GLA_DV), lambda i, l: (l[0], 0, 0))],
            out_specs=pl.BlockSpec((tm, GLA_VW), lambda i, l: (i, 0)),
            scratch_shapes=[pltpu.VMEM((GLA_HEADS, GLA_DV, GLA_DK), _F32)]),
        out_shape=jax.ShapeDtypeStruct((S, GLA_VW), _MXU),
        compiler_params=_params(),
    )(lid, x, w_gla_in, w_gate, b_gate, norm_g)


_MLA_IN = MLA_Q_RANK + MLA_KV_RANK + LANES


def _mla_prep_kernel(lid, x_ref, w_ref, qg_ref, kvg_ref, wuq_ref, wk_ref, wv_ref, ct_ref, st_ref,
                     q_o, k_o, v_o):
    h = _dot(x_ref[...].astype(_MXU), w_ref[...])
    c_q = _rms_norm(h[:, :MLA_Q_RANK], qg_ref[...]).astype(_MXU)
    c_kv = _rms_norm(h[:, MLA_Q_RANK:MLA_Q_RANK + MLA_KV_RANK], kvg_ref[...]).astype(_MXU)
    k_rope_raw = h[:, MLA_Q_RANK + MLA_KV_RANK:]
    q = _dot(c_q, wuq_ref[...])
    k_nope = _dot(c_kv, wk_ref[...])
    v = _dot(c_kv, wv_ref[...])
    cos_t = ct_ref[...]
    sin_t = st_ref[...]
    lane = lax.broadcasted_iota(jnp.int32, cos_t.shape, 1)
    half = MLA_ROPE // 2

    def rope(xr):
        rot = jnp.where(lane < half, pltpu.roll(xr, LANES - half, 1), pltpu.roll(xr, half, 1))
        return xr * cos_t + rot * sin_t

    k_rope = rope(k_rope_raw)
    scale = MLA_QK ** -0.5
    for hh in range(MLA_HEADS):
        base = hh * MLA_QK_PAD
        q_h = jnp.concatenate([q[:, base:base + MLA_NOPE], rope(q[:, base + MLA_NOPE:base + MLA_QK_PAD])],
                              axis=1)
        q_o[hh] = (q_h * scale).astype(q_o.dtype)
        k_o[hh] = jnp.concatenate([k_nope[:, hh * MLA_NOPE:(hh + 1) * MLA_NOPE], k_rope],
                                  axis=1).astype(k_o.dtype)
        v_o[hh] = v[:, hh * MLA_V:(hh + 1) * MLA_V].astype(v_o.dtype)


def _mla_prep(lid, x, w_mla_in, q_norm_g, kv_norm_g, w_uq, w_uk, w_uv, cos_t, sin_t):
    S = x.shape[0]
    tm = ROW_TILE
    H = MLA_HEADS
    return pl.pallas_call(
        _mla_prep_kernel,
        name="mla_prep",
        grid_spec=pltpu.PrefetchScalarGridSpec(
            num_scalar_prefetch=1,
            grid=(S // tm,),
            in_specs=[pl.BlockSpec((tm, D_MODEL), lambda i, l: (i, 0)),
                      pl.BlockSpec((None, D_MODEL, _MLA_IN), lambda i, l: (l[0], 0, 0)),
                      pl.BlockSpec((None, 1, MLA_Q_RANK), lambda i, l: (l[0], 0, 0)),
                      pl.BlockSpec((None, 1, MLA_KV_RANK), lambda i, l: (l[0], 0, 0)),
                      pl.BlockSpec((None, MLA_Q_RANK, H * MLA_QK_PAD), lambda i, l: (l[0], 0, 0)),
                      pl.BlockSpec((None, MLA_KV_RANK, H * MLA_NOPE), lambda i, l: (l[0], 0, 0)),
                      pl.BlockSpec((None, MLA_KV_RANK, H * MLA_V), lambda i, l: (l[0], 0, 0)),
                      pl.BlockSpec((tm, LANES), lambda i, l: (i, 0)),
                      pl.BlockSpec((tm, LANES), lambda i, l: (i, 0))],
            out_specs=[pl.BlockSpec((H, tm, MLA_QK_PAD), lambda i, l: (0, i, 0)),
                       pl.BlockSpec((H, tm, MLA_QK_PAD), lambda i, l: (0, i, 0)),
                       pl.BlockSpec((H, tm, MLA_V), lambda i, l: (0, i, 0))]),
        out_shape=[jax.ShapeDtypeStruct((H, S, MLA_QK_PAD), _MXU),
                   jax.ShapeDtypeStruct((H, S, MLA_QK_PAD), _MXU),
                   jax.ShapeDtypeStruct((H, S, MLA_V), _MXU)],
        compiler_params=_params(),
    )(lid, x, w_mla_in, q_norm_g, kv_norm_g, w_uq, w_uk, w_uv, cos_t, sin_t)


def _attn_kernel(q_ref, k_ref, v_ref, o_ref, m_scr, l_scr, acc_scr):
    tq = q_ref.shape[0]
    tk = ATTN_TK
    qi = pl.program_id(1)
    m_scr[...] = jnp.full(m_scr.shape, -jnp.inf, _F32)
    l_scr[...] = jnp.zeros(l_scr.shape, _F32)
    acc_scr[...] = jnp.zeros(acc_scr.shape, _F32)
    q = q_ref[...]

    def step(start, diag_offset):
        k = k_ref[pl.ds(start, tk), :]
        v = v_ref[pl.ds(start, tk), :]
        s = _dot_nt(q, k)
        if diag_offset is not None:
            ri = lax.broadcasted_iota(jnp.int32, s.shape, 0)
            ci = lax.broadcasted_iota(jnp.int32, s.shape, 1) + diag_offset
            s = jnp.where(ci <= ri, s, -jnp.inf)
        m_prev = m_scr[...]
        m_new = jnp.maximum(m_prev, jnp.max(s, axis=1, keepdims=True))
        alpha = jnp.exp(m_prev - m_new)
        p = jnp.exp(s - m_new[:, :1])
        l_scr[...] = alpha * l_scr[...] + jnp.sum(p, axis=1, keepdims=True)
        acc_scr[...] = alpha * acc_scr[...] + _dot(p.astype(v.dtype), v)
        m_scr[...] = m_new

    n_sub = tq // tk

    def body(j, carry):
        step(pl.multiple_of(j * tk, tk), None)
        return carry

    lax.fori_loop(0, qi * n_sub, body, 0)
    for d in range(n_sub):
        step(pl.multiple_of(qi * tq + d * tk, tk), d * tk)
    o_ref[...] = (acc_scr[...] / l_scr[...]).astype(o_ref.dtype)


def _attention(q, k, v):
    H, S, _ = q.shape
    tq = ATTN_TQ
    return pl.pallas_call(
        _attn_kernel,
        name="mla_attention",
        grid=(H, S // tq),
        in_specs=[pl.BlockSpec((None, tq, MLA_QK_PAD), lambda h, i: (h, i, 0)),
                  pl.BlockSpec((None, S, MLA_QK_PAD), lambda h, i: (h, 0, 0)),
                  pl.BlockSpec((None, S, MLA_V), lambda h, i: (h, 0, 0))],
        out_specs=pl.BlockSpec((tq, MLA_V), lambda h, i: (i, h)),
        out_shape=jax.ShapeDtypeStruct((S, MLA_VW), _MXU),
        scratch_shapes=[pltpu.VMEM((tq, LANES), _F32),
                        pltpu.VMEM((tq, LANES), _F32),
                        pltpu.VMEM((tq, MLA_V), _F32)],
        compiler_params=_params(2),
    )(q, k, v)


def _merge_kernel(lid, x_ref, ya_ref, yb_ref, yc_ref, wgt_ref, wbr_ref, wo_ref, g_ref, b_ref,
                  wr_ref, br_ref, tril_ref, x_o, xt_o, route_o, cnt_o, cnt_scr):
    tm = x_ref.shape[0]

    @pl.when(pl.program_id(0) == 0)
    def _():
        cnt_scr[...] = jnp.zeros(cnt_scr.shape, _F32)

    x = x_ref[...]
    gates = _dot(x.astype(_MXU), wgt_ref[...])
    wbr = wbr_ref
    merged = (_sigmoid(gates[:, :D_MODEL]) * _dot(ya_ref[...], wbr[0:SC_WIDTH, :])
              + _sigmoid(gates[:, D_MODEL:2 * D_MODEL])
              * _dot(yb_ref[...], wbr[SC_WIDTH:SC_WIDTH + GLA_VW, :])
              + _sigmoid(gates[:, 2 * D_MODEL:]) * _dot(yc_ref[...], wbr[SC_WIDTH + GLA_VW:, :]))
    x1 = _layer_norm(DN_ALPHA * x + _dot(merged.astype(_MXU), wo_ref[...]), g_ref[...], b_ref[...])
    x_o[...] = x1
    for c in range(TOKEN_ROWS):
        xt_o[pl.ds(c, tm, stride=TOKEN_ROWS), :] = x1[:, c * LANES:(c + 1) * LANES]

    logits = jnp.dot(x1, wr_ref[...], preferred_element_type=_F32,
                     precision=lax.Precision.HIGHEST) + br_ref[...]
    lane = lax.broadcasted_iota(jnp.int32, logits.shape, 1)
    neg = -jnp.inf
    gl = jnp.where(lane < N_GROUPS, logits, neg)
    g_max = jnp.max(gl, axis=1, keepdims=True)
    g_top = jnp.min(jnp.where(gl == g_max, lane, LANES), axis=1, keepdims=True)
    p_g = 1.0 / jnp.sum(jnp.where(lane < N_GROUPS, jnp.exp(logits - g_max), 0.0), axis=1, keepdims=True)
    lo = N_GROUPS + g_top * EXPERTS_PER_GROUP
    sl = jnp.where((lane >= lo) & (lane < lo + EXPERTS_PER_GROUP), logits, neg)
    v0 = jnp.max(sl, axis=1, keepdims=True)
    i0 = jnp.min(jnp.where(sl == v0, lane, LANES), axis=1, keepdims=True)
    sl = jnp.where(lane == i0, neg, sl)
    v1 = jnp.max(sl, axis=1, keepdims=True)
    i1 = jnp.min(jnp.where(sl == v1, lane, LANES), axis=1, keepdims=True)
    e1 = jnp.exp(v1 - v0)
    w0 = p_g / (1.0 + e1)
    w1 = p_g * e1 / (1.0 + e1)
    oh0 = lane == i0
    oh1 = lane == i1
    ohs = jnp.where(oh0, 1.0, jnp.where(oh1, 1.0, 0.0))
    before = _dot(tril_ref[...], ohs.astype(_MXU)) + cnt_scr[0:1, :]
    rank0 = jnp.sum(jnp.where(oh0, before, 0.0), axis=1, keepdims=True)
    rank1 = jnp.sum(jnp.where(oh1, before, 0.0), axis=1, keepdims=True)
    cnt_scr[...] = cnt_scr[...] + jnp.sum(ohs, axis=0, keepdims=True)
    cnt_o[...] = cnt_scr[...]
    route_o[...] = jnp.where(
        lane == 0, (i0 - N_GROUPS).astype(_F32),
        jnp.where(lane == 1, (i1 - N_GROUPS).astype(_F32),
                  jnp.where(lane == 2, w0,
                            jnp.where(lane == 3, w1,
                                      jnp.where(lane == 4, rank0, jnp.where(lane == 5, rank1, 0.0))))))


def _merge(lid, x, y_a, y_b, y_c, w_gt, w_br, w_o, ln_g, ln_b, w_route, b_route):
    S = x.shape[0]
    tm = ROW_TILE
    tril = jnp.tril(jnp.ones((tm, tm), _F32), -1).astype(_MXU)
    br_w = SC_WIDTH + GLA_VW + MLA_VW
    return pl.pallas_call(
        _merge_kernel,
        name="merge_route",
        grid_spec=pltpu.PrefetchScalarGridSpec(
            num_scalar_prefetch=1,
            grid=(S // tm,),
            in_specs=[pl.BlockSpec((tm, D_MODEL), lambda i, l: (i, 0)),
                      pl.BlockSpec((tm, SC_WIDTH), lambda i, l: (i, 0)),
                      pl.BlockSpec((tm, GLA_VW), lambda i, l: (i, 0)),
                      pl.BlockSpec((tm, MLA_VW), lambda i, l: (i, 0)),
                      pl.BlockSpec((None, D_MODEL, 3 * D_MODEL), lambda i, l: (l[0], 0, 0)),
                      pl.BlockSpec((None, br_w, D_MODEL), lambda i, l: (l[0], 0, 0)),
                      pl.BlockSpec((None, D_MODEL, D_MODEL), lambda i, l: (l[0], 0, 0)),
                      pl.BlockSpec((None, 1, D_MODEL), lambda i, l: (l[0], 0, 0)),
                      pl.BlockSpec((None, 1, D_MODEL), lambda i, l: (l[0], 0, 0)),
                      pl.BlockSpec((None, D_MODEL, LANES), lambda i, l: (l[0], 0, 0)),
                      pl.BlockSpec((None, 1, LANES), lambda i, l: (l[0], 0, 0)),
                      pl.BlockSpec((tm, tm), lambda i, l: (0, 0))],
            out_specs=[pl.BlockSpec((tm, D_MODEL), lambda i, l: (i, 0)),
                       pl.BlockSpec((tm * TOKEN_ROWS, LANES), lambda i, l: (i, 0)),
                       pl.BlockSpec((tm, LANES), lambda i, l: (i, 0)),
                       pl.BlockSpec((8, LANES), lambda i, l: (0, 0))],
            scratch_shapes=[pltpu.VMEM((8, LANES), _F32)]),
        out_shape=[jax.ShapeDtypeStruct((S, D_MODEL), _F32),
                   jax.ShapeDtypeStruct((S * TOKEN_ROWS, LANES), _F32),
                   jax.ShapeDtypeStruct((S, LANES), _F32),
                   jax.ShapeDtypeStruct((8, LANES), _F32)],
        compiler_params=_params(),
    )(lid, x, y_a, y_b, y_c, w_gt, w_br, w_o, ln_g, ln_b, w_route, b_route, tril)


def _dispatch_plan(route, counts_row, n_tokens):
    counts = counts_row[0, N_GROUPS:N_GROUPS + N_EXPERTS].astype(jnp.int32)
    padded = (counts + MOE_BM - 1) // MOE_BM * MOE_BM
    pad_end = jnp.cumsum(padded)
    pad_start = pad_end - padded
    expert_id = route[:, :TOP_K].astype(jnp.int32)
    rank = route[:, 4:4 + TOP_K].astype(jnp.int32)
    dest = (pad_start[expert_id] + rank).reshape(n_tokens * TOP_K)
    nb = n_tokens * TOP_K // MOE_BM + N_EXPERTS
    block_expert = jnp.minimum(
        jnp.searchsorted(pad_end, jnp.arange(nb, dtype=jnp.int32) * MOE_BM, side='right'),
        N_EXPERTS - 1).astype(jnp.int32)
    n_active = (pad_end[-1] // MOE_BM).astype(jnp.int32)
    first = jnp.concatenate([jnp.ones((1,), jnp.int32),
                             (block_expert[1:] != block_expert[:-1]).astype(jnp.int32)])
    meta = jnp.concatenate([n_active.reshape(1), jnp.zeros((1,), jnp.int32)])
    return dest, counts, pad_start.astype(jnp.int32), block_expert, first, meta


def _token_copy(src_hbm, src_row, dst_ref, dst_row, sem):
    return pltpu.make_async_copy(src_hbm.at[pl.ds(src_row * TOKEN_ROWS, TOKEN_ROWS), :],
                                 dst_ref.at[pl.ds(dst_row * TOKEN_ROWS, TOKEN_ROWS), :], sem)


def _dispatch_kernel(cnt, pstart, meta, dest_ref, x_hbm, xs_hbm, zero_scr, sem):
    i = pl.program_id(0)
    tt = DISPATCH_TILE
    blk = MOE_BM * TOKEN_ROWS
    n_blocks = xs_hbm.shape[0] // blk

    def spare_blocks(fn):
        def one(b, c):
            fn(pltpu.make_async_copy(zero_scr, xs_hbm.at[pl.ds(b * blk, blk), :], sem.at[2]))
            return c

        lax.fori_loop(meta[0], n_blocks, one, 0)

    @pl.when(i == 0)
    def _():
        zero_scr[...] = jnp.zeros(zero_scr.shape, _F32)
        spare_blocks(lambda cp: cp.start())

    def start(j, c):
        for kk in range(TOP_K):
            _token_copy(x_hbm, i * tt + j, xs_hbm, dest_ref[0, 0, TOP_K * j + kk], sem.at[0]).start()
        return c

    lax.fori_loop(0, tt, start, 0, unroll=8)

    def pad_rows(fn):
        def per_expert(e, c):
            n = cnt[e]
            n_pad = (n + MOE_BM - 1) // MOE_BM * MOE_BM

            def one(r, c2):
                fn(_token_copy(x_hbm, 0, xs_hbm, pstart[e] + r, sem.at[1]))
                return c2

            lax.fori_loop(n, n_pad, one, 0)
            return c

        lax.fori_loop(0, N_EXPERTS, per_expert, 0)

    @pl.when(i == 0)
    def _():
        pad_rows(lambda cp: cp.start())
        pad_rows(lambda cp: cp.wait())
        spare_blocks(lambda cp: cp.wait())

    def wait(j, c):
        for kk in range(TOP_K):
            _token_copy(x_hbm, 0, xs_hbm, 0, sem.at[0]).wait()
        return c

    lax.fori_loop(0, tt, wait, 0, unroll=8)


def _dispatch(counts, pad_start, meta, dest, x_tok, n_rows):
    n_tokens = x_tok.shape[0] // TOKEN_ROWS
    tt = DISPATCH_TILE
    return pl.pallas_call(
        _dispatch_kernel,
        name="moe_dispatch",
        grid_spec=pltpu.PrefetchScalarGridSpec(
            num_scalar_prefetch=3,
            grid=(n_tokens // tt,),
            in_specs=[pl.BlockSpec((1, 1, TOP_K * tt), lambda i, c, s, m: (i, 0, 0),
                                   memory_space=pltpu.SMEM),
                      pl.BlockSpec(memory_space=pl.ANY)],
            out_specs=pl.BlockSpec(memory_space=pl.ANY),
            scratch_shapes=[pltpu.VMEM((MOE_BM * TOKEN_ROWS, LANES), _F32),
                            pltpu.SemaphoreType.DMA((3,))]),
        out_shape=jax.ShapeDtypeStruct((n_rows * TOKEN_ROWS, LANES), _F32),
        compiler_params=_params(),
    )(counts, pad_start, meta, dest.reshape(n_tokens // tt, 1, TOP_K * tt), x_tok)


def _from_token_tiles(ref, n):
    return jnp.concatenate([ref[pl.ds(c, n, stride=TOKEN_ROWS), :] for c in range(TOKEN_ROWS)], axis=1)


def _expert_kernel(lid, bexp, first, meta, xs_ref, wg_ref, wu_ref, wd_ref, o_ref, wg_b, wu_b, wd_b):
    b = pl.program_id(0)

    @pl.when(b >= meta[0])
    def _():
        o_ref[...] = jnp.zeros(o_ref.shape, o_ref.dtype)

    @pl.when(b < meta[0])
    def _():
        @pl.when(first[b] == 1)
        def _():
            wg_b[...] = wg_ref[...].astype(_MXU)
            wu_b[...] = wu_ref[...].astype(_MXU)
            wd_b[...] = wd_ref[...].astype(_MXU)

        xb = _from_token_tiles(xs_ref, MOE_BM).astype(_MXU)
        hg = _dot(xb, wg_b[...])
        hu = _dot(xb, wu_b[...])
        hid = (hg * _sigmoid(hg)) * hu
        y = _dot(hid.astype(_MXU), wd_b[...])
        for c in range(TOKEN_ROWS):
            o_ref[pl.ds(c, MOE_BM, stride=TOKEN_ROWS), :] = y[:, c * LANES:(c + 1) * LANES]


def _experts(lid, block_expert, first, meta, xs_tok, w_gate, w_up, w_down):
    rows = MOE_BM * TOKEN_ROWS
    nb = xs_tok.shape[0] // rows
    live = lambda b, m: jnp.minimum(b, m[0] - 1)
    return pl.pallas_call(
        _expert_kernel,
        name="moe_experts",
        grid_spec=pltpu.PrefetchScalarGridSpec(
            num_scalar_prefetch=4,
            grid=(nb,),
            in_specs=[pl.BlockSpec((rows, LANES), lambda b, l, e, f, m: (live(b, m), 0)),
                      pl.BlockSpec((None, None, D_MODEL, EXPERT_HIDDEN),
                                   lambda b, l, e, f, m: (l[0], e[b], 0, 0)),
                      pl.BlockSpec((None, None, D_MODEL, EXPERT_HIDDEN),
                                   lambda b, l, e, f, m: (l[0], e[b], 0, 0)),
                      pl.BlockSpec((None, None, EXPERT_HIDDEN, D_MODEL),
                                   lambda b, l, e, f, m: (l[0], e[b], 0, 0))],
            out_specs=pl.BlockSpec((rows, LANES), lambda b, l, e, f, m: (b, 0)),
            scratch_shapes=[pltpu.VMEM((D_MODEL, EXPERT_HIDDEN), _MXU),
                            pltpu.VMEM((D_MODEL, EXPERT_HIDDEN), _MXU),
                            pltpu.VMEM((EXPERT_HIDDEN, D_MODEL), _MXU)]),
        out_shape=jax.ShapeDtypeStruct(xs_tok.shape, _F32),
        compiler_params=_params(),
    )(lid, block_expert, first, meta, xs_tok, w_gate, w_up, w_down)


def _combine_kernel(lid, pos_ref, pos_next_ref, y_hbm, x_ref, route_ref, g2_ref, b2_ref, wpg_ref, bpg_ref,
                    p_ref, wpu_ref, g3_ref, b3_ref, o_ref, ybuf, sem):
    tc = x_ref.shape[0]
    i = pl.program_id(0)
    slot = lax.rem(i, 2)

    def gather(rows_ref, s, fn):
        def body(r, c):
            for kk in range(TOP_K):
                fn(_token_copy(y_hbm, rows_ref[0, 0, TOP_K * r + kk], ybuf.at[s, kk], r, sem.at[s]))
            return c

        lax.fori_loop(0, tc, body, 0, unroll=8)

    @pl.when(i == 0)
    def _():
        gather(pos_ref, 0, lambda cp: cp.start())

    @pl.when(i + 1 < pl.num_programs(0))
    def _():
        gather(pos_next_ref, 1 - slot, lambda cp: cp.start())

    gather(pos_ref, slot, lambda cp: cp.wait())
    route = route_ref[...]
    y = (route[:, 2:3] * _from_token_tiles(ybuf.at[slot, 0], tc)
         + route[:, 3:4] * _from_token_tiles(ybuf.at[slot, 1], tc))
    x2 = _layer_norm(DN_ALPHA * x_ref[...] + y, g2_ref[...], b2_ref[...])
    gate = _sigmoid(_dot(x2.astype(_MXU), wpg_ref[...]) + bpg_ref[...])
    up = _dot(p_ref[...].astype(_MXU), wpu_ref[...])
    o_ref[...] = _layer_norm(DN_ALPHA * x2 + gate * up, g3_ref[...], b3_ref[...])


def _combine(lid, dest, y_tok, x, route, ln2_g, ln2_b, w_pg, b_pg, p, w_pu, ln3_g, ln3_b):
    S = x.shape[0]
    tc = COMBINE_TILE
    n_tiles = S // tc
    pos = dest.reshape(n_tiles, 1, TOP_K * tc)
    vec = lambda n: pl.BlockSpec((None, 1, n), lambda i, l: (l[0], 0, 0))
    return pl.pallas_call(
        _combine_kernel,
        name="combine_ple",
        grid_spec=pltpu.PrefetchScalarGridSpec(
            num_scalar_prefetch=1,
            grid=(S // tc,),
            in_specs=[pl.BlockSpec((1, 1, TOP_K * tc), lambda i, l: (i, 0, 0), memory_space=pltpu.SMEM),
                      pl.BlockSpec((1, 1, TOP_K * tc), lambda i, l: (jnp.minimum(i + 1, n_tiles - 1), 0, 0),
                                   memory_space=pltpu.SMEM),
                      pl.BlockSpec(memory_space=pl.ANY),
                      pl.BlockSpec((tc, D_MODEL), lambda i, l: (i, 0)),
                      pl.BlockSpec((tc, LANES), lambda i, l: (i, 0)),
                      vec(D_MODEL), vec(D_MODEL),
                      pl.BlockSpec((None, D_MODEL, D_MODEL), lambda i, l: (l[0], 0, 0)),
                      vec(D_MODEL),
                      pl.BlockSpec((None, tc, PLE_DIM), lambda i, l: (l[0], i, 0)),
                      pl.BlockSpec((None, PLE_DIM, D_MODEL), lambda i, l: (l[0], 0, 0)),
                      vec(D_MODEL), vec(D_MODEL)],
            out_specs=pl.BlockSpec((tc, D_MODEL), lambda i, l: (i, 0)),
            scratch_shapes=[pltpu.VMEM((2, TOP_K, tc * TOKEN_ROWS, LANES), _F32),
                            pltpu.SemaphoreType.DMA((2,))]),
        out_shape=jax.ShapeDtypeStruct((S, D_MODEL), _F32),
        compiler_params=_params(),
    )(lid, pos, pos, y_tok, x, route, ln2_g, ln2_b, w_pg, b_pg, p, w_pu, ln3_g, ln3_b)


def _split_in_proj(w_in):
    sizes = (SC_WIDTH, SC_WIDTH, SC_WIDTH, GLA_QK, GLA_QK, GLA_VW, GLA_GATE_RANK, GLA_VW,
             MLA_Q_RANK, MLA_KV_RANK, MLA_ROPE, D_MODEL, D_MODEL, D_MODEL)
    offs = [0]
    for s in sizes:
        offs.append(offs[-1] + s)
    col = lambda j: w_in[:, :, offs[j]:offs[j + 1]]
    L = w_in.shape[0]
    zeros = lambda n: jnp.zeros((L, D_MODEL, n), w_in.dtype)
    w_conv_in = jnp.concatenate([col(0), col(1), col(2)], axis=-1)
    w_gla_in = jnp.concatenate([col(3), col(4), col(5), col(7), col(6), zeros(LANES - GLA_GATE_RANK)], axis=-1)
    w_mla_in = jnp.concatenate([col(8), col(9), col(10), zeros(LANES - MLA_ROPE)], axis=-1)
    w_gates = jnp.concatenate([col(11), col(12), col(13)], axis=-1)
    return [w.astype(_MXU) for w in (w_conv_in, w_gla_in, w_mla_in, w_gates)]


def kernel(x, p, positions, ln0_g, ln0_b, w_in, w_conv, w_gla_gate, b_gla_gate, gla_norm_g, mla_q_norm_g, mla_kv_norm_g, w_uq, w_ukv, w_br, w_o, ln1_g, ln1_b, w_grp, b_grp, w_exp, b_exp, w_gate, w_up, w_down, ln2_g, ln2_b, w_ple_gate, b_ple_gate, w_ple_up, ln3_g, ln3_b):
    B, S, D = x.shape
    L = w_in.shape[0]
    assert B == 1 and D == D_MODEL and S % ROW_TILE == 0 and S % ATTN_TQ == 0
    H = MLA_HEADS

    w_conv_in, w_gla_in, w_mla_in, w_gates = _split_in_proj(w_in)
    w_gg = jnp.pad(w_gla_gate, ((0, 0), (0, LANES - GLA_GATE_RANK), (0, 0))).astype(_MXU)
    w_uq_p = jnp.pad(w_uq.reshape(L, MLA_Q_RANK, H, MLA_QK),
                     ((0, 0), (0, 0), (0, 0), (0, MLA_QK_PAD - MLA_QK))
                     ).reshape(L, MLA_Q_RANK, H * MLA_QK_PAD).astype(_MXU)
    w_ukv4 = w_ukv.reshape(L, MLA_KV_RANK, H, MLA_NOPE + MLA_V)
    w_uk = w_ukv4[..., :MLA_NOPE].reshape(L, MLA_KV_RANK, H * MLA_NOPE).astype(_MXU)
    w_uv = w_ukv4[..., MLA_NOPE:].reshape(L, MLA_KV_RANK, H * MLA_V).astype(_MXU)
    w_route = jnp.concatenate(
        [w_grp, w_exp.reshape(L, D, N_EXPERTS),
         jnp.zeros((L, D, LANES - N_GROUPS - N_EXPERTS), _F32)], axis=-1)
    b_route = jnp.concatenate(
        [b_grp, b_exp.reshape(L, N_EXPERTS),
         jnp.zeros((L, LANES - N_GROUPS - N_EXPERTS), _F32)], axis=-1).reshape(L, 1, LANES)
    w_br_c = w_br.astype(_MXU)
    w_o_c = w_o.astype(_MXU)
    w_pg_c = w_ple_gate.astype(_MXU)
    w_pu_c = w_ple_up.astype(_MXU)
    row = lambda a: a.reshape(L, 1, a.shape[-1])

    cos_t, sin_t = _rope_tables(positions.reshape(S))
    x0 = _entry_norm(x.reshape(S, D), ln0_g, ln0_b)
    p3 = p.reshape(L, S, PLE_DIM)

    def layer(xc, i):
        lid = jnp.reshape(i, (1,)).astype(jnp.int32)
        y_a = _conv_branch(lid, xc, w_conv_in, w_conv)
        y_b = _gla_branch(lid, xc, w_gla_in, w_gg, row(b_gla_gate), row(gla_norm_g))
        q, k, v = _mla_prep(lid, xc, w_mla_in, row(mla_q_norm_g), row(mla_kv_norm_g),
                            w_uq_p, w_uk, w_uv, cos_t, sin_t)
        y_c = _attention(q, k, v)
        x1, x1_tok, route, counts_row = _merge(lid, xc, y_a, y_b, y_c, w_gates, w_br_c, w_o_c,
                                               row(ln1_g), row(ln1_b), w_route, b_route)
        dest, counts, pad_start, block_expert, first, meta = _dispatch_plan(route, counts_row, S)
        n_rows = (S * TOP_K // MOE_BM + N_EXPERTS) * MOE_BM
        xs_tok = _dispatch(counts, pad_start, meta, dest, x1_tok, n_rows)
        y_tok = _experts(lid, block_expert, first, meta, xs_tok, w_gate, w_up, w_down)
        x3 = _combine(lid, dest, y_tok, x1, route, row(ln2_g), row(ln2_b), w_pg_c, row(b_ple_gate),
                      p3, w_pu_c, row(ln3_g), row(ln3_b))
        return x3, None

    out, _ = lax.scan(layer, x0, jnp.arange(L, dtype=jnp.int32))
    return out.reshape(B, S, D)
```

```python
import functools

import jax
import jax.numpy as jnp
from jax import lax
from jax.experimental import pallas as pl
from jax.experimental.pallas import tpu as pltpu

D_MODEL = 1024
DEPTH = 4
PLE_DIM = 256
SC_WIDTH = 512
SC_CONV = 3
GLA_HEADS = 4
GLA_DK = 64
GLA_DV = 128
GLA_GATE_RANK = 16
GLA_TAU = 16.0
GLA_CHUNK = 64
MLA_HEADS = 4
MLA_NOPE = 128
MLA_ROPE = 64
MLA_V = 128
MLA_Q_RANK = 256
MLA_KV_RANK = 128
ROPE_THETA = 10000.0
N_GROUPS = 8
EXPERTS_PER_GROUP = 8
N_EXPERTS = N_GROUPS * EXPERTS_PER_GROUP
TOP_K = 2
EXPERT_HIDDEN = 256
DN_ALPHA = (2 * DEPTH) ** 0.25
LN_EPS = 1e-5
RMS_EPS = 1e-6

GLA_QK = GLA_HEADS * GLA_DK
GLA_VW = GLA_HEADS * GLA_DV
MLA_QK = MLA_NOPE + MLA_ROPE
MLA_VW = MLA_HEADS * MLA_V

LANES = 128
MLA_QK_PAD = 2 * LANES
VMEM_LIMIT = 56 * 1024 * 1024

ROW_TILE = 512
ATTN_TQ = 1024
ATTN_TK = 1024
ATTN_ROWS = 256
LOG2_E = 1.4426950408889634
MOE_BM = 256
COMBINE_TILE = 256
DISPATCH_TILE = 1024
TOKEN_ROWS = D_MODEL // LANES

_MXU = jnp.bfloat16
_F32 = jnp.float32


def _dot(a, b):
    return jnp.dot(a, b, preferred_element_type=_F32)


def _dot_nt(a, b):
    return lax.dot_general(a, b, (((1,), (1,)), ((), ())), preferred_element_type=_F32)


def _dot_tn(a, b):
    return lax.dot_general(a, b, (((0,), (0,)), ((), ())), preferred_element_type=_F32)


def _layer_norm(x, g, b):
    mu = jnp.mean(x, axis=-1, keepdims=True)
    xc = x - mu
    var = jnp.mean(xc * xc, axis=-1, keepdims=True)
    return xc * lax.rsqrt(var + LN_EPS) * g + b


def _rms_norm(x, g):
    return x * lax.rsqrt(jnp.mean(x * x, axis=-1, keepdims=True) + RMS_EPS) * g


def _sigmoid(x):
    return 1.0 / (1.0 + jnp.exp(-x))


def _params(n_axes=1):
    return pltpu.CompilerParams(dimension_semantics=("arbitrary",) * n_axes,
                                vmem_limit_bytes=VMEM_LIMIT)


def _ln0_kernel(x_ref, g_ref, b_ref, o_ref):
    o_ref[...] = _layer_norm(x_ref[...], g_ref[...], b_ref[...])


def _entry_norm(x, g, b):
    S = x.shape[0]
    return pl.pallas_call(
        _ln0_kernel,
        name="entry_ln",
        grid=(S // ROW_TILE,),
        in_specs=[pl.BlockSpec((ROW_TILE, D_MODEL), lambda i: (i, 0)),
                  pl.BlockSpec((1, D_MODEL), lambda i: (0, 0)),
                  pl.BlockSpec((1, D_MODEL), lambda i: (0, 0))],
        out_specs=pl.BlockSpec((ROW_TILE, D_MODEL), lambda i: (i, 0)),
        out_shape=jax.ShapeDtypeStruct((S, D_MODEL), _F32),
        compiler_params=_params(),
    )(x, g.reshape(1, D_MODEL), b.reshape(1, D_MODEL))


def _rope_kernel(pos_ref, inv_ref, c_ref, s_ref):
    ang = pos_ref[...].astype(_F32) * inv_ref[...]
    lane = lax.broadcasted_iota(jnp.int32, ang.shape, 1)
    half = MLA_ROPE // 2
    cosv = jnp.cos(ang)
    sinv = jnp.sin(ang)
    c_ref[...] = jnp.where(lane < MLA_ROPE, cosv, 0.0)
    s_ref[...] = jnp.where(lane < half, -sinv, jnp.where(lane < MLA_ROPE, sinv, 0.0))


def _rope_tables(positions):
    S = positions.shape[0]
    half = MLA_ROPE // 2
    inv = 1.0 / (ROPE_THETA ** (jnp.arange(0, MLA_ROPE, 2, dtype=_F32) / MLA_ROPE))
    inv = jnp.tile(inv, LANES // half).reshape(1, LANES)
    return pl.pallas_call(
        _rope_kernel,
        name="rope_tables",
        grid=(S // ROW_TILE,),
        in_specs=[pl.BlockSpec((ROW_TILE, 1), lambda i: (i, 0)),
                  pl.BlockSpec((1, LANES), lambda i: (0, 0))],
        out_specs=[pl.BlockSpec((ROW_TILE, LANES), lambda i: (i, 0)),
                   pl.BlockSpec((ROW_TILE, LANES), lambda i: (i, 0))],
        out_shape=[jax.ShapeDtypeStruct((S, LANES), _F32)] * 2,
        compiler_params=_params(),
    )(positions.reshape(S, 1), inv)


def _conv_kernel(lid, x_ref, w_ref, wc_ref, o_ref, u_scr):
    tm = x_ref.shape[0]

    @pl.when(pl.program_id(0) == 0)
    def _():
        u_scr[0:8, :] = jnp.zeros((8, SC_WIDTH), _F32)

    h = _dot(x_ref[...].astype(_MXU), w_ref[...])
    a_b = h[:, :SC_WIDTH]
    u = h[:, SC_WIDTH:2 * SC_WIDTH] * h[:, 2 * SC_WIDTH:]
    u_scr[8:8 + tm, :] = u
    wc = wc_ref[...]
    y = wc[2:3, :] * u + wc[1:2, :] * u_scr[7:7 + tm, :] + wc[0:1, :] * u_scr[6:6 + tm, :]
    o_ref[...] = (a_b * y).astype(o_ref.dtype)
    u_scr[0:8, :] = u_scr[tm:tm + 8, :]


def _conv_branch(lid, x, w_conv_in, w_conv):
    S = x.shape[0]
    tm = ROW_TILE
    return pl.pallas_call(
        _conv_kernel,
        name="conv_branch",
        grid_spec=pltpu.PrefetchScalarGridSpec(
            num_scalar_prefetch=1,
            grid=(S // tm,),
            in_specs=[pl.BlockSpec((tm, D_MODEL), lambda i, l: (i, 0)),
                      pl.BlockSpec((None, D_MODEL, 3 * SC_WIDTH), lambda i, l: (l[0], 0, 0)),
                      pl.BlockSpec((None, SC_CONV, SC_WIDTH), lambda i, l: (l[0], 0, 0))],
            out_specs=pl.BlockSpec((tm, SC_WIDTH), lambda i, l: (i, 0)),
            scratch_shapes=[pltpu.VMEM((tm + 8, SC_WIDTH), _F32)]),
        out_shape=jax.ShapeDtypeStruct((S, SC_WIDTH), _MXU),
        compiler_params=_params(),
    )(lid, x, w_conv_in, w_conv)


_GLA_IN = 2 * GLA_QK + 2 * GLA_VW + LANES


def _gla_kernel(lid, x_ref, w_ref, wg_ref, bg_ref, ng_ref, o_ref, st_scr):
    tm = x_ref.shape[0]
    C = GLA_CHUNK

    @pl.when(pl.program_id(0) == 0)
    def _():
        st_scr[...] = jnp.zeros(st_scr.shape, _F32)

    h = _dot(x_ref[...].astype(_MXU), w_ref[...])
    q = h[:, :GLA_QK]
    k = h[:, GLA_QK:2 * GLA_QK]
    v = h[:, 2 * GLA_QK:2 * GLA_QK + GLA_VW]
    r = h[:, 2 * GLA_QK + GLA_VW:2 * GLA_QK + 2 * GLA_VW]
    g_lr = h[:, 2 * GLA_QK + 2 * GLA_VW:]
    z = _dot(g_lr.astype(_MXU), wg_ref[...]) + bg_ref[...]
    log_a = (jnp.minimum(z, 0.0) - jnp.log1p(jnp.exp(-jnp.abs(z)))) * (1.0 / GLA_TAU)

    row_in_chunk = lax.broadcasted_iota(jnp.int32, log_a.shape, 0) & (C - 1)
    b = log_a
    d = 1
    while d < C:
        b = b + jnp.where(row_in_chunk >= d, pltpu.roll(b, d, 0), 0.0)
        d *= 2

    ri = lax.broadcasted_iota(jnp.int32, (C, C), 0)
    ci = lax.broadcasted_iota(jnp.int32, (C, C), 1)
    causal = ci <= ri
    scale = GLA_DK ** -0.5
    ng = ng_ref[...]
    for c in range(tm // C):
        rows = slice(c * C, (c + 1) * C)
        bc = b[rows]
        b_last = bc[C - 1:C, :]
        kc = k[rows]
        q_e = (q[rows] * scale) * jnp.exp(bc)
        k_e = kc * jnp.exp(-bc)
        k_t = kc * jnp.exp(b_last - bc)
        decay = jnp.exp(b_last)
        for hh in range(GLA_HEADS):
            ks = slice(hh * GLA_DK, (hh + 1) * GLA_DK)
            vs = slice(hh * GLA_DV, (hh + 1) * GLA_DV)
            qh = q_e[:, ks].astype(_MXU)
            vh = v[rows, vs].astype(_MXU)
            a = jnp.where(causal, _dot_nt(qh, k_e[:, ks].astype(_MXU)), 0.0)
            st = st_scr[hh]
            o = _dot(a.astype(_MXU), vh) + _dot_nt(qh, st.astype(_MXU))
            st_scr[hh] = st * decay[:, ks] + _dot_tn(vh, k_t[:, ks].astype(_MXU))
            rr = r[rows, vs]
            o_ref[rows, vs] = (_rms_norm(o, ng) * (rr * _sigmoid(rr))).astype(o_ref.dtype)


def _gla_branch(lid, x, w_gla_in, w_gate, b_gate, norm_g):
    S = x.shape[0]
    tm = ROW_TILE
    return pl.pallas_call(
        _gla_kernel,
        name="gla_branch",
        grid_spec=pltpu.PrefetchScalarGridSpec(
            num_scalar_prefetch=1,
            grid=(S // tm,),
            in_specs=[pl.BlockSpec((tm, D_MODEL), lambda i, l: (i, 0)),
                      pl.BlockSpec((None, D_MODEL, _GLA_IN), lambda i, l: (l[0], 0, 0)),
                      pl.BlockSpec((None, LANES, GLA_QK), lambda i, l: (l[0], 0, 0)),
                      pl.BlockSpec((None, 1, GLA_QK), lambda i, l: (l[0], 0, 0)),
                      pl.BlockSpec((None, 1, GLA_DV), lambda i, l: (l[0], 0, 0))],
            out_specs=pl.BlockSpec((tm, GLA_VW), lambda i, l: (i, 0)),
            scratch_shapes=[pltpu.VMEM((GLA_HEADS, GLA_DV, GLA_DK), _F32)]),
        out_shape=jax.ShapeDtypeStruct((S, GLA_VW), _MXU),
        compiler_params=_params(),
    )(lid, x, w_gla_in, w_gate, b_gate, norm_g)


_MLA_IN = MLA_Q_RANK + MLA_KV_RANK + LANES


def _mla_prep_kernel(lid, x_ref, w_ref, qg_ref, kvg_ref, wuq_ref, wk_ref, wv_ref, ct_ref, st_ref,
                     q_o, k_o, v_o):
    h = _dot(x_ref[...].astype(_MXU), w_ref[...])
    c_q = _rms_norm(h[:, :MLA_Q_RANK], qg_ref[...]).astype(_MXU)
    c_kv = _rms_norm(h[:, MLA_Q_RANK:MLA_Q_RANK + MLA_KV_RANK], kvg_ref[...]).astype(_MXU)
    k_rope_raw = h[:, MLA_Q_RANK + MLA_KV_RANK:]
    q = _dot(c_q, wuq_ref[...])
    k_nope = _dot(c_kv, wk_ref[...])
    v = _dot(c_kv, wv_ref[...])
    cos_t = ct_ref[...]
    sin_t = st_ref[...]
    lane = lax.broadcasted_iota(jnp.int32, cos_t.shape, 1)
    half = MLA_ROPE // 2

    def rope(xr):
        rot = jnp.where(lane < half, pltpu.roll(xr, LANES - half, 1), pltpu.roll(xr, half, 1))
        return xr * cos_t + rot * sin_t

    k_rope = rope(k_rope_raw)
    scale = MLA_QK ** -0.5 * LOG2_E
    for hh in range(MLA_HEADS):
        base = hh * MLA_QK_PAD
        q_h = jnp.concatenate([q[:, base:base + MLA_NOPE], rope(q[:, base + MLA_NOPE:base + MLA_QK_PAD])],
                              axis=1)
        q_o[hh] = (q_h * scale).astype(q_o.dtype)
        k_o[hh] = jnp.concatenate([k_nope[:, hh * MLA_NOPE:(hh + 1) * MLA_NOPE], k_rope],
                                  axis=1).astype(k_o.dtype)
        v_o[hh] = v[:, hh * MLA_V:(hh + 1) * MLA_V].astype(v_o.dtype)


def _mla_prep(lid, x, w_mla_in, q_norm_g, kv_norm_g, w_uq, w_uk, w_uv, cos_t, sin_t):
    S = x.shape[0]
    tm = ROW_TILE
    H = MLA_HEADS
    return pl.pallas_call(
        _mla_prep_kernel,
        name="mla_prep",
        grid_spec=pltpu.PrefetchScalarGridSpec(
            num_scalar_prefetch=1,
            grid=(S // tm,),
            in_specs=[pl.BlockSpec((tm, D_MODEL), lambda i, l: (i, 0)),
                      pl.BlockSpec((None, D_MODEL, _MLA_IN), lambda i, l: (l[0], 0, 0)),
                      pl.BlockSpec((None, 1, MLA_Q_RANK), lambda i, l: (l[0], 0, 0)),
                      pl.BlockSpec((None, 1, MLA_KV_RANK), lambda i, l: (l[0], 0, 0)),
                      pl.BlockSpec((None, MLA_Q_RANK, H * MLA_QK_PAD), lambda i, l: (l[0], 0, 0)),
                      pl.BlockSpec((None, MLA_KV_RANK, H * MLA_NOPE), lambda i, l: (l[0], 0, 0)),
                      pl.BlockSpec((None, MLA_KV_RANK, H * MLA_V), lambda i, l: (l[0], 0, 0)),
                      pl.BlockSpec((tm, LANES), lambda i, l: (i, 0)),
                      pl.BlockSpec((tm, LANES), lambda i, l: (i, 0))],
            out_specs=[pl.BlockSpec((H, tm, MLA_QK_PAD), lambda i, l: (0, i, 0)),
                       pl.BlockSpec((H, tm, MLA_QK_PAD), lambda i, l: (0, i, 0)),
                       pl.BlockSpec((H, tm, MLA_V), lambda i, l: (0, i, 0))]),
        out_shape=[jax.ShapeDtypeStruct((H, S, MLA_QK_PAD), _MXU),
                   jax.ShapeDtypeStruct((H, S, MLA_QK_PAD), _MXU),
                   jax.ShapeDtypeStruct((H, S, MLA_V), _MXU)],
        compiler_params=_params(),
    )(lid, x, w_mla_in, q_norm_g, kv_norm_g, w_uq, w_uk, w_uv, cos_t, sin_t)


def _attn_kernel(q_ref, k_ref, v_ref, o_ref, m_scr, l_scr, acc_scr):
    tq = q_ref.shape[0]
    tk = ATTN_TK
    rb = ATTN_ROWS
    qi = pl.program_id(1)
    m_scr[...] = jnp.full(m_scr.shape, -jnp.inf, _F32)
    l_scr[...] = jnp.zeros(l_scr.shape, _F32)
    acc_scr[...] = jnp.zeros(acc_scr.shape, _F32)

    def tile(start, diagonal):
        for r in range(tq // rb):
            rows = slice(r * rb, (r + 1) * rb)
            n_cols = min(tk, (r + 1) * rb) if diagonal else tk
            k = k_ref[pl.ds(start, n_cols), :]
            v = v_ref[pl.ds(start, n_cols), :]
            s = _dot_nt(q_ref[rows, :], k)
            if diagonal:
                ri = lax.broadcasted_iota(jnp.int32, s.shape, 0) + r * rb
                ci = lax.broadcasted_iota(jnp.int32, s.shape, 1)
                s = jnp.where(ci <= ri, s, -jnp.inf)
            m_prev = m_scr[rows, :]
            m_new = jnp.maximum(m_prev, jnp.max(s, axis=1, keepdims=True))
            alpha = jnp.exp2(m_prev - m_new)
            p = jnp.exp2(s - m_new[:, :1])
            l_scr[rows, :] = alpha * l_scr[rows, :] + jnp.sum(p, axis=1, keepdims=True)
            acc_scr[rows, :] = alpha * acc_scr[rows, :] + _dot(p.astype(v.dtype), v)
            m_scr[rows, :] = m_new

    def body(j, carry):
        tile(pl.multiple_of(j * tk, tk), False)
        return carry

    lax.fori_loop(0, qi, body, 0)
    tile(pl.multiple_of(qi * tq, tq), True)
    o_ref[...] = (acc_scr[...] / l_scr[...]).astype(o_ref.dtype)


def _attention(q, k, v):
    H, S, _ = q.shape
    tq = ATTN_TQ
    return pl.pallas_call(
        _attn_kernel,
        name="mla_attention",
        grid=(H, S // tq),
        in_specs=[pl.BlockSpec((None, tq, MLA_QK_PAD), lambda h, i: (h, i, 0)),
                  pl.BlockSpec((None, S, MLA_QK_PAD), lambda h, i: (h, 0, 0)),
                  pl.BlockSpec((None, S, MLA_V), lambda h, i: (h, 0, 0))],
        out_specs=pl.BlockSpec((tq, MLA_V), lambda h, i: (i, h)),
        out_shape=jax.ShapeDtypeStruct((S, MLA_VW), _MXU),
        scratch_shapes=[pltpu.VMEM((tq, LANES), _F32),
                        pltpu.VMEM((tq, LANES), _F32),
                        pltpu.VMEM((tq, MLA_V), _F32)],
        compiler_params=_params(2),
    )(q, k, v)


def _merge_kernel(lid, x_ref, ya_ref, yb_ref, yc_ref, wgt_ref, wbr_ref, wo_ref, g_ref, b_ref,
                  wr_ref, br_ref, tril_ref, x_o, xt_o, route_o, cnt_o, cnt_scr):
    tm = x_ref.shape[0]

    @pl.when(pl.program_id(0) == 0)
    def _():
        cnt_scr[...] = jnp.zeros(cnt_scr.shape, _F32)

    x = x_ref[...]
    gates = _dot(x.astype(_MXU), wgt_ref[...])
    wbr = wbr_ref
    merged = (_sigmoid(gates[:, :D_MODEL]) * _dot(ya_ref[...], wbr[0:SC_WIDTH, :])
              + _sigmoid(gates[:, D_MODEL:2 * D_MODEL])
              * _dot(yb_ref[...], wbr[SC_WIDTH:SC_WIDTH + GLA_VW, :])
              + _sigmoid(gates[:, 2 * D_MODEL:]) * _dot(yc_ref[...], wbr[SC_WIDTH + GLA_VW:, :]))
    x1 = _layer_norm(DN_ALPHA * x + _dot(merged.astype(_MXU), wo_ref[...]), g_ref[...], b_ref[...])
    x_o[...] = x1
    for c in range(TOKEN_ROWS):
        xt_o[pl.ds(c, tm, stride=TOKEN_ROWS), :] = x1[:, c * LANES:(c + 1) * LANES]

    logits = jnp.dot(x1, wr_ref[...], preferred_element_type=_F32,
                     precision=lax.Precision.HIGHEST) + br_ref[...]
    lane = lax.broadcasted_iota(jnp.int32, logits.shape, 1)
    neg = -jnp.inf
    gl = jnp.where(lane < N_GROUPS, logits, neg)
    g_max = jnp.max(gl, axis=1, keepdims=True)
    g_top = jnp.min(jnp.where(gl == g_max, lane, LANES), axis=1, keepdims=True)
    p_g = 1.0 / jnp.sum(jnp.where(lane < N_GROUPS, jnp.exp(logits - g_max), 0.0), axis=1, keepdims=True)
    lo = N_GROUPS + g_top * EXPERTS_PER_GROUP
    sl = jnp.where((lane >= lo) & (lane < lo + EXPERTS_PER_GROUP), logits, neg)
    v0 = jnp.max(sl, axis=1, keepdims=True)
    i0 = jnp.min(jnp.where(sl == v0, lane, LANES), axis=1, keepdims=True)
    sl = jnp.where(lane == i0, neg, sl)
    v1 = jnp.max(sl, axis=1, keepdims=True)
    i1 = jnp.min(jnp.where(sl == v1, lane, LANES), axis=1, keepdims=True)
    e1 = jnp.exp(v1 - v0)
    w0 = p_g / (1.0 + e1)
    w1 = p_g * e1 / (1.0 + e1)
    oh0 = lane == i0
    oh1 = lane == i1
    ohs = jnp.where(oh0, 1.0, jnp.where(oh1, 1.0, 0.0))
    before = _dot(tril_ref[...], ohs.astype(_MXU)) + cnt_scr[0:1, :]
    rank0 = jnp.sum(jnp.where(oh0, before, 0.0), axis=1, keepdims=True)
    rank1 = jnp.sum(jnp.where(oh1, before, 0.0), axis=1, keepdims=True)
    cnt_scr[...] = cnt_scr[...] + jnp.sum(ohs, axis=0, keepdims=True)
    cnt_o[...] = cnt_scr[...]
    route_o[...] = jnp.where(
        lane == 0, (i0 - N_GROUPS).astype(_F32),
        jnp.where(lane == 1, (i1 - N_GROUPS).astype(_F32),
                  jnp.where(lane == 2, w0,
                            jnp.where(lane == 3, w1,
                                      jnp.where(lane == 4, rank0, jnp.where(lane == 5, rank1, 0.0))))))


def _merge(lid, x, y_a, y_b, y_c, w_gt, w_br, w_o, ln_g, ln_b, w_route, b_route):
    S = x.shape[0]
    tm = ROW_TILE
    tril = jnp.tril(jnp.ones((tm, tm), _F32), -1).astype(_MXU)
    br_w = SC_WIDTH + GLA_VW + MLA_VW
    return pl.pallas_call(
        _merge_kernel,
        name="merge_route",
        grid_spec=pltpu.PrefetchScalarGridSpec(
            num_scalar_prefetch=1,
            grid=(S // tm,),
            in_specs=[pl.BlockSpec((tm, D_MODEL), lambda i, l: (i, 0)),
                      pl.BlockSpec((tm, SC_WIDTH), lambda i, l: (i, 0)),
                      pl.BlockSpec((tm, GLA_VW), lambda i, l: (i, 0)),
                      pl.BlockSpec((tm, MLA_VW), lambda i, l: (i, 0)),
                      pl.BlockSpec((None, D_MODEL, 3 * D_MODEL), lambda i, l: (l[0], 0, 0)),
                      pl.BlockSpec((None, br_w, D_MODEL), lambda i, l: (l[0], 0, 0)),
                      pl.BlockSpec((None, D_MODEL, D_MODEL), lambda i, l: (l[0], 0, 0)),
                      pl.BlockSpec((None, 1, D_MODEL), lambda i, l: (l[0], 0, 0)),
                      pl.BlockSpec((None, 1, D_MODEL), lambda i, l: (l[0], 0, 0)),
                      pl.BlockSpec((None, D_MODEL, LANES), lambda i, l: (l[0], 0, 0)),
                      pl.BlockSpec((None, 1, LANES), lambda i, l: (l[0], 0, 0)),
                      pl.BlockSpec((tm, tm), lambda i, l: (0, 0))],
            out_specs=[pl.BlockSpec((tm, D_MODEL), lambda i, l: (i, 0)),
                       pl.BlockSpec((tm * TOKEN_ROWS, LANES), lambda i, l: (i, 0)),
                       pl.BlockSpec((tm, LANES), lambda i, l: (i, 0)),
                       pl.BlockSpec((8, LANES), lambda i, l: (0, 0))],
            scratch_shapes=[pltpu.VMEM((8, LANES), _F32)]),
        out_shape=[jax.ShapeDtypeStruct((S, D_MODEL), _F32),
                   jax.ShapeDtypeStruct((S * TOKEN_ROWS, LANES), _F32),
                   jax.ShapeDtypeStruct((S, LANES), _F32),
                   jax.ShapeDtypeStruct((8, LANES), _F32)],
        compiler_params=_params(),
    )(lid, x, y_a, y_b, y_c, w_gt, w_br, w_o, ln_g, ln_b, w_route, b_route, tril)


def _dispatch_plan(route, counts_row, n_tokens):
    counts = counts_row[0, N_GROUPS:N_GROUPS + N_EXPERTS].astype(jnp.int32)
    padded = (counts + MOE_BM - 1) // MOE_BM * MOE_BM
    pad_end = jnp.cumsum(padded)
    pad_start = pad_end - padded
    expert_id = route[:, :TOP_K].astype(jnp.int32)
    rank = route[:, 4:4 + TOP_K].astype(jnp.int32)
    dest = (pad_start[expert_id] + rank).reshape(n_tokens * TOP_K)
    nb = n_tokens * TOP_K // MOE_BM + N_EXPERTS
    block_expert = jnp.minimum(
        jnp.searchsorted(pad_end, jnp.arange(nb, dtype=jnp.int32) * MOE_BM, side='right'),
        N_EXPERTS - 1).astype(jnp.int32)
    n_active = (pad_end[-1] // MOE_BM).astype(jnp.int32)
    first = jnp.concatenate([jnp.ones((1,), jnp.int32),
                             (block_expert[1:] != block_expert[:-1]).astype(jnp.int32)])
    meta = jnp.concatenate([n_active.reshape(1), jnp.zeros((1,), jnp.int32)])
    return dest, counts, pad_start.astype(jnp.int32), block_expert, first, meta


def _token_copy(src_hbm, src_row, dst_ref, dst_row, sem):
    return pltpu.make_async_copy(src_hbm.at[pl.ds(src_row * TOKEN_ROWS, TOKEN_ROWS), :],
                                 dst_ref.at[pl.ds(dst_row * TOKEN_ROWS, TOKEN_ROWS), :], sem)


def _dispatch_kernel(cnt, pstart, meta, dest_ref, x_ref, xs_hbm, zero_scr, sem):
    i = pl.program_id(0)
    tt = DISPATCH_TILE
    blk = MOE_BM * TOKEN_ROWS
    n_blocks = xs_hbm.shape[0] // blk

    def spare_blocks(fn):
        def one(b, c):
            fn(pltpu.make_async_copy(zero_scr, xs_hbm.at[pl.ds(b * blk, blk), :], sem.at[2]))
            return c

        lax.fori_loop(meta[0], n_blocks, one, 0)

    @pl.when(i == 0)
    def _():
        zero_scr[...] = jnp.zeros(zero_scr.shape, _F32)
        spare_blocks(lambda cp: cp.start())

    def start(j, c):
        for kk in range(TOP_K):
            _token_copy(x_ref, j, xs_hbm, dest_ref[0, 0, TOP_K * j + kk], sem.at[0]).start()
        return c

    lax.fori_loop(0, tt, start, 0, unroll=8)

    def pad_rows(fn):
        def per_expert(e, c):
            n = cnt[e]
            n_pad = (n + MOE_BM - 1) // MOE_BM * MOE_BM

            def one(r, c2):
                fn(_token_copy(x_ref, 0, xs_hbm, pstart[e] + r, sem.at[1]))
                return c2

            lax.fori_loop(n, n_pad, one, 0)
            return c

        lax.fori_loop(0, N_EXPERTS, per_expert, 0)

    @pl.when(i == 0)
    def _():
        pad_rows(lambda cp: cp.start())
        pad_rows(lambda cp: cp.wait())
        spare_blocks(lambda cp: cp.wait())

    def wait(j, c):
        for kk in range(TOP_K):
            _token_copy(x_ref, 0, xs_hbm, 0, sem.at[0]).wait()
        return c

    lax.fori_loop(0, tt, wait, 0, unroll=8)


def _dispatch(counts, pad_start, meta, dest, x_tok, n_rows):
    n_tokens = x_tok.shape[0] // TOKEN_ROWS
    tt = DISPATCH_TILE
    return pl.pallas_call(
        _dispatch_kernel,
        name="moe_dispatch",
        grid_spec=pltpu.PrefetchScalarGridSpec(
            num_scalar_prefetch=3,
            grid=(n_tokens // tt,),
            in_specs=[pl.BlockSpec((1, 1, TOP_K * tt), lambda i, c, s, m: (i, 0, 0),
                                   memory_space=pltpu.SMEM),
                      pl.BlockSpec((tt * TOKEN_ROWS, LANES), lambda i, c, s, m: (i, 0))],
            out_specs=pl.BlockSpec(memory_space=pl.ANY),
            scratch_shapes=[pltpu.VMEM((MOE_BM * TOKEN_ROWS, LANES), _F32),
                            pltpu.SemaphoreType.DMA((3,))]),
        out_shape=jax.ShapeDtypeStruct((n_rows * TOKEN_ROWS, LANES), _F32),
        compiler_params=_params(),
    )(counts, pad_start, meta, dest.reshape(n_tokens // tt, 1, TOP_K * tt), x_tok)


def _from_token_tiles(ref, n):
    return jnp.concatenate([ref[pl.ds(c, n, stride=TOKEN_ROWS), :] for c in range(TOKEN_ROWS)], axis=1)


def _expert_kernel(lid, bexp, first, meta, xs_ref, wg_ref, wu_ref, wd_ref, o_ref, wg_b, wu_b, wd_b):
    b = pl.program_id(0)

    @pl.when(b >= meta[0])
    def _():
        o_ref[...] = jnp.zeros(o_ref.shape, o_ref.dtype)

    @pl.when(b < meta[0])
    def _():
        @pl.when(first[b] == 1)
        def _():
            wg_b[...] = wg_ref[...].astype(_MXU)
            wu_b[...] = wu_ref[...].astype(_MXU)
            wd_b[...] = wd_ref[...].astype(_MXU)

        xb = _from_token_tiles(xs_ref, MOE_BM).astype(_MXU)
        hg = _dot(xb, wg_b[...])
        hu = _dot(xb, wu_b[...])
        hid = (hg * _sigmoid(hg)) * hu
        y = _dot(hid.astype(_MXU), wd_b[...])
        for c in range(TOKEN_ROWS):
            o_ref[pl.ds(c, MOE_BM, stride=TOKEN_ROWS), :] = y[:, c * LANES:(c + 1) * LANES]


def _experts(lid, block_expert, first, meta, xs_tok, w_gate, w_up, w_down):
    rows = MOE_BM * TOKEN_ROWS
    nb = xs_tok.shape[0] // rows
    live = lambda b, m: jnp.minimum(b, m[0] - 1)
    return pl.pallas_call(
        _expert_kernel,
        name="moe_experts",
        grid_spec=pltpu.PrefetchScalarGridSpec(
            num_scalar_prefetch=4,
            grid=(nb,),
            in_specs=[pl.BlockSpec((rows, LANES), lambda b, l, e, f, m: (live(b, m), 0)),
                      pl.BlockSpec((None, None, D_MODEL, EXPERT_HIDDEN),
                                   lambda b, l, e, f, m: (l[0], e[b], 0, 0)),
                      pl.BlockSpec((None, None, D_MODEL, EXPERT_HIDDEN),
                                   lambda b, l, e, f, m: (l[0], e[b], 0, 0)),
                      pl.BlockSpec((None, None, EXPERT_HIDDEN, D_MODEL),
                                   lambda b, l, e, f, m: (l[0], e[b], 0, 0))],
            out_specs=pl.BlockSpec((rows, LANES), lambda b, l, e, f, m: (b, 0)),
            scratch_shapes=[pltpu.VMEM((D_MODEL, EXPERT_HIDDEN), _MXU),
                            pltpu.VMEM((D_MODEL, EXPERT_HIDDEN), _MXU),
                            pltpu.VMEM((EXPERT_HIDDEN, D_MODEL), _MXU)]),
        out_shape=jax.ShapeDtypeStruct(xs_tok.shape, _F32),
        compiler_params=_params(),
    )(lid, block_expert, first, meta, xs_tok, w_gate, w_up, w_down)


def _combine_kernel(lid, pos_ref, pos_next_ref, y_hbm, x_ref, route_ref, g2_ref, b2_ref, wpg_ref, bpg_ref,
                    p_ref, wpu_ref, g3_ref, b3_ref, o_ref, ybuf, sem):
    tc = x_ref.shape[0]
    i = pl.program_id(0)
    slot = lax.rem(i, 2)

    def gather(rows_ref, s, fn):
        def body(r, c):
            for kk in range(TOP_K):
                fn(_token_copy(y_hbm, rows_ref[0, 0, TOP_K * r + kk], ybuf.at[s, kk], r, sem.at[s]))
            return c

        lax.fori_loop(0, tc, body, 0, unroll=8)

    @pl.when(i == 0)
    def _():
        gather(pos_ref, 0, lambda cp: cp.start())

    @pl.when(i + 1 < pl.num_programs(0))
    def _():
        gather(pos_next_ref, 1 - slot, lambda cp: cp.start())

    gather(pos_ref, slot, lambda cp: cp.wait())
    route = route_ref[...]
    y = (route[:, 2:3] * _from_token_tiles(ybuf.at[slot, 0], tc)
         + route[:, 3:4] * _from_token_tiles(ybuf.at[slot, 1], tc))
    x2 = _layer_norm(DN_ALPHA * x_ref[...] + y, g2_ref[...], b2_ref[...])
    gate = _sigmoid(_dot(x2.astype(_MXU), wpg_ref[...]) + bpg_ref[...])
    up = _dot(p_ref[...].astype(_MXU), wpu_ref[...])
    o_ref[...] = _layer_norm(DN_ALPHA * x2 + gate * up, g3_ref[...], b3_ref[...])


def _combine(lid, dest, y_tok, x, route, ln2_g, ln2_b, w_pg, b_pg, p, w_pu, ln3_g, ln3_b):
    S = x.shape[0]
    tc = COMBINE_TILE
    n_tiles = S // tc
    pos = dest.reshape(n_tiles, 1, TOP_K * tc)
    vec = lambda n: pl.BlockSpec((None, 1, n), lambda i, l: (l[0], 0, 0))
    return pl.pallas_call(
        _combine_kernel,
        name="combine_ple",
        grid_spec=pltpu.PrefetchScalarGridSpec(
            num_scalar_prefetch=1,
            grid=(S // tc,),
            in_specs=[pl.BlockSpec((1, 1, TOP_K * tc), lambda i, l: (i, 0, 0), memory_space=pltpu.SMEM),
                      pl.BlockSpec((1, 1, TOP_K * tc), lambda i, l: (jnp.minimum(i + 1, n_tiles - 1), 0, 0),
                                   memory_space=pltpu.SMEM),
                      pl.BlockSpec(memory_space=pl.ANY),
                      pl.BlockSpec((tc, D_MODEL), lambda i, l: (i, 0)),
                      pl.BlockSpec((tc, LANES), lambda i, l: (i, 0)),
                      vec(D_MODEL), vec(D_MODEL),
                      pl.BlockSpec((None, D_MODEL, D_MODEL), lambda i, l: (l[0], 0, 0)),
                      vec(D_MODEL),
                      pl.BlockSpec((None, tc, PLE_DIM), lambda i, l: (l[0], i, 0)),
                      pl.BlockSpec((None, PLE_DIM, D_MODEL), lambda i, l: (l[0], 0, 0)),
                      vec(D_MODEL), vec(D_MODEL)],
            out_specs=pl.BlockSpec((tc, D_MODEL), lambda i, l: (i, 0)),
            scratch_shapes=[pltpu.VMEM((2, TOP_K, tc * TOKEN_ROWS, LANES), _F32),
                            pltpu.SemaphoreType.DMA((2,))]),
        out_shape=jax.ShapeDtypeStruct((S, D_MODEL), _F32),
        compiler_params=_params(),
    )(lid, pos, pos, y_tok, x, route, ln2_g, ln2_b, w_pg, b_pg, p, w_pu, ln3_g, ln3_b)


def _split_in_proj(w_in):
    sizes = (SC_WIDTH, SC_WIDTH, SC_WIDTH, GLA_QK, GLA_QK, GLA_VW, GLA_GATE_RANK, GLA_VW,
             MLA_Q_RANK, MLA_KV_RANK, MLA_ROPE, D_MODEL, D_MODEL, D_MODEL)
    offs = [0]
    for s in sizes:
        offs.append(offs[-1] + s)
    col = lambda j: w_in[:, :, offs[j]:offs[j + 1]]
    L = w_in.shape[0]
    zeros = lambda n: jnp.zeros((L, D_MODEL, n), w_in.dtype)
    w_conv_in = jnp.concatenate([col(0), col(1), col(2)], axis=-1)
    w_gla_in = jnp.concatenate([col(3), col(4), col(5), col(7), col(6), zeros(LANES - GLA_GATE_RANK)], axis=-1)
    w_mla_in = jnp.concatenate([col(8), col(9), col(10), zeros(LANES - MLA_ROPE)], axis=-1)
    w_gates = jnp.concatenate([col(11), col(12), col(13)], axis=-1)
    return [w.astype(_MXU) for w in (w_conv_in, w_gla_in, w_mla_in, w_gates)]


def kernel(x, p, positions, ln0_g, ln0_b, w_in, w_conv, w_gla_gate, b_gla_gate, gla_norm_g, mla_q_norm_g, mla_kv_norm_g, w_uq, w_ukv, w_br, w_o, ln1_g, ln1_b, w_grp, b_grp, w_exp, b_exp, w_gate, w_up, w_down, ln2_g, ln2_b, w_ple_gate, b_ple_gate, w_ple_up, ln3_g, ln3_b):
    B, S, D = x.shape
    L = w_in.shape[0]
    assert B == 1 and D == D_MODEL and S % ROW_TILE == 0 and S % ATTN_TQ == 0
    H = MLA_HEADS

    w_conv_in, w_gla_in, w_mla_in, w_gates = _split_in_proj(w_in)
    w_gg = jnp.pad(w_gla_gate, ((0, 0), (0, LANES - GLA_GATE_RANK), (0, 0))).astype(_MXU)
    w_uq_p = jnp.pad(w_uq.reshape(L, MLA_Q_RANK, H, MLA_QK),
                     ((0, 0), (0, 0), (0, 0), (0, MLA_QK_PAD - MLA_QK))
                     ).reshape(L, MLA_Q_RANK, H * MLA_QK_PAD).astype(_MXU)
    w_ukv4 = w_ukv.reshape(L, MLA_KV_RANK, H, MLA_NOPE + MLA_V)
    w_uk = w_ukv4[..., :MLA_NOPE].reshape(L, MLA_KV_RANK, H * MLA_NOPE).astype(_MXU)
    w_uv = w_ukv4[..., MLA_NOPE:].reshape(L, MLA_KV_RANK, H * MLA_V).astype(_MXU)
    w_route = jnp.concatenate(
        [w_grp, w_exp.reshape(L, D, N_EXPERTS),
         jnp.zeros((L, D, LANES - N_GROUPS - N_EXPERTS), _F32)], axis=-1)
    b_route = jnp.concatenate(
        [b_grp, b_exp.reshape(L, N_EXPERTS),
         jnp.zeros((L, LANES - N_GROUPS - N_EXPERTS), _F32)], axis=-1).reshape(L, 1, LANES)
    w_br_c = w_br.astype(_MXU)
    w_o_c = w_o.astype(_MXU)
    w_pg_c = w_ple_gate.astype(_MXU)
    w_pu_c = w_ple_up.astype(_MXU)
    row = lambda a: a.reshape(L, 1, a.shape[-1])

    cos_t, sin_t = _rope_tables(positions.reshape(S))
    x0 = _entry_norm(x.reshape(S, D), ln0_g, ln0_b)
    p3 = p.reshape(L, S, PLE_DIM)

    def layer(xc, i):
        lid = jnp.reshape(i, (1,)).astype(jnp.int32)
        y_a = _conv_branch(lid, xc, w_conv_in, w_conv)
        y_b = _gla_branch(lid, xc, w_gla_in, w_gg, row(b_gla_gate), row(gla_norm_g))
        q, k, v = _mla_prep(lid, xc, w_mla_in, row(mla_q_norm_g), row(mla_kv_norm_g),
                            w_uq_p, w_uk, w_uv, cos_t, sin_t)
        y_c = _attention(q, k, v)
        x1, x1_tok, route, counts_row = _merge(lid, xc, y_a, y_b, y_c, w_gates, w_br_c, w_o_c,
                                               row(ln1_g), row(ln1_b), w_route, b_route)
        dest, counts, pad_start, block_expert, first, meta = _dispatch_plan(route, counts_row, S)
        n_rows = (S * TOP_K // MOE_BM + N_EXPERTS) * MOE_BM
        xs_tok = _dispatch(counts, pad_start, meta, dest, x1_tok, n_rows)
        y_tok = _experts(lid, block_expert, first, meta, xs_tok, w_gate, w_up, w_down)
        x3 = _combine(lid, dest, y_tok, x1, route, row(ln2_g), row(ln2_b), w_pg_c, row(b_ple_gate),
                      p3, w_pu_c, row(ln3_g), row(ln3_b))
        return x3, None

    out, _ = lax.scan(layer, x0, jnp.arange(L, dtype=jnp.int32))
    return out.reshape(B, S, D)
```

```python
import functools

import jax
import jax.numpy as jnp
from jax import lax
from jax.experimental import pallas as pl
from jax.experimental.pallas import tpu as pltpu

D_MODEL = 1024
DEPTH = 4
PLE_DIM = 256
SC_WIDTH = 512
SC_CONV = 3
GLA_HEADS = 4
GLA_DK = 64
GLA_DV = 128
GLA_GATE_RANK = 16
GLA_TAU = 16.0
GLA_CHUNK = 64
MLA_HEADS = 4
MLA_NOPE = 128
MLA_ROPE = 64
MLA_V = 128
MLA_Q_RANK = 256
MLA_KV_RANK = 128
ROPE_THETA = 10000.0
N_GROUPS = 8
EXPERTS_PER_GROUP = 8
N_EXPERTS = N_GROUPS * EXPERTS_PER_GROUP
TOP_K = 2
EXPERT_HIDDEN = 256
DN_ALPHA = (2 * DEPTH) ** 0.25
LN_EPS = 1e-5
RMS_EPS = 1e-6

GLA_QK = GLA_HEADS * GLA_DK
GLA_VW = GLA_HEADS * GLA_DV
MLA_QK = MLA_NOPE + MLA_ROPE
MLA_VW = MLA_HEADS * MLA_V

LANES = 128
MLA_QK_PAD = 2 * LANES
VMEM_LIMIT = 56 * 1024 * 1024

ROW_TILE = 512
ATTN_TQ = 1024
ATTN_TK = 512
ATTN_ROWS = 256
LOG2_E = 1.4426950408889634
MOE_BM = 256
COMBINE_TILE = 256
DISPATCH_TILE = 1024
TOKEN_ROWS = D_MODEL // LANES

_MXU = jnp.bfloat16
_F32 = jnp.float32


def _dot(a, b):
    return jnp.dot(a, b, preferred_element_type=_F32)


def _dot_nt(a, b):
    return lax.dot_general(a, b, (((1,), (1,)), ((), ())), preferred_element_type=_F32)


def _dot_tn(a, b):
    return lax.dot_general(a, b, (((0,), (0,)), ((), ())), preferred_element_type=_F32)


def _layer_norm(x, g, b):
    mu = jnp.mean(x, axis=-1, keepdims=True)
    xc = x - mu
    var = jnp.mean(xc * xc, axis=-1, keepdims=True)
    return xc * lax.rsqrt(var + LN_EPS) * g + b


def _rms_norm(x, g):
    return x * lax.rsqrt(jnp.mean(x * x, axis=-1, keepdims=True) + RMS_EPS) * g


def _sigmoid(x):
    return 1.0 / (1.0 + jnp.exp(-x))


def _params(n_axes=1):
    return pltpu.CompilerParams(dimension_semantics=("arbitrary",) * n_axes,
                                vmem_limit_bytes=VMEM_LIMIT)


def _ln0_kernel(x_ref, g_ref, b_ref, o_ref):
    o_ref[...] = _layer_norm(x_ref[...], g_ref[...], b_ref[...])


def _entry_norm(x, g, b):
    S = x.shape[0]
    return pl.pallas_call(
        _ln0_kernel,
        name="entry_ln",
        grid=(S // ROW_TILE,),
        in_specs=[pl.BlockSpec((ROW_TILE, D_MODEL), lambda i: (i, 0)),
                  pl.BlockSpec((1, D_MODEL), lambda i: (0, 0)),
                  pl.BlockSpec((1, D_MODEL), lambda i: (0, 0))],
        out_specs=pl.BlockSpec((ROW_TILE, D_MODEL), lambda i: (i, 0)),
        out_shape=jax.ShapeDtypeStruct((S, D_MODEL), _F32),
        compiler_params=_params(),
    )(x, g.reshape(1, D_MODEL), b.reshape(1, D_MODEL))


def _rope_kernel(pos_ref, inv_ref, c_ref, s_ref):
    ang = pos_ref[...].astype(_F32) * inv_ref[...]
    lane = lax.broadcasted_iota(jnp.int32, ang.shape, 1)
    half = MLA_ROPE // 2
    cosv = jnp.cos(ang)
    sinv = jnp.sin(ang)
    c_ref[...] = jnp.where(lane < MLA_ROPE, cosv, 0.0)
    s_ref[...] = jnp.where(lane < half, -sinv, jnp.where(lane < MLA_ROPE, sinv, 0.0))


def _rope_tables(positions):
    S = positions.shape[0]
    half = MLA_ROPE // 2
    inv = 1.0 / (ROPE_THETA ** (jnp.arange(0, MLA_ROPE, 2, dtype=_F32) / MLA_ROPE))
    inv = jnp.tile(inv, LANES // half).reshape(1, LANES)
    return pl.pallas_call(
        _rope_kernel,
        name="rope_tables",
        grid=(S // ROW_TILE,),
        in_specs=[pl.BlockSpec((ROW_TILE, 1), lambda i: (i, 0)),
                  pl.BlockSpec((1, LANES), lambda i: (0, 0))],
        out_specs=[pl.BlockSpec((ROW_TILE, LANES), lambda i: (i, 0)),
                   pl.BlockSpec((ROW_TILE, LANES), lambda i: (i, 0))],
        out_shape=[jax.ShapeDtypeStruct((S, LANES), _F32)] * 2,
        compiler_params=_params(),
    )(positions.reshape(S, 1), inv)


def _conv_kernel(lid, x_ref, w_ref, wc_ref, o_ref, u_scr):
    tm = x_ref.shape[0]

    @pl.when(pl.program_id(0) == 0)
    def _():
        u_scr[0:8, :] = jnp.zeros((8, SC_WIDTH), _F32)

    h = _dot(x_ref[...].astype(_MXU), w_ref[...])
    a_b = h[:, :SC_WIDTH]
    u = h[:, SC_WIDTH:2 * SC_WIDTH] * h[:, 2 * SC_WIDTH:]
    u_scr[8:8 + tm, :] = u
    wc = wc_ref[...]
    y = wc[2:3, :] * u + wc[1:2, :] * u_scr[7:7 + tm, :] + wc[0:1, :] * u_scr[6:6 + tm, :]
    o_ref[...] = (a_b * y).astype(o_ref.dtype)
    u_scr[0:8, :] = u_scr[tm:tm + 8, :]


def _conv_branch(lid, x, w_conv_in, w_conv):
    S = x.shape[0]
    tm = ROW_TILE
    return pl.pallas_call(
        _conv_kernel,
        name="conv_branch",
        grid_spec=pltpu.PrefetchScalarGridSpec(
            num_scalar_prefetch=1,
            grid=(S // tm,),
            in_specs=[pl.BlockSpec((tm, D_MODEL), lambda i, l: (i, 0)),
                      pl.BlockSpec((None, D_MODEL, 3 * SC_WIDTH), lambda i, l: (l[0], 0, 0)),
                      pl.BlockSpec((None, SC_CONV, SC_WIDTH), lambda i, l: (l[0], 0, 0))],
            out_specs=pl.BlockSpec((tm, SC_WIDTH), lambda i, l: (i, 0)),
            scratch_shapes=[pltpu.VMEM((tm + 8, SC_WIDTH), _F32)]),
        out_shape=jax.ShapeDtypeStruct((S, SC_WIDTH), _MXU),
        compiler_params=_params(),
    )(lid, x, w_conv_in, w_conv)


_GLA_IN = 2 * GLA_QK + 2 * GLA_VW + LANES


def _gla_kernel(lid, x_ref, w_ref, wg_ref, bg_ref, ng_ref, o_ref, st_scr):
    tm = x_ref.shape[0]
    C = GLA_CHUNK

    @pl.when(pl.program_id(0) == 0)
    def _():
        st_scr[...] = jnp.zeros(st_scr.shape, _F32)

    h = _dot(x_ref[...].astype(_MXU), w_ref[...])
    q = h[:, :GLA_QK]
    k = h[:, GLA_QK:2 * GLA_QK]
    v = h[:, 2 * GLA_QK:2 * GLA_QK + GLA_VW]
    r = h[:, 2 * GLA_QK + GLA_VW:2 * GLA_QK + 2 * GLA_VW]
    g_lr = h[:, 2 * GLA_QK + 2 * GLA_VW:]
    z = _dot(g_lr.astype(_MXU), wg_ref[...]) + bg_ref[...]
    log_a = (jnp.minimum(z, 0.0) - jnp.log1p(jnp.exp(-jnp.abs(z)))) * (1.0 / GLA_TAU)

    row_in_chunk = lax.broadcasted_iota(jnp.int32, log_a.shape, 0) & (C - 1)
    b = log_a
    d = 1
    while d < C:
        b = b + jnp.where(row_in_chunk >= d, pltpu.roll(b, d, 0), 0.0)
        d *= 2

    ri = lax.broadcasted_iota(jnp.int32, (C, C), 0)
    ci = lax.broadcasted_iota(jnp.int32, (C, C), 1)
    causal = ci <= ri
    scale = GLA_DK ** -0.5
    ng = ng_ref[...]
    for c in range(tm // C):
        rows = slice(c * C, (c + 1) * C)
        bc = b[rows]
        b_last = bc[C - 1:C, :]
        kc = k[rows]
        q_e = (q[rows] * scale) * jnp.exp(bc)
        k_e = kc * jnp.exp(-bc)
        k_t = kc * jnp.exp(b_last - bc)
        decay = jnp.exp(b_last)
        for hh in range(GLA_HEADS):
            ks = slice(hh * GLA_DK, (hh + 1) * GLA_DK)
            vs = slice(hh * GLA_DV, (hh + 1) * GLA_DV)
            qh = q_e[:, ks].astype(_MXU)
            vh = v[rows, vs].astype(_MXU)
            a = jnp.where(causal, _dot_nt(qh, k_e[:, ks].astype(_MXU)), 0.0)
            st = st_scr[hh]
            o = _dot(a.astype(_MXU), vh) + _dot_nt(qh, st.astype(_MXU))
            st_scr[hh] = st * decay[:, ks] + _dot_tn(vh, k_t[:, ks].astype(_MXU))
            rr = r[rows, vs]
            o_ref[rows, vs] = (_rms_norm(o, ng) * (rr * _sigmoid(rr))).astype(o_ref.dtype)


def _gla_branch(lid, x, w_gla_in, w_gate, b_gate, norm_g):
    S = x.shape[0]
    tm = ROW_TILE
    return pl.pallas_call(
        _gla_kernel,
        name="gla_branch",
        grid_spec=pltpu.PrefetchScalarGridSpec(
            num_scalar_prefetch=1,
            grid=(S // tm,),
            in_specs=[pl.BlockSpec((tm, D_MODEL), lambda i, l: (i, 0)),
                      pl.BlockSpec((None, D_MODEL, _GLA_IN), lambda i, l: (l[0], 0, 0)),
                      pl.BlockSpec((None, LANES, GLA_QK), lambda i, l: (l[0], 0, 0)),
                      pl.BlockSpec((None, 1, GLA_QK), lambda i, l: (l[0], 0, 0)),
                      pl.BlockSpec((None, 1, GLA_DV), lambda i, l: (l[0], 0, 0))],
            out_specs=pl.BlockSpec((tm, GLA_VW), lambda i, l: (i, 0)),
            scratch_shapes=[pltpu.VMEM((GLA_HEADS, GLA_DV, GLA_DK), _F32)]),
        out_shape=jax.ShapeDtypeStruct((S, GLA_VW), _MXU),
        compiler_params=_params(),
    )(lid, x, w_gla_in, w_gate, b_gate, norm_g)


_MLA_IN = MLA_Q_RANK + MLA_KV_RANK + LANES


def _mla_prep_kernel(lid, x_ref, w_ref, qg_ref, kvg_ref, wuq_ref, wk_ref, wv_ref, ct_ref, st_ref,
                     q_o, k_o, v_o):
    h = _dot(x_ref[...].astype(_MXU), w_ref[...])
    c_q = _rms_norm(h[:, :MLA_Q_RANK], qg_ref[...]).astype(_MXU)
    c_kv = _rms_norm(h[:, MLA_Q_RANK:MLA_Q_RANK + MLA_KV_RANK], kvg_ref[...]).astype(_MXU)
    k_rope_raw = h[:, MLA_Q_RANK + MLA_KV_RANK:]
    q = _dot(c_q, wuq_ref[...])
    k_nope = _dot(c_kv, wk_ref[...])
    v = _dot(c_kv, wv_ref[...])
    cos_t = ct_ref[...]
    sin_t = st_ref[...]
    lane = lax.broadcasted_iota(jnp.int32, cos_t.shape, 1)
    half = MLA_ROPE // 2

    def rope(xr):
        rot = jnp.where(lane < half, pltpu.roll(xr, LANES - half, 1), pltpu.roll(xr, half, 1))
        return xr * cos_t + rot * sin_t

    k_rope = rope(k_rope_raw)
    scale = MLA_QK ** -0.5 * LOG2_E
    for hh in range(MLA_HEADS):
        base = hh * MLA_QK_PAD
        q_h = jnp.concatenate([q[:, base:base + MLA_NOPE], rope(q[:, base + MLA_NOPE:base + MLA_QK_PAD])],
                              axis=1)
        q_o[hh] = (q_h * scale).astype(q_o.dtype)
        k_o[hh] = jnp.concatenate([k_nope[:, hh * MLA_NOPE:(hh + 1) * MLA_NOPE], k_rope],
                                  axis=1).astype(k_o.dtype)
        v_o[hh] = v[:, hh * MLA_V:(hh + 1) * MLA_V].astype(v_o.dtype)


def _mla_prep(lid, x, w_mla_in, q_norm_g, kv_norm_g, w_uq, w_uk, w_uv, cos_t, sin_t):
    S = x.shape[0]
    tm = ROW_TILE
    H = MLA_HEADS
    return pl.pallas_call(
        _mla_prep_kernel,
        name="mla_prep",
        grid_spec=pltpu.PrefetchScalarGridSpec(
            num_scalar_prefetch=1,
            grid=(S // tm,),
            in_specs=[pl.BlockSpec((tm, D_MODEL), lambda i, l: (i, 0)),
                      pl.BlockSpec((None, D_MODEL, _MLA_IN), lambda i, l: (l[0], 0, 0)),
                      pl.BlockSpec((None, 1, MLA_Q_RANK), lambda i, l: (l[0], 0, 0)),
                      pl.BlockSpec((None, 1, MLA_KV_RANK), lambda i, l: (l[0], 0, 0)),
                      pl.BlockSpec((None, MLA_Q_RANK, H * MLA_QK_PAD), lambda i, l: (l[0], 0, 0)),
                      pl.BlockSpec((None, MLA_KV_RANK, H * MLA_NOPE), lambda i, l: (l[0], 0, 0)),
                      pl.BlockSpec((None, MLA_KV_RANK, H * MLA_V), lambda i, l: (l[0], 0, 0)),
                      pl.BlockSpec((tm, LANES), lambda i, l: (i, 0)),
                      pl.BlockSpec((tm, LANES), lambda i, l: (i, 0))],
            out_specs=[pl.BlockSpec((H, tm, MLA_QK_PAD), lambda i, l: (0, i, 0)),
                       pl.BlockSpec((H, tm, MLA_QK_PAD), lambda i, l: (0, i, 0)),
                       pl.BlockSpec((H, tm, MLA_V), lambda i, l: (0, i, 0))]),
        out_shape=[jax.ShapeDtypeStruct((H, S, MLA_QK_PAD), _MXU),
                   jax.ShapeDtypeStruct((H, S, MLA_QK_PAD), _MXU),
                   jax.ShapeDtypeStruct((H, S, MLA_V), _MXU)],
        compiler_params=_params(),
    )(lid, x, w_mla_in, q_norm_g, kv_norm_g, w_uq, w_uk, w_uv, cos_t, sin_t)


def _attn_kernel(q_ref, k_ref, v_ref, o_ref, s_a, s_b, m_scr, l_scr, acc_scr):
    tq = q_ref.shape[0]
    tk = ATTN_TK
    rb = ATTN_ROWS
    n_rb = tq // rb
    qi = pl.program_id(1)
    m_scr[...] = jnp.full(m_scr.shape, -jnp.inf, _F32)
    l_scr[...] = jnp.zeros(l_scr.shape, _F32)
    acc_scr[...] = jnp.zeros(acc_scr.shape, _F32)

    def scores(dst, start, first_rb=0):
        k = k_ref[pl.ds(start, tk), :]
        for r in range(first_rb, n_rb):
            rows = slice(r * rb, (r + 1) * rb)
            dst[rows, :] = _dot_nt(q_ref[rows, :], k)

    def consume(src, start, col0=None):
        for r in range(n_rb):
            row0 = r * rb
            rows = slice(row0, row0 + rb)
            n_cols = tk
            if col0 is not None:
                n_cols = min(tk, row0 + rb - col0)
                if n_cols <= 0:
                    continue
            s = src[rows, :n_cols]
            if col0 is not None and col0 + n_cols - 1 > row0:
                ri = lax.broadcasted_iota(jnp.int32, s.shape, 0) + row0
                ci = lax.broadcasted_iota(jnp.int32, s.shape, 1) + col0
                s = jnp.where(ci <= ri, s, -jnp.inf)
            v = v_ref[pl.ds(start, n_cols), :]
            m_prev = m_scr[rows, :]
            m_new = jnp.maximum(m_prev, jnp.max(s, axis=1, keepdims=True))
            alpha = jnp.exp2(m_prev - m_new)
            p = jnp.exp2(s - m_new[:, :1])
            l_scr[rows, :] = alpha * l_scr[rows, :] + jnp.sum(p, axis=1, keepdims=True)
            acc_scr[rows, :] = alpha * acc_scr[rows, :] + _dot(p.astype(v.dtype), v)
            m_scr[rows, :] = m_new

    col = lambda j: pl.multiple_of(j * tk, tk)
    scores(s_a, col(0))

    def body(i, carry):
        scores(s_b, col(2 * i + 1))
        consume(s_a, col(2 * i))
        scores(s_a, col(2 * i + 2))
        consume(s_b, col(2 * i + 1))
        return carry

    lax.fori_loop(0, qi, body, 0)
    scores(s_b, col(2 * qi + 1), first_rb=tk // rb)
    consume(s_a, col(2 * qi), col0=0)
    consume(s_b, col(2 * qi + 1), col0=tk)
    o_ref[...] = (acc_scr[...] / l_scr[...]).astype(o_ref.dtype)


def _attention(q, k, v):
    H, S, _ = q.shape
    tq = ATTN_TQ
    return pl.pallas_call(
        _attn_kernel,
        name="mla_attention",
        grid=(H, S // tq),
        in_specs=[pl.BlockSpec((None, tq, MLA_QK_PAD), lambda h, i: (h, i, 0)),
                  pl.BlockSpec((None, S, MLA_QK_PAD), lambda h, i: (h, 0, 0)),
                  pl.BlockSpec((None, S, MLA_V), lambda h, i: (h, 0, 0))],
        out_specs=pl.BlockSpec((tq, MLA_V), lambda h, i: (i, h)),
        out_shape=jax.ShapeDtypeStruct((S, MLA_VW), _MXU),
        scratch_shapes=[pltpu.VMEM((tq, ATTN_TK), _F32),
                        pltpu.VMEM((tq, ATTN_TK), _F32),
                        pltpu.VMEM((tq, LANES), _F32),
                        pltpu.VMEM((tq, LANES), _F32),
                        pltpu.VMEM((tq, MLA_V), _F32)],
        compiler_params=_params(2),
    )(q, k, v)


def _merge_kernel(lid, x_ref, ya_ref, yb_ref, yc_ref, wgt_ref, wbr_ref, wo_ref, g_ref, b_ref,
                  wr_ref, br_ref, tril_ref, x_o, xt_o, route_o, cnt_o, cnt_scr):
    tm = x_ref.shape[0]

    @pl.when(pl.program_id(0) == 0)
    def _():
        cnt_scr[...] = jnp.zeros(cnt_scr.shape, _F32)

    x = x_ref[...]
    gates = _dot(x.astype(_MXU), wgt_ref[...])
    wbr = wbr_ref
    merged = (_sigmoid(gates[:, :D_MODEL]) * _dot(ya_ref[...], wbr[0:SC_WIDTH, :])
              + _sigmoid(gates[:, D_MODEL:2 * D_MODEL])
              * _dot(yb_ref[...], wbr[SC_WIDTH:SC_WIDTH + GLA_VW, :])
              + _sigmoid(gates[:, 2 * D_MODEL:]) * _dot(yc_ref[...], wbr[SC_WIDTH + GLA_VW:, :]))
    x1 = _layer_norm(DN_ALPHA * x + _dot(merged.astype(_MXU), wo_ref[...]), g_ref[...], b_ref[...])
    x_o[...] = x1
    for c in range(TOKEN_ROWS):
        xt_o[pl.ds(c, tm, stride=TOKEN_ROWS), :] = x1[:, c * LANES:(c + 1) * LANES]

    logits = _dot(x1.astype(_MXU), wr_ref[...]) + br_ref[...]
    lane = lax.broadcasted_iota(jnp.int32, logits.shape, 1)
    neg = -jnp.inf
    gl = jnp.where(lane < N_GROUPS, logits, neg)
    g_max = jnp.max(gl, axis=1, keepdims=True)
    g_top = jnp.min(jnp.where(gl == g_max, lane, LANES), axis=1, keepdims=True)
    p_g = 1.0 / jnp.sum(jnp.where(lane < N_GROUPS, jnp.exp(logits - g_max), 0.0), axis=1, keepdims=True)
    lo = N_GROUPS + g_top * EXPERTS_PER_GROUP
    sl = jnp.where((lane >= lo) & (lane < lo + EXPERTS_PER_GROUP), logits, neg)
    v0 = jnp.max(sl, axis=1, keepdims=True)
    i0 = jnp.min(jnp.where(sl == v0, lane, LANES), axis=1, keepdims=True)
    sl = jnp.where(lane == i0, neg, sl)
    v1 = jnp.max(sl, axis=1, keepdims=True)
    i1 = jnp.min(jnp.where(sl == v1, lane, LANES), axis=1, keepdims=True)
    e1 = jnp.exp(v1 - v0)
    w0 = p_g / (1.0 + e1)
    w1 = p_g * e1 / (1.0 + e1)
    oh0 = lane == i0
    oh1 = lane == i1
    ohs = jnp.where(oh0, 1.0, jnp.where(oh1, 1.0, 0.0))
    before = _dot(tril_ref[...], ohs.astype(_MXU)) + cnt_scr[0:1, :]
    rank0 = jnp.sum(jnp.where(oh0, before, 0.0), axis=1, keepdims=True)
    rank1 = jnp.sum(jnp.where(oh1, before, 0.0), axis=1, keepdims=True)
    cnt_scr[...] = cnt_scr[...] + jnp.sum(ohs, axis=0, keepdims=True)
    cnt_o[...] = cnt_scr[...]
    route_o[...] = jnp.where(
        lane == 0, (i0 - N_GROUPS).astype(_F32),
        jnp.where(lane == 1, (i1 - N_GROUPS).astype(_F32),
                  jnp.where(lane == 2, w0,
                            jnp.where(lane == 3, w1,
                                      jnp.where(lane == 4, rank0, jnp.where(lane == 5, rank1, 0.0))))))


def _merge(lid, x, y_a, y_b, y_c, w_gt, w_br, w_o, ln_g, ln_b, w_route, b_route):
    S = x.shape[0]
    tm = ROW_TILE
    tril = jnp.tril(jnp.ones((tm, tm), _F32), -1).astype(_MXU)
    br_w = SC_WIDTH + GLA_VW + MLA_VW
    return pl.pallas_call(
        _merge_kernel,
        name="merge_route",
        grid_spec=pltpu.PrefetchScalarGridSpec(
            num_scalar_prefetch=1,
            grid=(S // tm,),
            in_specs=[pl.BlockSpec((tm, D_MODEL), lambda i, l: (i, 0)),
                      pl.BlockSpec((tm, SC_WIDTH), lambda i, l: (i, 0)),
                      pl.BlockSpec((tm, GLA_VW), lambda i, l: (i, 0)),
                      pl.BlockSpec((tm, MLA_VW), lambda i, l: (i, 0)),
                      pl.BlockSpec((None, D_MODEL, 3 * D_MODEL), lambda i, l: (l[0], 0, 0)),
                      pl.BlockSpec((None, br_w, D_MODEL), lambda i, l: (l[0], 0, 0)),
                      pl.BlockSpec((None, D_MODEL, D_MODEL), lambda i, l: (l[0], 0, 0)),
                      pl.BlockSpec((None, 1, D_MODEL), lambda i, l: (l[0], 0, 0)),
                      pl.BlockSpec((None, 1, D_MODEL), lambda i, l: (l[0], 0, 0)),
                      pl.BlockSpec((None, D_MODEL, LANES), lambda i, l: (l[0], 0, 0)),
                      pl.BlockSpec((None, 1, LANES), lambda i, l: (l[0], 0, 0)),
                      pl.BlockSpec((tm, tm), lambda i, l: (0, 0))],
            out_specs=[pl.BlockSpec((tm, D_MODEL), lambda i, l: (i, 0)),
                       pl.BlockSpec((tm * TOKEN_ROWS, LANES), lambda i, l: (i, 0)),
                       pl.BlockSpec((tm, LANES), lambda i, l: (i, 0)),
                       pl.BlockSpec((8, LANES), lambda i, l: (0, 0))],
            scratch_shapes=[pltpu.VMEM((8, LANES), _F32)]),
        out_shape=[jax.ShapeDtypeStruct((S, D_MODEL), _F32),
                   jax.ShapeDtypeStruct((S * TOKEN_ROWS, LANES), _F32),
                   jax.ShapeDtypeStruct((S, LANES), _F32),
                   jax.ShapeDtypeStruct((8, LANES), _F32)],
        compiler_params=_params(),
    )(lid, x, y_a, y_b, y_c, w_gt, w_br, w_o, ln_g, ln_b, w_route, b_route, tril)


def _dispatch_plan(route, counts_row, n_tokens):
    counts = counts_row[0, N_GROUPS:N_GROUPS + N_EXPERTS].astype(jnp.int32)
    padded = (counts + MOE_BM - 1) // MOE_BM * MOE_BM
    pad_end = jnp.cumsum(padded)
    pad_start = pad_end - padded
    expert_id = route[:, :TOP_K].astype(jnp.int32)
    rank = route[:, 4:4 + TOP_K].astype(jnp.int32)
    start_of = jnp.sum(jnp.where(expert_id[..., None] == jnp.arange(N_EXPERTS, dtype=jnp.int32),
                                 pad_start.astype(jnp.int32), 0), axis=-1)
    dest = (start_of + rank).reshape(n_tokens * TOP_K)
    nb = n_tokens * TOP_K // MOE_BM + N_EXPERTS
    block_expert = jnp.minimum(
        jnp.searchsorted(pad_end, jnp.arange(nb, dtype=jnp.int32) * MOE_BM, side='right'),
        N_EXPERTS - 1).astype(jnp.int32)
    n_active = (pad_end[-1] // MOE_BM).astype(jnp.int32)
    first = jnp.concatenate([jnp.ones((1,), jnp.int32),
                             (block_expert[1:] != block_expert[:-1]).astype(jnp.int32)])
    meta = jnp.concatenate([n_active.reshape(1), jnp.zeros((1,), jnp.int32)])
    return dest, counts, pad_start.astype(jnp.int32), block_expert, first, meta


def _token_copy(src_hbm, src_row, dst_ref, dst_row, sem):
    return pltpu.make_async_copy(src_hbm.at[pl.ds(src_row * TOKEN_ROWS, TOKEN_ROWS), :],
                                 dst_ref.at[pl.ds(dst_row * TOKEN_ROWS, TOKEN_ROWS), :], sem)


def _dispatch_kernel(cnt, pstart, meta, dest_ref, x_ref, xs_hbm, zero_scr, sem):
    i = pl.program_id(0)
    tt = DISPATCH_TILE
    blk = MOE_BM * TOKEN_ROWS
    n_blocks = xs_hbm.shape[0] // blk

    def spare_blocks(fn):
        def one(b, c):
            fn(pltpu.make_async_copy(zero_scr, xs_hbm.at[pl.ds(b * blk, blk), :], sem.at[2]))
            return c

        lax.fori_loop(meta[0], n_blocks, one, 0)

    @pl.when(i == 0)
    def _():
        zero_scr[...] = jnp.zeros(zero_scr.shape, _F32)
        spare_blocks(lambda cp: cp.start())

    def start(j, c):
        for kk in range(TOP_K):
            _token_copy(x_ref, j, xs_hbm, dest_ref[0, 0, TOP_K * j + kk], sem.at[0]).start(priority=kk)
        return c

    lax.fori_loop(0, tt, start, 0, unroll=8)

    def pad_rows(fn):
        def per_expert(e, c):
            n = cnt[e]
            n_pad = (n + MOE_BM - 1) // MOE_BM * MOE_BM

            def one(r, c2):
                fn(_token_copy(x_ref, 0, xs_hbm, pstart[e] + r, sem.at[1]))
                return c2

            lax.fori_loop(n, n_pad, one, 0)
            return c

        lax.fori_loop(0, N_EXPERTS, per_expert, 0)

    @pl.when(i == 0)
    def _():
        pad_rows(lambda cp: cp.start())
        pad_rows(lambda cp: cp.wait())
        spare_blocks(lambda cp: cp.wait())

    def wait(j, c):
        for kk in range(TOP_K):
            _token_copy(x_ref, 0, xs_hbm, 0, sem.at[0]).wait()
        return c

    lax.fori_loop(0, tt, wait, 0, unroll=8)


def _dispatch(counts, pad_start, meta, dest, x_tok, n_rows):
    n_tokens = x_tok.shape[0] // TOKEN_ROWS
    tt = DISPATCH_TILE
    return pl.pallas_call(
        _dispatch_kernel,
        name="moe_dispatch",
        grid_spec=pltpu.PrefetchScalarGridSpec(
            num_scalar_prefetch=3,
            grid=(n_tokens // tt,),
            in_specs=[pl.BlockSpec((1, 1, TOP_K * tt), lambda i, c, s, m: (i, 0, 0),
                                   memory_space=pltpu.SMEM),
                      pl.BlockSpec((tt * TOKEN_ROWS, LANES), lambda i, c, s, m: (i, 0))],
            out_specs=pl.BlockSpec(memory_space=pl.ANY),
            scratch_shapes=[pltpu.VMEM((MOE_BM * TOKEN_ROWS, LANES), _F32),
                            pltpu.SemaphoreType.DMA((3,))]),
        out_shape=jax.ShapeDtypeStruct((n_rows * TOKEN_ROWS, LANES), _F32),
        compiler_params=_params(),
    )(counts, pad_start, meta, dest.reshape(n_tokens // tt, 1, TOP_K * tt), x_tok)


def _from_token_tiles(ref, n):
    return jnp.concatenate([ref[pl.ds(c, n, stride=TOKEN_ROWS), :] for c in range(TOKEN_ROWS)], axis=1)


def _expert_kernel(lid, bexp, first, meta, xs_ref, wg_ref, wu_ref, wd_ref, o_ref, wg_b, wu_b, wd_b):
    b = pl.program_id(0)

    @pl.when(b >= meta[0])
    def _():
        o_ref[...] = jnp.zeros(o_ref.shape, o_ref.dtype)

    @pl.when(b < meta[0])
    def _():
        @pl.when(first[b] == 1)
        def _():
            wg_b[...] = wg_ref[...].astype(_MXU)
            wu_b[...] = wu_ref[...].astype(_MXU)
            wd_b[...] = wd_ref[...].astype(_MXU)

        xb = _from_token_tiles(xs_ref, MOE_BM).astype(_MXU)
        hg = _dot(xb, wg_b[...])
        hu = _dot(xb, wu_b[...])
        hid = (hg * _sigmoid(hg)) * hu
        y = _dot(hid.astype(_MXU), wd_b[...])
        for c in range(TOKEN_ROWS):
            o_ref[pl.ds(c, MOE_BM, stride=TOKEN_ROWS), :] = y[:, c * LANES:(c + 1) * LANES]


def _experts(lid, block_expert, first, meta, xs_tok, w_gate, w_up, w_down):
    rows = MOE_BM * TOKEN_ROWS
    nb = xs_tok.shape[0] // rows
    live = lambda b, m: jnp.minimum(b, m[0] - 1)
    return pl.pallas_call(
        _expert_kernel,
        name="moe_experts",
        grid_spec=pltpu.PrefetchScalarGridSpec(
            num_scalar_prefetch=4,
            grid=(nb,),
            in_specs=[pl.BlockSpec((rows, LANES), lambda b, l, e, f, m: (live(b, m), 0)),
                      pl.BlockSpec((None, None, D_MODEL, EXPERT_HIDDEN),
                                   lambda b, l, e, f, m: (l[0], e[b], 0, 0)),
                      pl.BlockSpec((None, None, D_MODEL, EXPERT_HIDDEN),
                                   lambda b, l, e, f, m: (l[0], e[b], 0, 0)),
                      pl.BlockSpec((None, None, EXPERT_HIDDEN, D_MODEL),
                                   lambda b, l, e, f, m: (l[0], e[b], 0, 0))],
            out_specs=pl.BlockSpec((rows, LANES), lambda b, l, e, f, m: (b, 0)),
            scratch_shapes=[pltpu.VMEM((D_MODEL, EXPERT_HIDDEN), _MXU),
                            pltpu.VMEM((D_MODEL, EXPERT_HIDDEN), _MXU),
                            pltpu.VMEM((EXPERT_HIDDEN, D_MODEL), _MXU)]),
        out_shape=jax.ShapeDtypeStruct(xs_tok.shape, _F32),
        compiler_params=_params(),
    )(lid, block_expert, first, meta, xs_tok, w_gate, w_up, w_down)


def _combine_kernel(lid, pos_ref, pos_next_ref, y_hbm, x_ref, route_ref, g2_ref, b2_ref, wpg_ref, bpg_ref,
                    p_ref, wpu_ref, g3_ref, b3_ref, o_ref, ybuf, sem):
    tc = x_ref.shape[0]
    i = pl.program_id(0)
    slot = lax.rem(i, 2)

    def gather(rows_ref, s, fn):
        def body(r, c):
            for kk in range(TOP_K):
                fn(_token_copy(y_hbm, rows_ref[0, 0, TOP_K * r + kk], ybuf.at[s, kk], r, sem.at[s]), kk)
            return c

        lax.fori_loop(0, tc, body, 0, unroll=8)

    start = lambda cp, kk: cp.start(priority=kk)

    @pl.when(i == 0)
    def _():
        gather(pos_ref, 0, start)

    @pl.when(i + 1 < pl.num_programs(0))
    def _():
        gather(pos_next_ref, 1 - slot, start)

    gather(pos_ref, slot, lambda cp, kk: cp.wait())
    route = route_ref[...]
    y = (route[:, 2:3] * _from_token_tiles(ybuf.at[slot, 0], tc)
         + route[:, 3:4] * _from_token_tiles(ybuf.at[slot, 1], tc))
    x2 = _layer_norm(DN_ALPHA * x_ref[...] + y, g2_ref[...], b2_ref[...])
    gate = _sigmoid(_dot(x2.astype(_MXU), wpg_ref[...]) + bpg_ref[...])
    up = _dot(p_ref[...].astype(_MXU), wpu_ref[...])
    o_ref[...] = _layer_norm(DN_ALPHA * x2 + gate * up, g3_ref[...], b3_ref[...])


def _combine(lid, dest, y_tok, x, route, ln2_g, ln2_b, w_pg, b_pg, p, w_pu, ln3_g, ln3_b):
    S = x.shape[0]
    tc = COMBINE_TILE
    n_tiles = S // tc
    pos = dest.reshape(n_tiles, 1, TOP_K * tc)
    vec = lambda n: pl.BlockSpec((None, 1, n), lambda i, l: (l[0], 0, 0))
    return pl.pallas_call(
        _combine_kernel,
        name="combine_ple",
        grid_spec=pltpu.PrefetchScalarGridSpec(
            num_scalar_prefetch=1,
            grid=(S // tc,),
            in_specs=[pl.BlockSpec((1, 1, TOP_K * tc), lambda i, l: (i, 0, 0), memory_space=pltpu.SMEM),
                      pl.BlockSpec((1, 1, TOP_K * tc), lambda i, l: (jnp.minimum(i + 1, n_tiles - 1), 0, 0),
                                   memory_space=pltpu.SMEM),
                      pl.BlockSpec(memory_space=pl.ANY),
                      pl.BlockSpec((tc, D_MODEL), lambda i, l: (i, 0)),
                      pl.BlockSpec((tc, LANES), lambda i, l: (i, 0)),
                      vec(D_MODEL), vec(D_MODEL),
                      pl.BlockSpec((None, D_MODEL, D_MODEL), lambda i, l: (l[0], 0, 0)),
                      vec(D_MODEL),
                      pl.BlockSpec((None, tc, PLE_DIM), lambda i, l: (l[0], i, 0)),
                      pl.BlockSpec((None, PLE_DIM, D_MODEL), lambda i, l: (l[0], 0, 0)),
                      vec(D_MODEL), vec(D_MODEL)],
            out_specs=pl.BlockSpec((tc, D_MODEL), lambda i, l: (i, 0)),
            scratch_shapes=[pltpu.VMEM((2, TOP_K, tc * TOKEN_ROWS, LANES), _F32),
                            pltpu.SemaphoreType.DMA((2,))]),
        out_shape=jax.ShapeDtypeStruct((S, D_MODEL), _F32),
        compiler_params=_params(),
    )(lid, pos, pos, y_tok, x, route, ln2_g, ln2_b, w_pg, b_pg, p, w_pu, ln3_g, ln3_b)


def _split_in_proj(w_in):
    sizes = (SC_WIDTH, SC_WIDTH, SC_WIDTH, GLA_QK, GLA_QK, GLA_VW, GLA_GATE_RANK, GLA_VW,
             MLA_Q_RANK, MLA_KV_RANK, MLA_ROPE, D_MODEL, D_MODEL, D_MODEL)
    offs = [0]
    for s in sizes:
        offs.append(offs[-1] + s)
    col = lambda j: w_in[:, :, offs[j]:offs[j + 1]]
    L = w_in.shape[0]
    zeros = lambda n: jnp.zeros((L, D_MODEL, n), w_in.dtype)
    w_conv_in = jnp.concatenate([col(0), col(1), col(2)], axis=-1)
    w_gla_in = jnp.concatenate([col(3), col(4), col(5), col(7), col(6), zeros(LANES - GLA_GATE_RANK)], axis=-1)
    w_mla_in = jnp.concatenate([col(8), col(9), col(10), zeros(LANES - MLA_ROPE)], axis=-1)
    w_gates = jnp.concatenate([col(11), col(12), col(13)], axis=-1)
    return [w.astype(_MXU) for w in (w_conv_in, w_gla_in, w_mla_in, w_gates)]


def kernel(x, p, positions, ln0_g, ln0_b, w_in, w_conv, w_gla_gate, b_gla_gate, gla_norm_g, mla_q_norm_g, mla_kv_norm_g, w_uq, w_ukv, w_br, w_o, ln1_g, ln1_b, w_grp, b_grp, w_exp, b_exp, w_gate, w_up, w_down, ln2_g, ln2_b, w_ple_gate, b_ple_gate, w_ple_up, ln3_g, ln3_b):
    B, S, D = x.shape
    L = w_in.shape[0]
    assert B == 1 and D == D_MODEL and S % ROW_TILE == 0 and S % ATTN_TQ == 0
    H = MLA_HEADS

    w_conv_in, w_gla_in, w_mla_in, w_gates = _split_in_proj(w_in)
    w_gg = jnp.pad(w_gla_gate, ((0, 0), (0, LANES - GLA_GATE_RANK), (0, 0))).astype(_MXU)
    w_uq_p = jnp.pad(w_uq.reshape(L, MLA_Q_RANK, H, MLA_QK),
                     ((0, 0), (0, 0), (0, 0), (0, MLA_QK_PAD - MLA_QK))
                     ).reshape(L, MLA_Q_RANK, H * MLA_QK_PAD).astype(_MXU)
    w_ukv4 = w_ukv.reshape(L, MLA_KV_RANK, H, MLA_NOPE + MLA_V)
    w_uk = w_ukv4[..., :MLA_NOPE].reshape(L, MLA_KV_RANK, H * MLA_NOPE).astype(_MXU)
    w_uv = w_ukv4[..., MLA_NOPE:].reshape(L, MLA_KV_RANK, H * MLA_V).astype(_MXU)
    w_route = jnp.concatenate(
        [w_grp, w_exp.reshape(L, D, N_EXPERTS),
         jnp.zeros((L, D, LANES - N_GROUPS - N_EXPERTS), _F32)], axis=-1).astype(_MXU)
    b_route = jnp.concatenate(
        [b_grp, b_exp.reshape(L, N_EXPERTS),
         jnp.zeros((L, LANES - N_GROUPS - N_EXPERTS), _F32)], axis=-1).reshape(L, 1, LANES)
    w_br_c = w_br.astype(_MXU)
    w_o_c = w_o.astype(_MXU)
    w_pg_c = w_ple_gate.astype(_MXU)
    w_pu_c = w_ple_up.astype(_MXU)
    row = lambda a: a.reshape(L, 1, a.shape[-1])

    cos_t, sin_t = _rope_tables(positions.reshape(S))
    x0 = _entry_norm(x.reshape(S, D), ln0_g, ln0_b)
    p3 = p.reshape(L, S, PLE_DIM)

    def layer(xc, i):
        lid = jnp.reshape(i, (1,)).astype(jnp.int32)
        y_a = _conv_branch(lid, xc, w_conv_in, w_conv)
        y_b = _gla_branch(lid, xc, w_gla_in, w_gg, row(b_gla_gate), row(gla_norm_g))
        q, k, v = _mla_prep(lid, xc, w_mla_in, row(mla_q_norm_g), row(mla_kv_norm_g),
                            w_uq_p, w_uk, w_uv, cos_t, sin_t)
        y_c = _attention(q, k, v)
        x1, x1_tok, route, counts_row = _merge(lid, xc, y_a, y_b, y_c, w_gates, w_br_c, w_o_c,
                                               row(ln1_g), row(ln1_b), w_route, b_route)
        dest, counts, pad_start, block_expert, first, meta = _dispatch_plan(route, counts_row, S)
        n_rows = (S * TOP_K // MOE_BM + N_EXPERTS) * MOE_BM
        xs_tok = _dispatch(counts, pad_start, meta, dest, x1_tok, n_rows)
        y_tok = _experts(lid, block_expert, first, meta, xs_tok, w_gate, w_up, w_down)
        x3 = _combine(lid, dest, y_tok, x1, route, row(ln2_g), row(ln2_b), w_pg_c, row(b_ple_gate),
                      p3, w_pu_c, row(ln3_g), row(ln3_b))
        return x3, None

    out, _ = lax.scan(layer, x0, jnp.arange(L, dtype=jnp.int32))
    return out.reshape(B, S, D)
```

```python
import functools

import jax
import jax.numpy as jnp
from jax import lax
from jax.experimental import pallas as pl
from jax.experimental.pallas import tpu as pltpu

D_MODEL = 1024
DEPTH = 4
PLE_DIM = 256
SC_WIDTH = 512
SC_CONV = 3
GLA_HEADS = 4
GLA_DK = 64
GLA_DV = 128
GLA_GATE_RANK = 16
GLA_TAU = 16.0
GLA_CHUNK = 64
MLA_HEADS = 4
MLA_NOPE = 128
MLA_ROPE = 64
MLA_V = 128
MLA_Q_RANK = 256
MLA_KV_RANK = 128
ROPE_THETA = 10000.0
N_GROUPS = 8
EXPERTS_PER_GROUP = 8
N_EXPERTS = N_GROUPS * EXPERTS_PER_GROUP
TOP_K = 2
EXPERT_HIDDEN = 256
DN_ALPHA = (2 * DEPTH) ** 0.25
LN_EPS = 1e-5
RMS_EPS = 1e-6

GLA_QK = GLA_HEADS * GLA_DK
GLA_VW = GLA_HEADS * GLA_DV
MLA_QK = MLA_NOPE + MLA_ROPE
MLA_VW = MLA_HEADS * MLA_V

LANES = 128
MLA_QK_PAD = 2 * LANES
VMEM_LIMIT = 56 * 1024 * 1024

ROW_TILE = 512
ATTN_TQ = 1024
ATTN_TK = 512
ATTN_ROWS = 256
LOG2_E = 1.4426950408889634
MOE_BM = 256
COMBINE_TILE = 256
DISPATCH_TILE = 1024
TOKEN_ROWS = D_MODEL // LANES

_MXU = jnp.bfloat16
_F32 = jnp.float32


def _dot(a, b):
    return jnp.dot(a, b, preferred_element_type=_F32)


def _dot_nt(a, b):
    return lax.dot_general(a, b, (((1,), (1,)), ((), ())), preferred_element_type=_F32)


def _dot_tn(a, b):
    return lax.dot_general(a, b, (((0,), (0,)), ((), ())), preferred_element_type=_F32)


def _layer_norm(x, g, b):
    mu = jnp.mean(x, axis=-1, keepdims=True)
    xc = x - mu
    var = jnp.mean(xc * xc, axis=-1, keepdims=True)
    return xc * lax.rsqrt(var + LN_EPS) * g + b


def _rms_norm(x, g):
    return x * lax.rsqrt(jnp.mean(x * x, axis=-1, keepdims=True) + RMS_EPS) * g


def _sigmoid(x):
    return 1.0 / (1.0 + jnp.exp(-x))


def _params(n_axes=1):
    return pltpu.CompilerParams(dimension_semantics=("arbitrary",) * n_axes,
                                vmem_limit_bytes=VMEM_LIMIT)


def _ln0_kernel(x_ref, g_ref, b_ref, o_ref):
    o_ref[...] = _layer_norm(x_ref[...], g_ref[...], b_ref[...])


def _entry_norm(x, g, b):
    S = x.shape[0]
    return pl.pallas_call(
        _ln0_kernel,
        name="entry_ln",
        grid=(S // ROW_TILE,),
        in_specs=[pl.BlockSpec((ROW_TILE, D_MODEL), lambda i: (i, 0)),
                  pl.BlockSpec((1, D_MODEL), lambda i: (0, 0)),
                  pl.BlockSpec((1, D_MODEL), lambda i: (0, 0))],
        out_specs=pl.BlockSpec((ROW_TILE, D_MODEL), lambda i: (i, 0)),
        out_shape=jax.ShapeDtypeStruct((S, D_MODEL), _F32),
        compiler_params=_params(),
    )(x, g.reshape(1, D_MODEL), b.reshape(1, D_MODEL))


def _rope_kernel(pos_ref, inv_ref, c_ref, s_ref):
    ang = pos_ref[...].astype(_F32) * inv_ref[...]
    lane = lax.broadcasted_iota(jnp.int32, ang.shape, 1)
    half = MLA_ROPE // 2
    cosv = jnp.cos(ang)
    sinv = jnp.sin(ang)
    c_ref[...] = jnp.where(lane < MLA_ROPE, cosv, 0.0)
    s_ref[...] = jnp.where(lane < half, -sinv, jnp.where(lane < MLA_ROPE, sinv, 0.0))


def _rope_tables(positions):
    S = positions.shape[0]
    half = MLA_ROPE // 2
    inv = 1.0 / (ROPE_THETA ** (jnp.arange(0, MLA_ROPE, 2, dtype=_F32) / MLA_ROPE))
    inv = jnp.tile(inv, LANES // half).reshape(1, LANES)
    return pl.pallas_call(
        _rope_kernel,
        name="rope_tables",
        grid=(S // ROW_TILE,),
        in_specs=[pl.BlockSpec((ROW_TILE, 1), lambda i: (i, 0)),
                  pl.BlockSpec((1, LANES), lambda i: (0, 0))],
        out_specs=[pl.BlockSpec((ROW_TILE, LANES), lambda i: (i, 0)),
                   pl.BlockSpec((ROW_TILE, LANES), lambda i: (i, 0))],
        out_shape=[jax.ShapeDtypeStruct((S, LANES), _F32)] * 2,
        compiler_params=_params(),
    )(positions.reshape(S, 1), inv)


def _conv_kernel(lid, x_ref, w_ref, wc_ref, o_ref, u_scr):
    tm = x_ref.shape[0]

    @pl.when(pl.program_id(0) == 0)
    def _():
        u_scr[0:8, :] = jnp.zeros((8, SC_WIDTH), _F32)

    h = _dot(x_ref[...].astype(_MXU), w_ref[...])
    a_b = h[:, :SC_WIDTH]
    u = h[:, SC_WIDTH:2 * SC_WIDTH] * h[:, 2 * SC_WIDTH:]
    u_scr[8:8 + tm, :] = u
    wc = wc_ref[...]
    y = wc[2:3, :] * u + wc[1:2, :] * u_scr[7:7 + tm, :] + wc[0:1, :] * u_scr[6:6 + tm, :]
    o_ref[...] = (a_b * y).astype(o_ref.dtype)
    u_scr[0:8, :] = u_scr[tm:tm + 8, :]


def _conv_branch(lid, x, w_conv_in, w_conv):
    S = x.shape[0]
    tm = ROW_TILE
    return pl.pallas_call(
        _conv_kernel,
        name="conv_branch",
        grid_spec=pltpu.PrefetchScalarGridSpec(
            num_scalar_prefetch=1,
            grid=(S // tm,),
            in_specs=[pl.BlockSpec((tm, D_MODEL), lambda i, l: (i, 0)),
                      pl.BlockSpec((None, D_MODEL, 3 * SC_WIDTH), lambda i, l: (l[0], 0, 0)),
                      pl.BlockSpec((None, SC_CONV, SC_WIDTH), lambda i, l: (l[0], 0, 0))],
            out_specs=pl.BlockSpec((tm, SC_WIDTH), lambda i, l: (i, 0)),
            scratch_shapes=[pltpu.VMEM((tm + 8, SC_WIDTH), _F32)]),
        out_shape=jax.ShapeDtypeStruct((S, SC_WIDTH), _MXU),
        compiler_params=_params(),
    )(lid, x, w_conv_in, w_conv)


_GLA_IN = 2 * GLA_QK + 2 * GLA_VW + LANES


def _gla_kernel(lid, x_ref, w_ref, wg_ref, bg_ref, ng_ref, o_ref, st_scr):
    tm = x_ref.shape[0]
    C = GLA_CHUNK

    @pl.when(pl.program_id(0) == 0)
    def _():
        st_scr[...] = jnp.zeros(st_scr.shape, _F32)

    h = _dot(x_ref[...].astype(_MXU), w_ref[...])
    q = h[:, :GLA_QK]
    k = h[:, GLA_QK:2 * GLA_QK]
    v = h[:, 2 * GLA_QK:2 * GLA_QK + GLA_VW]
    r = h[:, 2 * GLA_QK + GLA_VW:2 * GLA_QK + 2 * GLA_VW]
    g_lr = h[:, 2 * GLA_QK + 2 * GLA_VW:]
    z = _dot(g_lr.astype(_MXU), wg_ref[...]) + bg_ref[...]
    log_a = (jnp.minimum(z, 0.0) - jnp.log1p(jnp.exp(-jnp.abs(z)))) * (1.0 / GLA_TAU)

    row_in_chunk = lax.broadcasted_iota(jnp.int32, log_a.shape, 0) & (C - 1)
    b = log_a
    d = 1
    while d < C:
        b = b + jnp.where(row_in_chunk >= d, pltpu.roll(b, d, 0), 0.0)
        d *= 2

    ri = lax.broadcasted_iota(jnp.int32, (C, C), 0)
    ci = lax.broadcasted_iota(jnp.int32, (C, C), 1)
    causal = ci <= ri
    scale = GLA_DK ** -0.5
    ng = ng_ref[...]
    for c in range(tm // C):
        rows = slice(c * C, (c + 1) * C)
        bc = b[rows]
        b_last = bc[C - 1:C, :]
        kc = k[rows]
        q_e = (q[rows] * scale) * jnp.exp(bc)
        k_e = kc * jnp.exp(-bc)
        k_t = kc * jnp.exp(b_last - bc)
        decay = jnp.exp(b_last)
        for hh in range(GLA_HEADS):
            ks = slice(hh * GLA_DK, (hh + 1) * GLA_DK)
            vs = slice(hh * GLA_DV, (hh + 1) * GLA_DV)
            qh = q_e[:, ks].astype(_MXU)
            vh = v[rows, vs].astype(_MXU)
            a = jnp.where(causal, _dot_nt(qh, k_e[:, ks].astype(_MXU)), 0.0)
            st = st_scr[hh]
            o = _dot(a.astype(_MXU), vh) + _dot_nt(qh, st.astype(_MXU))
            st_scr[hh] = st * decay[:, ks] + _dot_tn(vh, k_t[:, ks].astype(_MXU))
            rr = r[rows, vs]
            o_ref[rows, vs] = (_rms_norm(o, ng) * (rr * _sigmoid(rr))).astype(o_ref.dtype)


def _gla_branch(lid, x, w_gla_in, w_gate, b_gate, norm_g):
    S = x.shape[0]
    tm = ROW_TILE
    return pl.pallas_call(
        _gla_kernel,
        name="gla_branch",
        grid_spec=pltpu.PrefetchScalarGridSpec(
            num_scalar_prefetch=1,
            grid=(S // tm,),
            in_specs=[pl.BlockSpec((tm, D_MODEL), lambda i, l: (i, 0)),
                      pl.BlockSpec((None, D_MODEL, _GLA_IN), lambda i, l: (l[0], 0, 0)),
                      pl.BlockSpec((None, LANES, GLA_QK), lambda i, l: (l[0], 0, 0)),
                      pl.BlockSpec((None, 1, GLA_QK), lambda i, l: (l[0], 0, 0)),
                      pl.BlockSpec((None, 1, GLA_DV), lambda i, l: (l[0], 0, 0))],
            out_specs=pl.BlockSpec((tm, GLA_VW), lambda i, l: (i, 0)),
            scratch_shapes=[pltpu.VMEM((GLA_HEADS, GLA_DV, GLA_DK), _F32)]),
        out_shape=jax.ShapeDtypeStruct((S, GLA_VW), _MXU),
        compiler_params=_params(),
    )(lid, x, w_gla_in, w_gate, b_gate, norm_g)


_MLA_IN = MLA_Q_RANK + MLA_KV_RANK + LANES


def _mla_prep_kernel(lid, x_ref, w_ref, qg_ref, kvg_ref, wuq_ref, wk_ref, wv_ref, ct_ref, st_ref,
                     q_o, k_o, v_o):
    h = _dot(x_ref[...].astype(_MXU), w_ref[...])
    c_q = _rms_norm(h[:, :MLA_Q_RANK], qg_ref[...]).astype(_MXU)
    c_kv = _rms_norm(h[:, MLA_Q_RANK:MLA_Q_RANK + MLA_KV_RANK], kvg_ref[...]).astype(_MXU)
    k_rope_raw = h[:, MLA_Q_RANK + MLA_KV_RANK:]
    q = _dot(c_q, wuq_ref[...])
    k_nope = _dot(c_kv, wk_ref[...])
    v = _dot(c_kv, wv_ref[...])
    cos_t = ct_ref[...]
    sin_t = st_ref[...]
    lane = lax.broadcasted_iota(jnp.int32, cos_t.shape, 1)
    half = MLA_ROPE // 2

    def rope(xr):
        rot = jnp.where(lane < half, pltpu.roll(xr, LANES - half, 1), pltpu.roll(xr, half, 1))
        return xr * cos_t + rot * sin_t

    k_rope = rope(k_rope_raw)
    one_col = jnp.where(lane == 0, 1.0, 0.0)
    scale = MLA_QK ** -0.5 * LOG2_E
    for hh in range(MLA_HEADS):
        base = hh * MLA_QK_PAD
        q_h = jnp.concatenate([q[:, base:base + MLA_NOPE], rope(q[:, base + MLA_NOPE:base + MLA_QK_PAD])],
                              axis=1)
        q_o[hh] = (q_h * scale).astype(q_o.dtype)
        k_o[hh] = jnp.concatenate([k_nope[:, hh * MLA_NOPE:(hh + 1) * MLA_NOPE], k_rope],
                                  axis=1).astype(k_o.dtype)
        v_o[hh] = jnp.concatenate([v[:, hh * MLA_V:(hh + 1) * MLA_V], one_col], axis=1).astype(v_o.dtype)


def _mla_prep(lid, x, w_mla_in, q_norm_g, kv_norm_g, w_uq, w_uk, w_uv, cos_t, sin_t):
    S = x.shape[0]
    tm = ROW_TILE
    H = MLA_HEADS
    return pl.pallas_call(
        _mla_prep_kernel,
        name="mla_prep",
        grid_spec=pltpu.PrefetchScalarGridSpec(
            num_scalar_prefetch=1,
            grid=(S // tm,),
            in_specs=[pl.BlockSpec((tm, D_MODEL), lambda i, l: (i, 0)),
                      pl.BlockSpec((None, D_MODEL, _MLA_IN), lambda i, l: (l[0], 0, 0)),
                      pl.BlockSpec((None, 1, MLA_Q_RANK), lambda i, l: (l[0], 0, 0)),
                      pl.BlockSpec((None, 1, MLA_KV_RANK), lambda i, l: (l[0], 0, 0)),
                      pl.BlockSpec((None, MLA_Q_RANK, H * MLA_QK_PAD), lambda i, l: (l[0], 0, 0)),
                      pl.BlockSpec((None, MLA_KV_RANK, H * MLA_NOPE), lambda i, l: (l[0], 0, 0)),
                      pl.BlockSpec((None, MLA_KV_RANK, H * MLA_V), lambda i, l: (l[0], 0, 0)),
                      pl.BlockSpec((tm, LANES), lambda i, l: (i, 0)),
                      pl.BlockSpec((tm, LANES), lambda i, l: (i, 0))],
            out_specs=[pl.BlockSpec((H, tm, MLA_QK_PAD), lambda i, l: (0, i, 0)),
                       pl.BlockSpec((H, tm, MLA_QK_PAD), lambda i, l: (0, i, 0)),
                       pl.BlockSpec((H, tm, MLA_V + LANES), lambda i, l: (0, i, 0))]),
        out_shape=[jax.ShapeDtypeStruct((H, S, MLA_QK_PAD), _MXU),
                   jax.ShapeDtypeStruct((H, S, MLA_QK_PAD), _MXU),
                   jax.ShapeDtypeStruct((H, S, MLA_V + LANES), _MXU)],
        compiler_params=_params(),
    )(lid, x, w_mla_in, q_norm_g, kv_norm_g, w_uq, w_uk, w_uv, cos_t, sin_t)


def _attn_kernel(q_ref, k_ref, v_ref, o_ref, s_a, s_b, m_scr, acc_scr):
    tq = q_ref.shape[0]
    tk = ATTN_TK
    rb = ATTN_ROWS
    n_rb = tq // rb
    qi = pl.program_id(1)
    m_scr[...] = jnp.full(m_scr.shape, -jnp.inf, _F32)
    acc_scr[...] = jnp.zeros(acc_scr.shape, _F32)

    def scores(dst, start, first_rb=0):
        k = k_ref[pl.ds(start, tk), :]
        for r in range(first_rb, n_rb):
            rows = slice(r * rb, (r + 1) * rb)
            dst[rows, :] = _dot_nt(q_ref[rows, :], k)

    def consume(src, start, col0=None):
        for r in range(n_rb):
            row0 = r * rb
            rows = slice(row0, row0 + rb)
            n_cols = tk
            if col0 is not None:
                n_cols = min(tk, row0 + rb - col0)
                if n_cols <= 0:
                    continue
            s = src[rows, :n_cols]
            if col0 is not None and col0 + n_cols - 1 > row0:
                ri = lax.broadcasted_iota(jnp.int32, s.shape, 0) + row0
                ci = lax.broadcasted_iota(jnp.int32, s.shape, 1) + col0
                s = jnp.where(ci <= ri, s, -jnp.inf)
            v = v_ref[pl.ds(start, n_cols), :]
            m_prev = m_scr[rows, :]
            m_new = jnp.maximum(m_prev, jnp.max(s, axis=1, keepdims=True))
            alpha = jnp.exp2(m_prev - m_new)
            p = jnp.exp2(s - m_new[:, :1]).astype(v.dtype)
            acc_scr[rows, :] = jnp.concatenate([alpha, alpha], axis=1) * acc_scr[rows, :] + _dot(p, v)
            m_scr[rows, :] = m_new

    col = lambda j: pl.multiple_of(j * tk, tk)
    scores(s_a, col(0))

    def body(i, carry):
        scores(s_b, col(2 * i + 1))
        consume(s_a, col(2 * i))
        scores(s_a, col(2 * i + 2))
        consume(s_b, col(2 * i + 1))
        return carry

    lax.fori_loop(0, qi, body, 0)
    scores(s_b, col(2 * qi + 1), first_rb=tk // rb)
    consume(s_a, col(2 * qi), col0=0)
    consume(s_b, col(2 * qi + 1), col0=tk)
    o_ref[...] = (acc_scr[:, :MLA_V] / acc_scr[:, MLA_V:MLA_V + 1]).astype(o_ref.dtype)


def _attention(q, k, v):
    H, S, _ = q.shape
    tq = ATTN_TQ
    return pl.pallas_call(
        _attn_kernel,
        name="mla_attention",
        grid=(H, S // tq),
        in_specs=[pl.BlockSpec((None, tq, MLA_QK_PAD), lambda h, i: (h, i, 0)),
                  pl.BlockSpec((None, S, MLA_QK_PAD), lambda h, i: (h, 0, 0)),
                  pl.BlockSpec((None, S, MLA_V + LANES), lambda h, i: (h, 0, 0))],
        out_specs=pl.BlockSpec((tq, MLA_V), lambda h, i: (i, h)),
        out_shape=jax.ShapeDtypeStruct((S, MLA_VW), _MXU),
        scratch_shapes=[pltpu.VMEM((tq, ATTN_TK), _F32),
                        pltpu.VMEM((tq, ATTN_TK), _F32),
                        pltpu.VMEM((tq, LANES), _F32),
                        pltpu.VMEM((tq, MLA_V + LANES), _F32)],
        compiler_params=_params(2),
    )(q, k, v)


def _merge_kernel(lid, x_ref, ya_ref, yb_ref, yc_ref, wgt_ref, wbr_ref, wo_ref, g_ref, b_ref,
                  wr_ref, br_ref, tril_ref, x_o, xt_o, route_o, cnt_o, cnt_scr):
    tm = x_ref.shape[0]

    @pl.when(pl.program_id(0) == 0)
    def _():
        cnt_scr[...] = jnp.zeros(cnt_scr.shape, _F32)

    x = x_ref[...]
    gates = _dot(x.astype(_MXU), wgt_ref[...])
    wbr = wbr_ref
    merged = (_sigmoid(gates[:, :D_MODEL]) * _dot(ya_ref[...], wbr[0:SC_WIDTH, :])
              + _sigmoid(gates[:, D_MODEL:2 * D_MODEL])
              * _dot(yb_ref[...], wbr[SC_WIDTH:SC_WIDTH + GLA_VW, :])
              + _sigmoid(gates[:, 2 * D_MODEL:]) * _dot(yc_ref[...], wbr[SC_WIDTH + GLA_VW:, :]))
    x1 = _layer_norm(DN_ALPHA * x + _dot(merged.astype(_MXU), wo_ref[...]), g_ref[...], b_ref[...])
    x_o[...] = x1
    for c in range(TOKEN_ROWS):
        xt_o[pl.ds(c, tm, stride=TOKEN_ROWS), :] = x1[:, c * LANES:(c + 1) * LANES]

    logits = _dot(x1.astype(_MXU), wr_ref[...]) + br_ref[...]
    lane = lax.broadcasted_iota(jnp.int32, logits.shape, 1)
    neg = -jnp.inf
    gl = jnp.where(lane < N_GROUPS, logits, neg)
    g_max = jnp.max(gl, axis=1, keepdims=True)
    g_top = jnp.min(jnp.where(gl == g_max, lane, LANES), axis=1, keepdims=True)
    p_g = 1.0 / jnp.sum(jnp.where(lane < N_GROUPS, jnp.exp(logits - g_max), 0.0), axis=1, keepdims=True)
    lo = N_GROUPS + g_top * EXPERTS_PER_GROUP
    sl = jnp.where((lane >= lo) & (lane < lo + EXPERTS_PER_GROUP), logits, neg)
    v0 = jnp.max(sl, axis=1, keepdims=True)
    i0 = jnp.min(jnp.where(sl == v0, lane, LANES), axis=1, keepdims=True)
    sl = jnp.where(lane == i0, neg, sl)
    v1 = jnp.max(sl, axis=1, keepdims=True)
    i1 = jnp.min(jnp.where(sl == v1, lane, LANES), axis=1, keepdims=True)
    e1 = jnp.exp(v1 - v0)
    w0 = p_g / (1.0 + e1)
    w1 = p_g * e1 / (1.0 + e1)
    oh0 = lane == i0
    oh1 = lane == i1
    ohs = jnp.where(oh0, 1.0, jnp.where(oh1, 1.0, 0.0))
    before = _dot(tril_ref[...], ohs.astype(_MXU)) + cnt_scr[0:1, :]
    rank0 = jnp.sum(jnp.where(oh0, before, 0.0), axis=1, keepdims=True)
    rank1 = jnp.sum(jnp.where(oh1, before, 0.0), axis=1, keepdims=True)
    cnt_scr[...] = cnt_scr[...] + jnp.sum(ohs, axis=0, keepdims=True)
    cnt_o[...] = cnt_scr[...]
    route_o[...] = jnp.where(
        lane == 0, (i0 - N_GROUPS).astype(_F32),
        jnp.where(lane == 1, (i1 - N_GROUPS).astype(_F32),
                  jnp.where(lane == 2, w0,
                            jnp.where(lane == 3, w1,
                                      jnp.where(lane == 4, rank0, jnp.where(lane == 5, rank1, 0.0))))))


def _merge(lid, x, y_a, y_b, y_c, w_gt, w_br, w_o, ln_g, ln_b, w_route, b_route):
    S = x.shape[0]
    tm = ROW_TILE
    tril = jnp.tril(jnp.ones((tm, tm), _F32), -1).astype(_MXU)
    br_w = SC_WIDTH + GLA_VW + MLA_VW
    return pl.pallas_call(
        _merge_kernel,
        name="merge_route",
        grid_spec=pltpu.PrefetchScalarGridSpec(
            num_scalar_prefetch=1,
            grid=(S // tm,),
            in_specs=[pl.BlockSpec((tm, D_MODEL), lambda i, l: (i, 0)),
                      pl.BlockSpec((tm, SC_WIDTH), lambda i, l: (i, 0)),
                      pl.BlockSpec((tm, GLA_VW), lambda i, l: (i, 0)),
                      pl.BlockSpec((tm, MLA_VW), lambda i, l: (i, 0)),
                      pl.BlockSpec((None, D_MODEL, 3 * D_MODEL), lambda i, l: (l[0], 0, 0)),
                      pl.BlockSpec((None, br_w, D_MODEL), lambda i, l: (l[0], 0, 0)),
                      pl.BlockSpec((None, D_MODEL, D_MODEL), lambda i, l: (l[0], 0, 0)),
                      pl.BlockSpec((None, 1, D_MODEL), lambda i, l: (l[0], 0, 0)),
                      pl.BlockSpec((None, 1, D_MODEL), lambda i, l: (l[0], 0, 0)),
                      pl.BlockSpec((None, D_MODEL, LANES), lambda i, l: (l[0], 0, 0)),
                      pl.BlockSpec((None, 1, LANES), lambda i, l: (l[0], 0, 0)),
                      pl.BlockSpec((tm, tm), lambda i, l: (0, 0))],
            out_specs=[pl.BlockSpec((tm, D_MODEL), lambda i, l: (i, 0)),
                       pl.BlockSpec((tm * TOKEN_ROWS, LANES), lambda i, l: (i, 0)),
                       pl.BlockSpec((tm, LANES), lambda i, l: (i, 0)),
                       pl.BlockSpec((8, LANES), lambda i, l: (0, 0))],
            scratch_shapes=[pltpu.VMEM((8, LANES), _F32)]),
        out_shape=[jax.ShapeDtypeStruct((S, D_MODEL), _F32),
                   jax.ShapeDtypeStruct((S * TOKEN_ROWS, LANES), _F32),
                   jax.ShapeDtypeStruct((S, LANES), _F32),
                   jax.ShapeDtypeStruct((8, LANES), _F32)],
        compiler_params=_params(),
    )(lid, x, y_a, y_b, y_c, w_gt, w_br, w_o, ln_g, ln_b, w_route, b_route, tril)


def _dispatch_plan(route, counts_row, n_tokens):
    counts = counts_row[0, N_GROUPS:N_GROUPS + N_EXPERTS].astype(jnp.int32)
    padded = (counts + MOE_BM - 1) // MOE_BM * MOE_BM
    pad_end = jnp.cumsum(padded)
    pad_start = pad_end - padded
    expert_id = route[:, :TOP_K].astype(jnp.int32)
    rank = route[:, 4:4 + TOP_K].astype(jnp.int32)
    start_of = jnp.sum(jnp.where(expert_id[..., None] == jnp.arange(N_EXPERTS, dtype=jnp.int32),
                                 pad_start.astype(jnp.int32), 0), axis=-1)
    dest = (start_of + rank).reshape(n_tokens * TOP_K)
    nb = n_tokens * TOP_K // MOE_BM + N_EXPERTS
    block_expert = jnp.minimum(
        jnp.searchsorted(pad_end, jnp.arange(nb, dtype=jnp.int32) * MOE_BM, side='right'),
        N_EXPERTS - 1).astype(jnp.int32)
    n_active = (pad_end[-1] // MOE_BM).astype(jnp.int32)
    first = jnp.concatenate([jnp.ones((1,), jnp.int32),
                             (block_expert[1:] != block_expert[:-1]).astype(jnp.int32)])
    meta = jnp.concatenate([n_active.reshape(1), jnp.zeros((1,), jnp.int32)])
    return dest, counts, pad_start.astype(jnp.int32), block_expert, first, meta


def _token_copy(src_hbm, src_row, dst_ref, dst_row, sem):
    return pltpu.make_async_copy(src_hbm.at[pl.ds(src_row * TOKEN_ROWS, TOKEN_ROWS), :],
                                 dst_ref.at[pl.ds(dst_row * TOKEN_ROWS, TOKEN_ROWS), :], sem)


def _dispatch_kernel(cnt, pstart, meta, dest_ref, x_ref, xs_hbm, zero_scr, sem):
    i = pl.program_id(0)
    tt = DISPATCH_TILE
    blk = MOE_BM * TOKEN_ROWS
    n_blocks = xs_hbm.shape[0] // blk

    def spare_blocks(fn):
        def one(b, c):
            fn(pltpu.make_async_copy(zero_scr, xs_hbm.at[pl.ds(b * blk, blk), :], sem.at[2]))
            return c

        lax.fori_loop(meta[0], n_blocks, one, 0)

    @pl.when(i == 0)
    def _():
        zero_scr[...] = jnp.zeros(zero_scr.shape, _F32)
        spare_blocks(lambda cp: cp.start())

    def start(j, c):
        for kk in range(TOP_K):
            _token_copy(x_ref, j, xs_hbm, dest_ref[0, 0, TOP_K * j + kk], sem.at[0]).start(priority=kk)
        return c

    lax.fori_loop(0, tt, start, 0, unroll=8)

    def pad_rows(fn):
        def per_expert(e, c):
            n = cnt[e]
            n_pad = (n + MOE_BM - 1) // MOE_BM * MOE_BM

            def one(r, c2):
                fn(_token_copy(x_ref, 0, xs_hbm, pstart[e] + r, sem.at[1]))
                return c2

            lax.fori_loop(n, n_pad, one, 0)
            return c

        lax.fori_loop(0, N_EXPERTS, per_expert, 0)

    @pl.when(i == 0)
    def _():
        pad_rows(lambda cp: cp.start())
        pad_rows(lambda cp: cp.wait())
        spare_blocks(lambda cp: cp.wait())

    def wait(j, c):
        for kk in range(TOP_K):
            _token_copy(x_ref, 0, xs_hbm, 0, sem.at[0]).wait()
        return c

    lax.fori_loop(0, tt, wait, 0, unroll=8)


def _dispatch(counts, pad_start, meta, dest, x_tok, n_rows):
    n_tokens = x_tok.shape[0] // TOKEN_ROWS
    tt = DISPATCH_TILE
    return pl.pallas_call(
        _dispatch_kernel,
        name="moe_dispatch",
        grid_spec=pltpu.PrefetchScalarGridSpec(
            num_scalar_prefetch=3,
            grid=(n_tokens // tt,),
            in_specs=[pl.BlockSpec((1, 1, TOP_K * tt), lambda i, c, s, m: (i, 0, 0),
                                   memory_space=pltpu.SMEM),
                      pl.BlockSpec((tt * TOKEN_ROWS, LANES), lambda i, c, s, m: (i, 0))],
            out_specs=pl.BlockSpec(memory_space=pl.ANY),
            scratch_shapes=[pltpu.VMEM((MOE_BM * TOKEN_ROWS, LANES), _F32),
                            pltpu.SemaphoreType.DMA((3,))]),
        out_shape=jax.ShapeDtypeStruct((n_rows * TOKEN_ROWS, LANES), _F32),
        compiler_params=_params(),
    )(counts, pad_start, meta, dest.reshape(n_tokens // tt, 1, TOP_K * tt), x_tok)


def _from_token_tiles(ref, n):
    return jnp.concatenate([ref[pl.ds(c, n, stride=TOKEN_ROWS), :] for c in range(TOKEN_ROWS)], axis=1)


def _expert_kernel(lid, bexp, first, meta, xs_ref, wg_ref, wu_ref, wd_ref, o_ref, wg_b, wu_b, wd_b):
    b = pl.program_id(0)

    @pl.when(b >= meta[0])
    def _():
        o_ref[...] = jnp.zeros(o_ref.shape, o_ref.dtype)

    @pl.when(b < meta[0])
    def _():
        @pl.when(first[b] == 1)
        def _():
            wg_b[...] = wg_ref[...].astype(_MXU)
            wu_b[...] = wu_ref[...].astype(_MXU)
            wd_b[...] = wd_ref[...].astype(_MXU)

        xb = _from_token_tiles(xs_ref, MOE_BM).astype(_MXU)
        hg = _dot(xb, wg_b[...])
        hu = _dot(xb, wu_b[...])
        hid = (hg * _sigmoid(hg)) * hu
        y = _dot(hid.astype(_MXU), wd_b[...])
        for c in range(TOKEN_ROWS):
            o_ref[pl.ds(c, MOE_BM, stride=TOKEN_ROWS), :] = y[:, c * LANES:(c + 1) * LANES]


def _experts(lid, block_expert, first, meta, xs_tok, w_gate, w_up, w_down):
    rows = MOE_BM * TOKEN_ROWS
    nb = xs_tok.shape[0] // rows
    live = lambda b, m: jnp.minimum(b, m[0] - 1)
    return pl.pallas_call(
        _expert_kernel,
        name="moe_experts",
        grid_spec=pltpu.PrefetchScalarGridSpec(
            num_scalar_prefetch=4,
            grid=(nb,),
            in_specs=[pl.BlockSpec((rows, LANES), lambda b, l, e, f, m: (live(b, m), 0)),
                      pl.BlockSpec((None, None, D_MODEL, EXPERT_HIDDEN),
                                   lambda b, l, e, f, m: (l[0], e[b], 0, 0)),
                      pl.BlockSpec((None, None, D_MODEL, EXPERT_HIDDEN),
                                   lambda b, l, e, f, m: (l[0], e[b], 0, 0)),
                      pl.BlockSpec((None, None, EXPERT_HIDDEN, D_MODEL),
                                   lambda b, l, e, f, m: (l[0], e[b], 0, 0))],
            out_specs=pl.BlockSpec((rows, LANES), lambda b, l, e, f, m: (b, 0)),
            scratch_shapes=[pltpu.VMEM((D_MODEL, EXPERT_HIDDEN), _MXU),
                            pltpu.VMEM((D_MODEL, EXPERT_HIDDEN), _MXU),
                            pltpu.VMEM((EXPERT_HIDDEN, D_MODEL), _MXU)]),
        out_shape=jax.ShapeDtypeStruct(xs_tok.shape, _F32),
        compiler_params=_params(),
    )(lid, block_expert, first, meta, xs_tok, w_gate, w_up, w_down)


def _combine_kernel(lid, pos_ref, pos_next_ref, y_hbm, x_ref, route_ref, g2_ref, b2_ref, wpg_ref, bpg_ref,
                    p_ref, wpu_ref, g3_ref, b3_ref, o_ref, ybuf, sem):
    tc = x_ref.shape[0]
    i = pl.program_id(0)
    slot = lax.rem(i, 2)

    def gather(rows_ref, s, fn):
        def body(r, c):
            for kk in range(TOP_K):
                fn(_token_copy(y_hbm, rows_ref[0, 0, TOP_K * r + kk], ybuf.at[s, kk], r, sem.at[s]), kk)
            return c

        lax.fori_loop(0, tc, body, 0, unroll=8)

    start = lambda cp, kk: cp.start(priority=kk)

    @pl.when(i == 0)
    def _():
        gather(pos_ref, 0, start)

    @pl.when(i + 1 < pl.num_programs(0))
    def _():
        gather(pos_next_ref, 1 - slot, start)

    gather(pos_ref, slot, lambda cp, kk: cp.wait())
    route = route_ref[...]
    y = (route[:, 2:3] * _from_token_tiles(ybuf.at[slot, 0], tc)
         + route[:, 3:4] * _from_token_tiles(ybuf.at[slot, 1], tc))
    x2 = _layer_norm(DN_ALPHA * x_ref[...] + y, g2_ref[...], b2_ref[...])
    gate = _sigmoid(_dot(x2.astype(_MXU), wpg_ref[...]) + bpg_ref[...])
    up = _dot(p_ref[...].astype(_MXU), wpu_ref[...])
    o_ref[...] = _layer_norm(DN_ALPHA * x2 + gate * up, g3_ref[...], b3_ref[...])


def _combine(lid, dest, y_tok, x, route, ln2_g, ln2_b, w_pg, b_pg, p, w_pu, ln3_g, ln3_b):
    S = x.shape[0]
    tc = COMBINE_TILE
    n_tiles = S // tc
    pos = dest.reshape(n_tiles, 1, TOP_K * tc)
    vec = lambda n: pl.BlockSpec((None, 1, n), lambda i, l: (l[0], 0, 0))
    return pl.pallas_call(
        _combine_kernel,
        name="combine_ple",
        grid_spec=pltpu.PrefetchScalarGridSpec(
            num_scalar_prefetch=1,
            grid=(S // tc,),
            in_specs=[pl.BlockSpec((1, 1, TOP_K * tc), lambda i, l: (i, 0, 0), memory_space=pltpu.SMEM),
                      pl.BlockSpec((1, 1, TOP_K * tc), lambda i, l: (jnp.minimum(i + 1, n_tiles - 1), 0, 0),
                                   memory_space=pltpu.SMEM),
                      pl.BlockSpec(memory_space=pl.ANY),
                      pl.BlockSpec((tc, D_MODEL), lambda i, l: (i, 0)),
                      pl.BlockSpec((tc, LANES), lambda i, l: (i, 0)),
                      vec(D_MODEL), vec(D_MODEL),
                      pl.BlockSpec((None, D_MODEL, D_MODEL), lambda i, l: (l[0], 0, 0)),
                      vec(D_MODEL),
                      pl.BlockSpec((None, tc, PLE_DIM), lambda i, l: (l[0], i, 0)),
                      pl.BlockSpec((None, PLE_DIM, D_MODEL), lambda i, l: (l[0], 0, 0)),
                      vec(D_MODEL), vec(D_MODEL)],
            out_specs=pl.BlockSpec((tc, D_MODEL), lambda i, l: (i, 0)),
            scratch_shapes=[pltpu.VMEM((2, TOP_K, tc * TOKEN_ROWS, LANES), _F32),
                            pltpu.SemaphoreType.DMA((2,))]),
        out_shape=jax.ShapeDtypeStruct((S, D_MODEL), _F32),
        compiler_params=_params(),
    )(lid, pos, pos, y_tok, x, route, ln2_g, ln2_b, w_pg, b_pg, p, w_pu, ln3_g, ln3_b)


def _split_in_proj(w_in):
    sizes = (SC_WIDTH, SC_WIDTH, SC_WIDTH, GLA_QK, GLA_QK, GLA_VW, GLA_GATE_RANK, GLA_VW,
             MLA_Q_RANK, MLA_KV_RANK, MLA_ROPE, D_MODEL, D_MODEL, D_MODEL)
    offs = [0]
    for s in sizes:
        offs.append(offs[-1] + s)
    col = lambda j: w_in[:, :, offs[j]:offs[j + 1]]
    L = w_in.shape[0]
    zeros = lambda n: jnp.zeros((L, D_MODEL, n), w_in.dtype)
    w_conv_in = jnp.concatenate([col(0), col(1), col(2)], axis=-1)
    w_gla_in = jnp.concatenate([col(3), col(4), col(5), col(7), col(6), zeros(LANES - GLA_GATE_RANK)], axis=-1)
    w_mla_in = jnp.concatenate([col(8), col(9), col(10), zeros(LANES - MLA_ROPE)], axis=-1)
    w_gates = jnp.concatenate([col(11), col(12), col(13)], axis=-1)
    return [w.astype(_MXU) for w in (w_conv_in, w_gla_in, w_mla_in, w_gates)]


def kernel(x, p, positions, ln0_g, ln0_b, w_in, w_conv, w_gla_gate, b_gla_gate, gla_norm_g, mla_q_norm_g, mla_kv_norm_g, w_uq, w_ukv, w_br, w_o, ln1_g, ln1_b, w_grp, b_grp, w_exp, b_exp, w_gate, w_up, w_down, ln2_g, ln2_b, w_ple_gate, b_ple_gate, w_ple_up, ln3_g, ln3_b):
    B, S, D = x.shape
    L = w_in.shape[0]
    assert B == 1 and D == D_MODEL and S % ROW_TILE == 0 and S % ATTN_TQ == 0
    H = MLA_HEADS

    w_conv_in, w_gla_in, w_mla_in, w_gates = _split_in_proj(w_in)
    w_gg = jnp.pad(w_gla_gate, ((0, 0), (0, LANES - GLA_GATE_RANK), (0, 0))).astype(_MXU)
    w_uq_p = jnp.pad(w_uq.reshape(L, MLA_Q_RANK, H, MLA_QK),
                     ((0, 0), (0, 0), (0, 0), (0, MLA_QK_PAD - MLA_QK))
                     ).reshape(L, MLA_Q_RANK, H * MLA_QK_PAD).astype(_MXU)
    w_ukv4 = w_ukv.reshape(L, MLA_KV_RANK, H, MLA_NOPE + MLA_V)
    w_uk = w_ukv4[..., :MLA_NOPE].reshape(L, MLA_KV_RANK, H * MLA_NOPE).astype(_MXU)
    w_uv = w_ukv4[..., MLA_NOPE:].reshape(L, MLA_KV_RANK, H * MLA_V).astype(_MXU)
    w_route = jnp.concatenate(
        [w_grp, w_exp.reshape(L, D, N_EXPERTS),
         jnp.zeros((L, D, LANES - N_GROUPS - N_EXPERTS), _F32)], axis=-1).astype(_MXU)
    b_route = jnp.concatenate(
        [b_grp, b_exp.reshape(L, N_EXPERTS),
         jnp.zeros((L, LANES - N_GROUPS - N_EXPERTS), _F32)], axis=-1).reshape(L, 1, LANES)
    w_br_c = w_br.astype(_MXU)
    w_o_c = w_o.astype(_MXU)
    w_pg_c = w_ple_gate.astype(_MXU)
    w_pu_c = w_ple_up.astype(_MXU)
    row = lambda a: a.reshape(L, 1, a.shape[-1])

    cos_t, sin_t = _rope_tables(positions.reshape(S))
    x0 = _entry_norm(x.reshape(S, D), ln0_g, ln0_b)
    p3 = p.reshape(L, S, PLE_DIM)

    def layer(xc, i):
        lid = jnp.reshape(i, (1,)).astype(jnp.int32)
        y_a = _conv_branch(lid, xc, w_conv_in, w_conv)
        y_b = _gla_branch(lid, xc, w_gla_in, w_gg, row(b_gla_gate), row(gla_norm_g))
        q, k, v = _mla_prep(lid, xc, w_mla_in, row(mla_q_norm_g), row(mla_kv_norm_g),
                            w_uq_p, w_uk, w_uv, cos_t, sin_t)
        y_c = _attention(q, k, v)
        x1, x1_tok, route, counts_row = _merge(lid, xc, y_a, y_b, y_c, w_gates, w_br_c, w_o_c,
                                               row(ln1_g), row(ln1_b), w_route, b_route)
        dest, counts, pad_start, block_expert, first, meta = _dispatch_plan(route, counts_row, S)
        n_rows = (S * TOP_K // MOE_BM + N_EXPERTS) * MOE_BM
        xs_tok = _dispatch(counts, pad_start, meta, dest, x1_tok, n_rows)
        y_tok = _experts(lid, block_expert, first, meta, xs_tok, w_gate, w_up, w_down)
        x3 = _combine(lid, dest, y_tok, x1, route, row(ln2_g), row(ln2_b), w_pg_c, row(b_ple_gate),
                      p3, w_pu_c, row(ln3_g), row(ln3_b))
        return x3, None

    out, _ = lax.scan(layer, x0, jnp.arange(L, dtype=jnp.int32))
    return out.reshape(B, S, D)
```

```python
import functools

import jax
import jax.numpy as jnp
from jax import lax
from jax.experimental import pallas as pl
from jax.experimental.pallas import tpu as pltpu

D_MODEL = 1024
DEPTH = 4
PLE_DIM = 256
SC_WIDTH = 512
SC_CONV = 3
GLA_HEADS = 4
GLA_DK = 64
GLA_DV = 128
GLA_GATE_RANK = 16
GLA_TAU = 16.0
GLA_CHUNK = 64
MLA_HEADS = 4
MLA_NOPE = 128
MLA_ROPE = 64
MLA_V = 128
MLA_Q_RANK = 256
MLA_KV_RANK = 128
ROPE_THETA = 10000.0
N_GROUPS = 8
EXPERTS_PER_GROUP = 8
N_EXPERTS = N_GROUPS * EXPERTS_PER_GROUP
TOP_K = 2
EXPERT_HIDDEN = 256
DN_ALPHA = (2 * DEPTH) ** 0.25
LN_EPS = 1e-5
RMS_EPS = 1e-6

GLA_QK = GLA_HEADS * GLA_DK
GLA_VW = GLA_HEADS * GLA_DV
MLA_QK = MLA_NOPE + MLA_ROPE
MLA_VW = MLA_HEADS * MLA_V

LANES = 128
MLA_QK_PAD = 2 * LANES
VMEM_LIMIT = 56 * 1024 * 1024

ROW_TILE = 512
ATTN_TQ = 1024
ATTN_TK = 512
ATTN_ROWS = 256
LOG2_E = 1.4426950408889634
MOE_BM = 256
COMBINE_TILE = 256
DISPATCH_TILE = 1024
TOKEN_ROWS = D_MODEL // LANES

_MXU = jnp.bfloat16
_F32 = jnp.float32


def _dot(a, b):
    return jnp.dot(a, b, preferred_element_type=_F32)


def _dot_nt(a, b):
    return lax.dot_general(a, b, (((1,), (1,)), ((), ())), preferred_element_type=_F32)


def _dot_tn(a, b):
    return lax.dot_general(a, b, (((0,), (0,)), ((), ())), preferred_element_type=_F32)


def _layer_norm(x, g, b):
    mu = jnp.mean(x, axis=-1, keepdims=True)
    xc = x - mu
    var = jnp.mean(xc * xc, axis=-1, keepdims=True)
    return xc * lax.rsqrt(var + LN_EPS) * g + b


def _rms_norm(x, g):
    return x * lax.rsqrt(jnp.mean(x * x, axis=-1, keepdims=True) + RMS_EPS) * g


def _sigmoid(x):
    return 1.0 / (1.0 + jnp.exp(-x))


def _params(n_axes=1):
    return pltpu.CompilerParams(dimension_semantics=("arbitrary",) * n_axes,
                                vmem_limit_bytes=VMEM_LIMIT)


def _ln0_kernel(x_ref, g_ref, b_ref, o_ref):
    o_ref[...] = _layer_norm(x_ref[...], g_ref[...], b_ref[...])


def _entry_norm(x, g, b):
    S = x.shape[0]
    return pl.pallas_call(
        _ln0_kernel,
        name="entry_ln",
        grid=(S // ROW_TILE,),
        in_specs=[pl.BlockSpec((ROW_TILE, D_MODEL), lambda i: (i, 0)),
                  pl.BlockSpec((1, D_MODEL), lambda i: (0, 0)),
                  pl.BlockSpec((1, D_MODEL), lambda i: (0, 0))],
        out_specs=pl.BlockSpec((ROW_TILE, D_MODEL), lambda i: (i, 0)),
        out_shape=jax.ShapeDtypeStruct((S, D_MODEL), _F32),
        compiler_params=_params(),
    )(x, g.reshape(1, D_MODEL), b.reshape(1, D_MODEL))


def _rope_kernel(pos_ref, inv_ref, c_ref, s_ref):
    ang = pos_ref[...].astype(_F32) * inv_ref[...]
    lane = lax.broadcasted_iota(jnp.int32, ang.shape, 1)
    half = MLA_ROPE // 2
    cosv = jnp.cos(ang)
    sinv = jnp.sin(ang)
    c_ref[...] = jnp.where(lane < MLA_ROPE, cosv, 0.0)
    s_ref[...] = jnp.where(lane < half, -sinv, jnp.where(lane < MLA_ROPE, sinv, 0.0))


def _rope_tables(positions):
    S = positions.shape[0]
    half = MLA_ROPE // 2
    inv = 1.0 / (ROPE_THETA ** (jnp.arange(0, MLA_ROPE, 2, dtype=_F32) / MLA_ROPE))
    inv = jnp.tile(inv, LANES // half).reshape(1, LANES)
    return pl.pallas_call(
        _rope_kernel,
        name="rope_tables",
        grid=(S // ROW_TILE,),
        in_specs=[pl.BlockSpec((ROW_TILE, 1), lambda i: (i, 0)),
                  pl.BlockSpec((1, LANES), lambda i: (0, 0))],
        out_specs=[pl.BlockSpec((ROW_TILE, LANES), lambda i: (i, 0)),
                   pl.BlockSpec((ROW_TILE, LANES), lambda i: (i, 0))],
        out_shape=[jax.ShapeDtypeStruct((S, LANES), _F32)] * 2,
        compiler_params=_params(),
    )(positions.reshape(S, 1), inv)


def _conv_kernel(lid, x_ref, w_ref, wc_ref, o_ref, u_scr):
    tm = x_ref.shape[0]

    @pl.when(pl.program_id(0) == 0)
    def _():
        u_scr[0:8, :] = jnp.zeros((8, SC_WIDTH), _F32)

    h = _dot(x_ref[...].astype(_MXU), w_ref[...])
    a_b = h[:, :SC_WIDTH]
    u = h[:, SC_WIDTH:2 * SC_WIDTH] * h[:, 2 * SC_WIDTH:]
    u_scr[8:8 + tm, :] = u
    wc = wc_ref[...]
    y = wc[2:3, :] * u + wc[1:2, :] * u_scr[7:7 + tm, :] + wc[0:1, :] * u_scr[6:6 + tm, :]
    o_ref[...] = (a_b * y).astype(o_ref.dtype)
    u_scr[0:8, :] = u_scr[tm:tm + 8, :]


def _conv_branch(lid, x, w_conv_in, w_conv):
    S = x.shape[0]
    tm = ROW_TILE
    return pl.pallas_call(
        _conv_kernel,
        name="conv_branch",
        grid_spec=pltpu.PrefetchScalarGridSpec(
            num_scalar_prefetch=1,
            grid=(S // tm,),
            in_specs=[pl.BlockSpec((tm, D_MODEL), lambda i, l: (i, 0)),
                      pl.BlockSpec((None, D_MODEL, 3 * SC_WIDTH), lambda i, l: (l[0], 0, 0)),
                      pl.BlockSpec((None, SC_CONV, SC_WIDTH), lambda i, l: (l[0], 0, 0))],
            out_specs=pl.BlockSpec((tm, SC_WIDTH), lambda i, l: (i, 0)),
            scratch_shapes=[pltpu.VMEM((tm + 8, SC_WIDTH), _F32)]),
        out_shape=jax.ShapeDtypeStruct((S, SC_WIDTH), _MXU),
        compiler_params=_params(),
    )(lid, x, w_conv_in, w_conv)


_GLA_IN = 2 * GLA_QK + 2 * GLA_VW + LANES


def _gla_kernel(lid, x_ref, w_ref, wg_ref, bg_ref, ng_ref, o_ref, st_scr):
    tm = x_ref.shape[0]
    C = GLA_CHUNK

    @pl.when(pl.program_id(0) == 0)
    def _():
        st_scr[...] = jnp.zeros(st_scr.shape, _F32)

    h = _dot(x_ref[...].astype(_MXU), w_ref[...])
    q = h[:, :GLA_QK]
    k = h[:, GLA_QK:2 * GLA_QK]
    v = h[:, 2 * GLA_QK:2 * GLA_QK + GLA_VW]
    r = h[:, 2 * GLA_QK + GLA_VW:2 * GLA_QK + 2 * GLA_VW]
    g_lr = h[:, 2 * GLA_QK + 2 * GLA_VW:]
    z = _dot(g_lr.astype(_MXU), wg_ref[...]) + bg_ref[...]
    log_a = (jnp.minimum(z, 0.0) - jnp.log1p(jnp.exp(-jnp.abs(z)))) * (1.0 / GLA_TAU)

    row_in_chunk = lax.broadcasted_iota(jnp.int32, log_a.shape, 0) & (C - 1)
    b = log_a
    d = 1
    while d < C:
        b = b + jnp.where(row_in_chunk >= d, pltpu.roll(b, d, 0), 0.0)
        d *= 2

    ri = lax.broadcasted_iota(jnp.int32, (C, C), 0)
    ci = lax.broadcasted_iota(jnp.int32, (C, C), 1)
    causal = ci <= ri
    scale = GLA_DK ** -0.5
    ng = ng_ref[...]
    for c in range(tm // C):
        rows = slice(c * C, (c + 1) * C)
        bc = b[rows]
        b_last = bc[C - 1:C, :]
        kc = k[rows]
        q_e = (q[rows] * scale) * jnp.exp(bc)
        k_e = kc * jnp.exp(-bc)
        k_t = kc * jnp.exp(b_last - bc)
        decay = jnp.exp(b_last)
        for hh in range(GLA_HEADS):
            ks = slice(hh * GLA_DK, (hh + 1) * GLA_DK)
            vs = slice(hh * GLA_DV, (hh + 1) * GLA_DV)
            qh = q_e[:, ks].astype(_MXU)
            vh = v[rows, vs].astype(_MXU)
            a = jnp.where(causal, _dot_nt(qh, k_e[:, ks].astype(_MXU)), 0.0)
            st = st_scr[hh]
            o = _dot(a.astype(_MXU), vh) + _dot_nt(qh, st.astype(_MXU))
            st_scr[hh] = st * decay[:, ks] + _dot_tn(vh, k_t[:, ks].astype(_MXU))
            rr = r[rows, vs]
            o_ref[rows, vs] = (_rms_norm(o, ng) * (rr * _sigmoid(rr))).astype(o_ref.dtype)


def _gla_branch(lid, x, w_gla_in, w_gate, b_gate, norm_g):
    S = x.shape[0]
    tm = ROW_TILE
    return pl.pallas_call(
        _gla_kernel,
        name="gla_branch",
        grid_spec=pltpu.PrefetchScalarGridSpec(
            num_scalar_prefetch=1,
            grid=(S // tm,),
            in_specs=[pl.BlockSpec((tm, D_MODEL), lambda i, l: (i, 0)),
                      pl.BlockSpec((None, D_MODEL, _GLA_IN), lambda i, l: (l[0], 0, 0)),
                      pl.BlockSpec((None, LANES, GLA_QK), lambda i, l: (l[0], 0, 0)),
                      pl.BlockSpec((None, 1, GLA_QK), lambda i, l: (l[0], 0, 0)),
                      pl.BlockSpec((None, 1, GLA_DV), lambda i, l: (l[0], 0, 0))],
            out_specs=pl.BlockSpec((tm, GLA_VW), lambda i, l: (i, 0)),
            scratch_shapes=[pltpu.VMEM((GLA_HEADS, GLA_DV, GLA_DK), _F32)]),
        out_shape=jax.ShapeDtypeStruct((S, GLA_VW), _MXU),
        compiler_params=_params(),
    )(lid, x, w_gla_in, w_gate, b_gate, norm_g)


_MLA_IN = MLA_Q_RANK + MLA_KV_RANK + LANES


def _mla_prep_kernel(lid, x_ref, w_ref, qg_ref, kvg_ref, wuq_ref, wk_ref, wv_ref, ct_ref, st_ref,
                     q_o, k_o, v_o):
    h = _dot(x_ref[...].astype(_MXU), w_ref[...])
    c_q = _rms_norm(h[:, :MLA_Q_RANK], qg_ref[...]).astype(_MXU)
    c_kv = _rms_norm(h[:, MLA_Q_RANK:MLA_Q_RANK + MLA_KV_RANK], kvg_ref[...]).astype(_MXU)
    k_rope_raw = h[:, MLA_Q_RANK + MLA_KV_RANK:]
    q = _dot(c_q, wuq_ref[...])
    k_nope = _dot(c_kv, wk_ref[...])
    v = _dot(c_kv, wv_ref[...])
    cos_t = ct_ref[...]
    sin_t = st_ref[...]
    lane = lax.broadcasted_iota(jnp.int32, cos_t.shape, 1)
    half = MLA_ROPE // 2

    def rope(xr):
        rot = jnp.where(lane < half, pltpu.roll(xr, LANES - half, 1), pltpu.roll(xr, half, 1))
        return xr * cos_t + rot * sin_t

    k_rope = rope(k_rope_raw)
    one_col = jnp.where(lane == 0, 1.0, 0.0)
    scale = MLA_QK ** -0.5 * LOG2_E
    for hh in range(MLA_HEADS):
        base = hh * MLA_QK_PAD
        q_h = jnp.concatenate([q[:, base:base + MLA_NOPE], rope(q[:, base + MLA_NOPE:base + MLA_QK_PAD])],
                              axis=1)
        q_o[hh] = (q_h * scale).astype(q_o.dtype)
        k_o[hh] = jnp.concatenate([k_nope[:, hh * MLA_NOPE:(hh + 1) * MLA_NOPE], k_rope],
                                  axis=1).astype(k_o.dtype)
        v_o[hh] = jnp.concatenate([v[:, hh * MLA_V:(hh + 1) * MLA_V], one_col], axis=1).astype(v_o.dtype)


def _mla_prep(lid, x, w_mla_in, q_norm_g, kv_norm_g, w_uq, w_uk, w_uv, cos_t, sin_t):
    S = x.shape[0]
    tm = ROW_TILE
    H = MLA_HEADS
    return pl.pallas_call(
        _mla_prep_kernel,
        name="mla_prep",
        grid_spec=pltpu.PrefetchScalarGridSpec(
            num_scalar_prefetch=1,
            grid=(S // tm,),
            in_specs=[pl.BlockSpec((tm, D_MODEL), lambda i, l: (i, 0)),
                      pl.BlockSpec((None, D_MODEL, _MLA_IN), lambda i, l: (l[0], 0, 0)),
                      pl.BlockSpec((None, 1, MLA_Q_RANK), lambda i, l: (l[0], 0, 0)),
                      pl.BlockSpec((None, 1, MLA_KV_RANK), lambda i, l: (l[0], 0, 0)),
                      pl.BlockSpec((None, MLA_Q_RANK, H * MLA_QK_PAD), lambda i, l: (l[0], 0, 0)),
                      pl.BlockSpec((None, MLA_KV_RANK, H * MLA_NOPE), lambda i, l: (l[0], 0, 0)),
                      pl.BlockSpec((None, MLA_KV_RANK, H * MLA_V), lambda i, l: (l[0], 0, 0)),
                      pl.BlockSpec((tm, LANES), lambda i, l: (i, 0)),
                      pl.BlockSpec((tm, LANES), lambda i, l: (i, 0))],
            out_specs=[pl.BlockSpec((H, tm, MLA_QK_PAD), lambda i, l: (0, i, 0)),
                       pl.BlockSpec((H, tm, MLA_QK_PAD), lambda i, l: (0, i, 0)),
                       pl.BlockSpec((H, tm, MLA_V + LANES), lambda i, l: (0, i, 0))]),
        out_shape=[jax.ShapeDtypeStruct((H, S, MLA_QK_PAD), _MXU),
                   jax.ShapeDtypeStruct((H, S, MLA_QK_PAD), _MXU),
                   jax.ShapeDtypeStruct((H, S, MLA_V + LANES), _MXU)],
        compiler_params=_params(),
    )(lid, x, w_mla_in, q_norm_g, kv_norm_g, w_uq, w_uk, w_uv, cos_t, sin_t)


def _attn_kernel(q_ref, k_ref, v_ref, o_ref, s_a, s_b, m_scr, acc_scr):
    tq = q_ref.shape[0]
    tk = ATTN_TK
    rb = ATTN_ROWS
    n_rb = tq // rb
    qi = pl.program_id(1)
    m_scr[...] = jnp.full(m_scr.shape, -jnp.inf, _F32)
    acc_scr[...] = jnp.zeros(acc_scr.shape, _F32)

    def scores(dst, start, first_rb=0):
        k = k_ref[pl.ds(start, tk), :]
        for r in range(first_rb, n_rb):
            rows = slice(r * rb, (r + 1) * rb)
            dst[rows, :] = _dot_nt(q_ref[rows, :], k)

    def consume(src, start, col0=None):
        for r in range(n_rb):
            row0 = r * rb
            rows = slice(row0, row0 + rb)
            n_cols = tk
            if col0 is not None:
                n_cols = min(tk, row0 + rb - col0)
                if n_cols <= 0:
                    continue
            s = src[rows, :n_cols]
            if col0 is not None and col0 + n_cols - 1 > row0:
                ri = lax.broadcasted_iota(jnp.int32, s.shape, 0) + row0
                ci = lax.broadcasted_iota(jnp.int32, s.shape, 1) + col0
                s = jnp.where(ci <= ri, s, -jnp.inf)
            v = v_ref[pl.ds(start, n_cols), :]
            m_prev = m_scr[rows, :]
            m_new = jnp.maximum(m_prev, jnp.max(s, axis=1, keepdims=True))
            alpha = jnp.exp2(m_prev - m_new)
            p = jnp.exp2(s - m_new[:, :1]).astype(v.dtype)
            acc_scr[rows, :] = jnp.concatenate([alpha, alpha], axis=1) * acc_scr[rows, :] + _dot(p, v)
            m_scr[rows, :] = m_new

    col = lambda j: pl.multiple_of(j * tk, tk)
    scores(s_a, col(0))

    def body(i, carry):
        scores(s_b, col(2 * i + 1))
        consume(s_a, col(2 * i))
        scores(s_a, col(2 * i + 2))
        consume(s_b, col(2 * i + 1))
        return carry

    lax.fori_loop(0, qi, body, 0)
    scores(s_b, col(2 * qi + 1), first_rb=tk // rb)
    consume(s_a, col(2 * qi), col0=0)
    consume(s_b, col(2 * qi + 1), col0=tk)
    o_ref[...] = (acc_scr[:, :MLA_V] / acc_scr[:, MLA_V:MLA_V + 1]).astype(o_ref.dtype)


def _attention(q, k, v):
    H, S, _ = q.shape
    tq = ATTN_TQ
    return pl.pallas_call(
        _attn_kernel,
        name="mla_attention",
        grid=(H, S // tq),
        in_specs=[pl.BlockSpec((None, tq, MLA_QK_PAD), lambda h, i: (h, i, 0)),
                  pl.BlockSpec((None, S, MLA_QK_PAD), lambda h, i: (h, 0, 0)),
                  pl.BlockSpec((None, S, MLA_V + LANES), lambda h, i: (h, 0, 0))],
        out_specs=pl.BlockSpec((tq, MLA_V), lambda h, i: (i, h)),
        out_shape=jax.ShapeDtypeStruct((S, MLA_VW), _MXU),
        scratch_shapes=[pltpu.VMEM((tq, ATTN_TK), _F32),
                        pltpu.VMEM((tq, ATTN_TK), _F32),
                        pltpu.VMEM((tq, LANES), _F32),
                        pltpu.VMEM((tq, MLA_V + LANES), _F32)],
        compiler_params=_params(2),
    )(q, k, v)


def _merge_kernel(lid, x_ref, ya_ref, yb_ref, yc_ref, wgt_ref, wbr_ref, wo_ref, g_ref, b_ref,
                  wr_ref, br_ref, tril_ref, x_o, xt_o, route_o, cnt_o, cnt_scr):
    tm = x_ref.shape[0]

    @pl.when(pl.program_id(0) == 0)
    def _():
        cnt_scr[...] = jnp.zeros(cnt_scr.shape, _F32)

    x = x_ref[...]
    gates = _dot(x.astype(_MXU), wgt_ref[...])
    wbr = wbr_ref
    merged = (_sigmoid(gates[:, :D_MODEL]) * _dot(ya_ref[...], wbr[0:SC_WIDTH, :])
              + _sigmoid(gates[:, D_MODEL:2 * D_MODEL])
              * _dot(yb_ref[...], wbr[SC_WIDTH:SC_WIDTH + GLA_VW, :])
              + _sigmoid(gates[:, 2 * D_MODEL:]) * _dot(yc_ref[...], wbr[SC_WIDTH + GLA_VW:, :]))
    x1 = _layer_norm(DN_ALPHA * x + _dot(merged.astype(_MXU), wo_ref[...]), g_ref[...], b_ref[...])
    x_o[...] = x1
    for c in range(TOKEN_ROWS):
        xt_o[pl.ds(c, tm, stride=TOKEN_ROWS), :] = x1[:, c * LANES:(c + 1) * LANES]

    logits = _dot(x1.astype(_MXU), wr_ref[...]) + br_ref[...]
    lane = lax.broadcasted_iota(jnp.int32, logits.shape, 1)
    neg = -jnp.inf
    gl = jnp.where(lane < N_GROUPS, logits, neg)
    g_max = jnp.max(gl, axis=1, keepdims=True)
    g_top = jnp.min(jnp.where(gl == g_max, lane, LANES), axis=1, keepdims=True)
    p_g = 1.0 / jnp.sum(jnp.where(lane < N_GROUPS, jnp.exp(logits - g_max), 0.0), axis=1, keepdims=True)
    lo = N_GROUPS + g_top * EXPERTS_PER_GROUP
    sl = jnp.where((lane >= lo) & (lane < lo + EXPERTS_PER_GROUP), logits, neg)
    v0 = jnp.max(sl, axis=1, keepdims=True)
    i0 = jnp.min(jnp.where(sl == v0, lane, LANES), axis=1, keepdims=True)
    sl = jnp.where(lane == i0, neg, sl)
    v1 = jnp.max(sl, axis=1, keepdims=True)
    i1 = jnp.min(jnp.where(sl == v1, lane, LANES), axis=1, keepdims=True)
    e1 = jnp.exp(v1 - v0)
    w0 = p_g / (1.0 + e1)
    w1 = p_g * e1 / (1.0 + e1)
    oh0 = lane == i0
    oh1 = lane == i1
    ohs = jnp.where(oh0, 1.0, jnp.where(oh1, 1.0, 0.0))
    before = _dot(tril_ref[...], ohs.astype(_MXU)) + cnt_scr[0:1, :]
    rank0 = jnp.sum(jnp.where(oh0, before, 0.0), axis=1, keepdims=True)
    rank1 = jnp.sum(jnp.where(oh1, before, 0.0), axis=1, keepdims=True)
    cnt_scr[...] = cnt_scr[...] + jnp.sum(ohs, axis=0, keepdims=True)
    cnt_o[...] = cnt_scr[...]
    route_o[...] = jnp.where(
        lane == 0, (i0 - N_GROUPS).astype(_F32),
        jnp.where(lane == 1, (i1 - N_GROUPS).astype(_F32),
                  jnp.where(lane == 2, w0,
                            jnp.where(lane == 3, w1,
                                      jnp.where(lane == 4, rank0, jnp.where(lane == 5, rank1, 0.0))))))


def _merge(lid, x, y_a, y_b, y_c, w_gt, w_br, w_o, ln_g, ln_b, w_route, b_route):
    S = x.shape[0]
    tm = ROW_TILE
    tril = jnp.tril(jnp.ones((tm, tm), _F32), -1).astype(_MXU)
    br_w = SC_WIDTH + GLA_VW + MLA_VW
    return pl.pallas_call(
        _merge_kernel,
        name="merge_route",
        grid_spec=pltpu.PrefetchScalarGridSpec(
            num_scalar_prefetch=1,
            grid=(S // tm,),
            in_specs=[pl.BlockSpec((tm, D_MODEL), lambda i, l: (i, 0)),
                      pl.BlockSpec((tm, SC_WIDTH), lambda i, l: (i, 0)),
                      pl.BlockSpec((tm, GLA_VW), lambda i, l: (i, 0)),
                      pl.BlockSpec((tm, MLA_VW), lambda i, l: (i, 0)),
                      pl.BlockSpec((None, D_MODEL, 3 * D_MODEL), lambda i, l: (l[0], 0, 0)),
                      pl.BlockSpec((None, br_w, D_MODEL), lambda i, l: (l[0], 0, 0)),
                      pl.BlockSpec((None, D_MODEL, D_MODEL), lambda i, l: (l[0], 0, 0)),
                      pl.BlockSpec((None, 1, D_MODEL), lambda i, l: (l[0], 0, 0)),
                      pl.BlockSpec((None, 1, D_MODEL), lambda i, l: (l[0], 0, 0)),
                      pl.BlockSpec((None, D_MODEL, LANES), lambda i, l: (l[0], 0, 0)),
                      pl.BlockSpec((None, 1, LANES), lambda i, l: (l[0], 0, 0)),
                      pl.BlockSpec((tm, tm), lambda i, l: (0, 0))],
            out_specs=[pl.BlockSpec((tm, D_MODEL), lambda i, l: (i, 0)),
                       pl.BlockSpec((tm * TOKEN_ROWS, LANES), lambda i, l: (i, 0)),
                       pl.BlockSpec((tm, LANES), lambda i, l: (i, 0)),
                       pl.BlockSpec((8, LANES), lambda i, l: (0, 0))],
            scratch_shapes=[pltpu.VMEM((8, LANES), _F32)]),
        out_shape=[jax.ShapeDtypeStruct((S, D_MODEL), _F32),
                   jax.ShapeDtypeStruct((S * TOKEN_ROWS, LANES), _F32),
                   jax.ShapeDtypeStruct((S, LANES), _F32),
                   jax.ShapeDtypeStruct((8, LANES), _F32)],
        compiler_params=_params(),
    )(lid, x, y_a, y_b, y_c, w_gt, w_br, w_o, ln_g, ln_b, w_route, b_route, tril)


def _dispatch_plan(route, counts_row, n_tokens):
    counts = counts_row[0, N_GROUPS:N_GROUPS + N_EXPERTS].astype(jnp.int32)
    padded = (counts + MOE_BM - 1) // MOE_BM * MOE_BM
    pad_end = jnp.cumsum(padded)
    pad_start = pad_end - padded
    expert_id = route[:, :TOP_K].astype(jnp.int32)
    rank = route[:, 4:4 + TOP_K].astype(jnp.int32)
    start_of = jnp.sum(jnp.where(expert_id[..., None] == jnp.arange(N_EXPERTS, dtype=jnp.int32),
                                 pad_start.astype(jnp.int32), 0), axis=-1)
    dest = (start_of + rank).reshape(n_tokens * TOP_K)
    nb = n_tokens * TOP_K // MOE_BM + N_EXPERTS
    block_expert = jnp.minimum(
        jnp.searchsorted(pad_end, jnp.arange(nb, dtype=jnp.int32) * MOE_BM, side='right'),
        N_EXPERTS - 1).astype(jnp.int32)
    n_active = (pad_end[-1] // MOE_BM).astype(jnp.int32)
    first = jnp.concatenate([jnp.ones((1,), jnp.int32),
                             (block_expert[1:] != block_expert[:-1]).astype(jnp.int32)])
    meta = jnp.concatenate([n_active.reshape(1), jnp.zeros((1,), jnp.int32)])
    ids = jnp.arange(N_EXPERTS, dtype=jnp.int32)
    present = counts > 0
    later = jnp.where(present[None, :] & (ids[None, :] > ids[:, None]), ids[None, :], N_EXPERTS)
    next_of = jnp.min(later, axis=1)
    next_of = jnp.where(next_of == N_EXPERTS, -1, next_of).astype(jnp.int32)
    parity_of = ((jnp.cumsum(present.astype(jnp.int32)) - 1) % 2).astype(jnp.int32)
    onehot_be = (block_expert[:, None] == ids[None, :])
    next_expert = jnp.sum(jnp.where(onehot_be, next_of[None, :], 0), axis=1).astype(jnp.int32)
    slot_of = jnp.sum(jnp.where(onehot_be, parity_of[None, :], 0), axis=1).astype(jnp.int32)
    return dest, counts, pad_start.astype(jnp.int32), block_expert, first, next_expert, slot_of, meta


def _token_copy(src_hbm, src_row, dst_ref, dst_row, sem):
    return pltpu.make_async_copy(src_hbm.at[pl.ds(src_row * TOKEN_ROWS, TOKEN_ROWS), :],
                                 dst_ref.at[pl.ds(dst_row * TOKEN_ROWS, TOKEN_ROWS), :], sem)


def _dispatch_kernel(cnt, pstart, meta, dest_ref, x_ref, xs_hbm, zero_scr, sem):
    i = pl.program_id(0)
    tt = DISPATCH_TILE
    blk = MOE_BM * TOKEN_ROWS
    n_blocks = xs_hbm.shape[0] // blk

    def spare_blocks(fn):
        def one(b, c):
            fn(pltpu.make_async_copy(zero_scr, xs_hbm.at[pl.ds(b * blk, blk), :], sem.at[2]))
            return c

        lax.fori_loop(meta[0], n_blocks, one, 0)

    @pl.when(i == 0)
    def _():
        zero_scr[...] = jnp.zeros(zero_scr.shape, _F32)
        spare_blocks(lambda cp: cp.start())

    def start(j, c):
        for kk in range(TOP_K):
            _token_copy(x_ref, j, xs_hbm, dest_ref[0, 0, TOP_K * j + kk], sem.at[0]).start(priority=kk)
        return c

    lax.fori_loop(0, tt, start, 0, unroll=8)

    def pad_rows(fn):
        def per_expert(e, c):
            n = cnt[e]
            n_pad = (n + MOE_BM - 1) // MOE_BM * MOE_BM

            def one(r, c2):
                fn(_token_copy(x_ref, 0, xs_hbm, pstart[e] + r, sem.at[1]))
                return c2

            lax.fori_loop(n, n_pad, one, 0)
            return c

        lax.fori_loop(0, N_EXPERTS, per_expert, 0)

    @pl.when(i == 0)
    def _():
        pad_rows(lambda cp: cp.start())
        pad_rows(lambda cp: cp.wait())
        spare_blocks(lambda cp: cp.wait())

    def wait(j, c):
        for kk in range(TOP_K):
            _token_copy(x_ref, 0, xs_hbm, 0, sem.at[0]).wait()
        return c

    lax.fori_loop(0, tt, wait, 0, unroll=8)


def _dispatch(counts, pad_start, meta, dest, x_tok, n_rows):
    n_tokens = x_tok.shape[0] // TOKEN_ROWS
    tt = DISPATCH_TILE
    return pl.pallas_call(
        _dispatch_kernel,
        name="moe_dispatch",
        grid_spec=pltpu.PrefetchScalarGridSpec(
            num_scalar_prefetch=3,
            grid=(n_tokens // tt,),
            in_specs=[pl.BlockSpec((1, 1, TOP_K * tt), lambda i, c, s, m: (i, 0, 0),
                                   memory_space=pltpu.SMEM),
                      pl.BlockSpec((tt * TOKEN_ROWS, LANES), lambda i, c, s, m: (i, 0))],
            out_specs=pl.BlockSpec(memory_space=pl.ANY),
            scratch_shapes=[pltpu.VMEM((MOE_BM * TOKEN_ROWS, LANES), _F32),
                            pltpu.SemaphoreType.DMA((3,))]),
        out_shape=jax.ShapeDtypeStruct((n_rows * TOKEN_ROWS, LANES), _F32),
        compiler_params=_params(),
    )(counts, pad_start, meta, dest.reshape(n_tokens // tt, 1, TOP_K * tt), x_tok)


def _from_token_tiles(ref, n):
    return jnp.concatenate([ref[pl.ds(c, n, stride=TOKEN_ROWS), :] for c in range(TOKEN_ROWS)], axis=1)


def _expert_kernel(lid, bexp, first, nxt, slot_of, meta, xs_ref, wg_hbm, wu_hbm, wd_hbm, o_ref,
                   wg_f, wu_f, wd_f, wg_b, wu_b, wd_b, sem):
    b = pl.program_id(0)
    layer = lid[0]

    def weight_copies(e, s):
        return (pltpu.make_async_copy(wg_hbm.at[layer, e], wg_f.at[s], sem.at[s, 0]),
                pltpu.make_async_copy(wu_hbm.at[layer, e], wu_f.at[s], sem.at[s, 1]),
                pltpu.make_async_copy(wd_hbm.at[layer, e], wd_f.at[s], sem.at[s, 2]))

    @pl.when(b == 0)
    def _():
        for cp in weight_copies(bexp[0], 0):
            cp.start()

    @pl.when(b >= meta[0])
    def _():
        o_ref[...] = jnp.zeros(o_ref.shape, o_ref.dtype)

    @pl.when(b < meta[0])
    def _():
        @pl.when(first[b] == 1)
        def _():
            s = slot_of[b]
            for cp in weight_copies(bexp[b], s):
                cp.wait()
            wg_b[...] = wg_f[s].astype(_MXU)
            wu_b[...] = wu_f[s].astype(_MXU)
            wd_b[...] = wd_f[s].astype(_MXU)

            @pl.when(nxt[b] >= 0)
            def _():
                for cp in weight_copies(nxt[b], 1 - s):
                    cp.start()

        xb = _from_token_tiles(xs_ref, MOE_BM).astype(_MXU)
        hg = _dot(xb, wg_b[...])
        hu = _dot(xb, wu_b[...])
        hid = (hg * _sigmoid(hg)) * hu
        y = _dot(hid.astype(_MXU), wd_b[...])
        for c in range(TOKEN_ROWS):
            o_ref[pl.ds(c, MOE_BM, stride=TOKEN_ROWS), :] = y[:, c * LANES:(c + 1) * LANES]


def _experts(lid, block_expert, first, next_expert, slot_of, meta, xs_tok, w_gate, w_up, w_down):
    rows = MOE_BM * TOKEN_ROWS
    nb = xs_tok.shape[0] // rows
    live = lambda b, m: jnp.minimum(b, m[0] - 1)
    hbm = pl.BlockSpec(memory_space=pl.ANY)
    return pl.pallas_call(
        _expert_kernel,
        name="moe_experts",
        grid_spec=pltpu.PrefetchScalarGridSpec(
            num_scalar_prefetch=6,
            grid=(nb,),
            in_specs=[pl.BlockSpec((rows, LANES), lambda b, l, e, f, n, s, m: (live(b, m), 0)),
                      hbm, hbm, hbm],
            out_specs=pl.BlockSpec((rows, LANES), lambda b, l, e, f, n, s, m: (b, 0)),
            scratch_shapes=[pltpu.VMEM((2, D_MODEL, EXPERT_HIDDEN), _F32),
                            pltpu.VMEM((2, D_MODEL, EXPERT_HIDDEN), _F32),
                            pltpu.VMEM((2, EXPERT_HIDDEN, D_MODEL), _F32),
                            pltpu.VMEM((D_MODEL, EXPERT_HIDDEN), _MXU),
                            pltpu.VMEM((D_MODEL, EXPERT_HIDDEN), _MXU),
                            pltpu.VMEM((EXPERT_HIDDEN, D_MODEL), _MXU),
                            pltpu.SemaphoreType.DMA((2, 3))]),
        out_shape=jax.ShapeDtypeStruct(xs_tok.shape, _F32),
        compiler_params=_params(),
    )(lid, block_expert, first, next_expert, slot_of, meta, xs_tok, w_gate, w_up, w_down)


def _combine_kernel(lid, pos_ref, pos_next_ref, y_hbm, x_ref, route_ref, g2_ref, b2_ref, wpg_ref, bpg_ref,
                    p_ref, wpu_ref, g3_ref, b3_ref, o_ref, ybuf, sem):
    tc = x_ref.shape[0]
    i = pl.program_id(0)
    slot = lax.rem(i, 2)

    def gather(rows_ref, s, fn):
        def body(r, c):
            for kk in range(TOP_K):
                fn(_token_copy(y_hbm, rows_ref[0, 0, TOP_K * r + kk], ybuf.at[s, kk], r, sem.at[s]), kk)
            return c

        lax.fori_loop(0, tc, body, 0, unroll=8)

    start = lambda cp, kk: cp.start(priority=kk)

    @pl.when(i == 0)
    def _():
        gather(pos_ref, 0, start)

    @pl.when(i + 1 < pl.num_programs(0))
    def _():
        gather(pos_next_ref, 1 - slot, start)

    gather(pos_ref, slot, lambda cp, kk: cp.wait())
    route = route_ref[...]
    y = (route[:, 2:3] * _from_token_tiles(ybuf.at[slot, 0], tc)
         + route[:, 3:4] * _from_token_tiles(ybuf.at[slot, 1], tc))
    x2 = _layer_norm(DN_ALPHA * x_ref[...] + y, g2_ref[...], b2_ref[...])
    gate = _sigmoid(_dot(x2.astype(_MXU), wpg_ref[...]) + bpg_ref[...])
    up = _dot(p_ref[...].astype(_MXU), wpu_ref[...])
    o_ref[...] = _layer_norm(DN_ALPHA * x2 + gate * up, g3_ref[...], b3_ref[...])


def _combine(lid, dest, y_tok, x, route, ln2_g, ln2_b, w_pg, b_pg, p, w_pu, ln3_g, ln3_b):
    S = x.shape[0]
    tc = COMBINE_TILE
    n_tiles = S // tc
    pos = dest.reshape(n_tiles, 1, TOP_K * tc)
    vec = lambda n: pl.BlockSpec((None, 1, n), lambda i, l: (l[0], 0, 0))
    return pl.pallas_call(
        _combine_kernel,
        name="combine_ple",
        grid_spec=pltpu.PrefetchScalarGridSpec(
            num_scalar_prefetch=1,
            grid=(S // tc,),
            in_specs=[pl.BlockSpec((1, 1, TOP_K * tc), lambda i, l: (i, 0, 0), memory_space=pltpu.SMEM),
                      pl.BlockSpec((1, 1, TOP_K * tc), lambda i, l: (jnp.minimum(i + 1, n_tiles - 1), 0, 0),
                                   memory_space=pltpu.SMEM),
                      pl.BlockSpec(memory_space=pl.ANY),
                      pl.BlockSpec((tc, D_MODEL), lambda i, l: (i, 0)),
                      pl.BlockSpec((tc, LANES), lambda i, l: (i, 0)),
                      vec(D_MODEL), vec(D_MODEL),
                      pl.BlockSpec((None, D_MODEL, D_MODEL), lambda i, l: (l[0], 0, 0)),
                      vec(D_MODEL),
                      pl.BlockSpec((None, tc, PLE_DIM), lambda i, l: (l[0], i, 0)),
                      pl.BlockSpec((None, PLE_DIM, D_MODEL), lambda i, l: (l[0], 0, 0)),
                      vec(D_MODEL), vec(D_MODEL)],
            out_specs=pl.BlockSpec((tc, D_MODEL), lambda i, l: (i, 0)),
            scratch_shapes=[pltpu.VMEM((2, TOP_K, tc * TOKEN_ROWS, LANES), _F32),
                            pltpu.SemaphoreType.DMA((2,))]),
        out_shape=jax.ShapeDtypeStruct((S, D_MODEL), _F32),
        compiler_params=_params(),
    )(lid, pos, pos, y_tok, x, route, ln2_g, ln2_b, w_pg, b_pg, p, w_pu, ln3_g, ln3_b)


def _split_in_proj(w_in):
    sizes = (SC_WIDTH, SC_WIDTH, SC_WIDTH, GLA_QK, GLA_QK, GLA_VW, GLA_GATE_RANK, GLA_VW,
             MLA_Q_RANK, MLA_KV_RANK, MLA_ROPE, D_MODEL, D_MODEL, D_MODEL)
    offs = [0]
    for s in sizes:
        offs.append(offs[-1] + s)
    col = lambda j: w_in[:, :, offs[j]:offs[j + 1]]
    L = w_in.shape[0]
    zeros = lambda n: jnp.zeros((L, D_MODEL, n), w_in.dtype)
    w_conv_in = jnp.concatenate([col(0), col(1), col(2)], axis=-1)
    w_gla_in = jnp.concatenate([col(3), col(4), col(5), col(7), col(6), zeros(LANES - GLA_GATE_RANK)], axis=-1)
    w_mla_in = jnp.concatenate([col(8), col(9), col(10), zeros(LANES - MLA_ROPE)], axis=-1)
    w_gates = jnp.concatenate([col(11), col(12), col(13)], axis=-1)
    return [w.astype(_MXU) for w in (w_conv_in, w_gla_in, w_mla_in, w_gates)]


def kernel(x, p, positions, ln0_g, ln0_b, w_in, w_conv, w_gla_gate, b_gla_gate, gla_norm_g, mla_q_norm_g, mla_kv_norm_g, w_uq, w_ukv, w_br, w_o, ln1_g, ln1_b, w_grp, b_grp, w_exp, b_exp, w_gate, w_up, w_down, ln2_g, ln2_b, w_ple_gate, b_ple_gate, w_ple_up, ln3_g, ln3_b):
    B, S, D = x.shape
    L = w_in.shape[0]
    assert B == 1 and D == D_MODEL and S % ROW_TILE == 0 and S % ATTN_TQ == 0
    H = MLA_HEADS

    w_conv_in, w_gla_in, w_mla_in, w_gates = _split_in_proj(w_in)
    w_gg = jnp.pad(w_gla_gate, ((0, 0), (0, LANES - GLA_GATE_RANK), (0, 0))).astype(_MXU)
    w_uq_p = jnp.pad(w_uq.reshape(L, MLA_Q_RANK, H, MLA_QK),
                     ((0, 0), (0, 0), (0, 0), (0, MLA_QK_PAD - MLA_QK))
                     ).reshape(L, MLA_Q_RANK, H * MLA_QK_PAD).astype(_MXU)
    w_ukv4 = w_ukv.reshape(L, MLA_KV_RANK, H, MLA_NOPE + MLA_V)
    w_uk = w_ukv4[..., :MLA_NOPE].reshape(L, MLA_KV_RANK, H * MLA_NOPE).astype(_MXU)
    w_uv = w_ukv4[..., MLA_NOPE:].reshape(L, MLA_KV_RANK, H * MLA_V).astype(_MXU)
    w_route = jnp.concatenate(
        [w_grp, w_exp.reshape(L, D, N_EXPERTS),
         jnp.zeros((L, D, LANES - N_GROUPS - N_EXPERTS), _F32)], axis=-1).astype(_MXU)
    b_route = jnp.concatenate(
        [b_grp, b_exp.reshape(L, N_EXPERTS),
         jnp.zeros((L, LANES - N_GROUPS - N_EXPERTS), _F32)], axis=-1).reshape(L, 1, LANES)
    w_br_c = w_br.astype(_MXU)
    w_o_c = w_o.astype(_MXU)
    w_pg_c = w_ple_gate.astype(_MXU)
    w_pu_c = w_ple_up.astype(_MXU)
    row = lambda a: a.reshape(L, 1, a.shape[-1])

    cos_t, sin_t = _rope_tables(positions.reshape(S))
    x0 = _entry_norm(x.reshape(S, D), ln0_g, ln0_b)
    p3 = p.reshape(L, S, PLE_DIM)

    def layer(xc, i):
        lid = jnp.reshape(i, (1,)).astype(jnp.int32)
        y_a = _conv_branch(lid, xc, w_conv_in, w_conv)
        y_b = _gla_branch(lid, xc, w_gla_in, w_gg, row(b_gla_gate), row(gla_norm_g))
        q, k, v = _mla_prep(lid, xc, w_mla_in, row(mla_q_norm_g), row(mla_kv_norm_g),
                            w_uq_p, w_uk, w_uv, cos_t, sin_t)
        y_c = _attention(q, k, v)
        x1, x1_tok, route, counts_row = _merge(lid, xc, y_a, y_b, y_c, w_gates, w_br_c, w_o_c,
                                               row(ln1_g), row(ln1_b), w_route, b_route)
        (dest, counts, pad_start, block_expert, first, next_expert, slot_of,
         meta) = _dispatch_plan(route, counts_row, S)
        n_rows = (S * TOP_K // MOE_BM + N_EXPERTS) * MOE_BM
        xs_tok = _dispatch(counts, pad_start, meta, dest, x1_tok, n_rows)
        y_tok = _experts(lid, block_expert, first, next_expert, slot_of, meta, xs_tok, w_gate, w_up, w_down)
        x3 = _combine(lid, dest, y_tok, x1, route, row(ln2_g), row(ln2_b), w_pg_c, row(b_ple_gate),
                      p3, w_pu_c, row(ln3_g), row(ln3_b))
        return x3, None

    out, _ = lax.scan(layer, x0, jnp.arange(L, dtype=jnp.int32))
    return out.reshape(B, S, D)
```

```python
import functools

import jax
import jax.numpy as jnp
from jax import lax
from jax.experimental import pallas as pl
from jax.experimental.pallas import tpu as pltpu

D_MODEL = 1024
DEPTH = 4
PLE_DIM = 256
SC_WIDTH = 512
SC_CONV = 3
GLA_HEADS = 4
GLA_DK = 64
GLA_DV = 128
GLA_GATE_RANK = 16
GLA_TAU = 16.0
GLA_CHUNK = 64
MLA_HEADS = 4
MLA_NOPE = 128
MLA_ROPE = 64
MLA_V = 128
MLA_Q_RANK = 256
MLA_KV_RANK = 128
ROPE_THETA = 10000.0
N_GROUPS = 8
EXPERTS_PER_GROUP = 8
N_EXPERTS = N_GROUPS * EXPERTS_PER_GROUP
TOP_K = 2
EXPERT_HIDDEN = 256
DN_ALPHA = (2 * DEPTH) ** 0.25
LN_EPS = 1e-5
RMS_EPS = 1e-6

GLA_QK = GLA_HEADS * GLA_DK
GLA_VW = GLA_HEADS * GLA_DV
MLA_QK = MLA_NOPE + MLA_ROPE
MLA_VW = MLA_HEADS * MLA_V

LANES = 128
MLA_QK_PAD = 2 * LANES
VMEM_LIMIT = 56 * 1024 * 1024

ROW_TILE = 512
ATTN_TQ = 1024
ATTN_TK = 512
ATTN_ROWS = 256
LOG2_E = 1.4426950408889634
MOE_BM = 256
COMBINE_TILE = 256
DISPATCH_TILE = 1024
TOKEN_ROWS = D_MODEL // LANES

_MXU = jnp.bfloat16
_F32 = jnp.float32


def _dot(a, b):
    return jnp.dot(a, b, preferred_element_type=_F32)


def _dot_nt(a, b):
    return lax.dot_general(a, b, (((1,), (1,)), ((), ())), preferred_element_type=_F32)


def _dot_tn(a, b):
    return lax.dot_general(a, b, (((0,), (0,)), ((), ())), preferred_element_type=_F32)


def _layer_norm(x, g, b):
    mu = jnp.mean(x, axis=-1, keepdims=True)
    xc = x - mu
    var = jnp.mean(xc * xc, axis=-1, keepdims=True)
    return xc * lax.rsqrt(var + LN_EPS) * g + b


def _rms_norm(x, g):
    return x * lax.rsqrt(jnp.mean(x * x, axis=-1, keepdims=True) + RMS_EPS) * g


def _sigmoid(x):
    return 1.0 / (1.0 + jnp.exp(-x))


def _params(n_axes=1):
    return pltpu.CompilerParams(dimension_semantics=("arbitrary",) * n_axes,
                                vmem_limit_bytes=VMEM_LIMIT)


def _ln0_kernel(x_ref, g_ref, b_ref, o_ref):
    o_ref[...] = _layer_norm(x_ref[...], g_ref[...], b_ref[...])


def _entry_norm(x, g, b):
    S = x.shape[0]
    return pl.pallas_call(
        _ln0_kernel,
        name="entry_ln",
        grid=(S // ROW_TILE,),
        in_specs=[pl.BlockSpec((ROW_TILE, D_MODEL), lambda i: (i, 0)),
                  pl.BlockSpec((1, D_MODEL), lambda i: (0, 0)),
                  pl.BlockSpec((1, D_MODEL), lambda i: (0, 0))],
        out_specs=pl.BlockSpec((ROW_TILE, D_MODEL), lambda i: (i, 0)),
        out_shape=jax.ShapeDtypeStruct((S, D_MODEL), _F32),
        compiler_params=_params(),
    )(x, g.reshape(1, D_MODEL), b.reshape(1, D_MODEL))


def _rope_kernel(pos_ref, inv_ref, c_ref, s_ref):
    ang = pos_ref[...].astype(_F32) * inv_ref[...]
    lane = lax.broadcasted_iota(jnp.int32, ang.shape, 1)
    half = MLA_ROPE // 2
    cosv = jnp.cos(ang)
    sinv = jnp.sin(ang)
    c_ref[...] = jnp.where(lane < MLA_ROPE, cosv, 0.0)
    s_ref[...] = jnp.where(lane < half, -sinv, jnp.where(lane < MLA_ROPE, sinv, 0.0))


def _rope_tables(positions):
    S = positions.shape[0]
    half = MLA_ROPE // 2
    inv = 1.0 / (ROPE_THETA ** (jnp.arange(0, MLA_ROPE, 2, dtype=_F32) / MLA_ROPE))
    inv = jnp.tile(inv, LANES // half).reshape(1, LANES)
    return pl.pallas_call(
        _rope_kernel,
        name="rope_tables",
        grid=(S // ROW_TILE,),
        in_specs=[pl.BlockSpec((ROW_TILE, 1), lambda i: (i, 0)),
                  pl.BlockSpec((1, LANES), lambda i: (0, 0))],
        out_specs=[pl.BlockSpec((ROW_TILE, LANES), lambda i: (i, 0)),
                   pl.BlockSpec((ROW_TILE, LANES), lambda i: (i, 0))],
        out_shape=[jax.ShapeDtypeStruct((S, LANES), _F32)] * 2,
        compiler_params=_params(),
    )(positions.reshape(S, 1), inv)


def _conv_compute(xb, w_ref, wc_ref, o_ref, u_scr):
    tm = xb.shape[0]
    h = _dot(xb, w_ref[...])
    a_b = h[:, :SC_WIDTH]
    u = h[:, SC_WIDTH:2 * SC_WIDTH] * h[:, 2 * SC_WIDTH:]
    u_scr[8:8 + tm, :] = u
    wc = wc_ref[...]
    y = wc[2:3, :] * u + wc[1:2, :] * u_scr[7:7 + tm, :] + wc[0:1, :] * u_scr[6:6 + tm, :]
    o_ref[...] = (a_b * y).astype(o_ref.dtype)
    u_scr[0:8, :] = u_scr[tm:tm + 8, :]


_GLA_IN = 2 * GLA_QK + 2 * GLA_VW + LANES


def _gla_compute(xb, w_ref, wg_ref, bg_ref, ng_ref, o_ref, st_scr):
    tm = xb.shape[0]
    C = GLA_CHUNK
    h = _dot(xb, w_ref[...])
    q = h[:, :GLA_QK]
    k = h[:, GLA_QK:2 * GLA_QK]
    v = h[:, 2 * GLA_QK:2 * GLA_QK + GLA_VW]
    r = h[:, 2 * GLA_QK + GLA_VW:2 * GLA_QK + 2 * GLA_VW]
    g_lr = h[:, 2 * GLA_QK + 2 * GLA_VW:]
    z = _dot(g_lr.astype(_MXU), wg_ref[...]) + bg_ref[...]
    log_a = (jnp.minimum(z, 0.0) - jnp.log1p(jnp.exp(-jnp.abs(z)))) * (1.0 / GLA_TAU)

    row_in_chunk = lax.broadcasted_iota(jnp.int32, log_a.shape, 0) & (C - 1)
    b = log_a
    d = 1
    while d < C:
        b = b + jnp.where(row_in_chunk >= d, pltpu.roll(b, d, 0), 0.0)
        d *= 2

    ri = lax.broadcasted_iota(jnp.int32, (C, C), 0)
    ci = lax.broadcasted_iota(jnp.int32, (C, C), 1)
    causal = ci <= ri
    scale = GLA_DK ** -0.5
    ng = ng_ref[...]
    for c in range(tm // C):
        rows = slice(c * C, (c + 1) * C)
        bc = b[rows]
        b_last = bc[C - 1:C, :]
        kc = k[rows]
        q_e = (q[rows] * scale) * jnp.exp(bc)
        k_e = kc * jnp.exp(-bc)
        k_t = kc * jnp.exp(b_last - bc)
        decay = jnp.exp(b_last)
        for hh in range(GLA_HEADS):
            ks = slice(hh * GLA_DK, (hh + 1) * GLA_DK)
            vs = slice(hh * GLA_DV, (hh + 1) * GLA_DV)
            qh = q_e[:, ks].astype(_MXU)
            vh = v[rows, vs].astype(_MXU)
            a = jnp.where(causal, _dot_nt(qh, k_e[:, ks].astype(_MXU)), 0.0)
            st = st_scr[hh]
            o = _dot(a.astype(_MXU), vh) + _dot_nt(qh, st.astype(_MXU))
            st_scr[hh] = st * decay[:, ks] + _dot_tn(vh, k_t[:, ks].astype(_MXU))
            rr = r[rows, vs]
            o_ref[rows, vs] = (_rms_norm(o, ng) * (rr * _sigmoid(rr))).astype(o_ref.dtype)


_MLA_IN = MLA_Q_RANK + MLA_KV_RANK + LANES


def _mla_compute(xb, w_ref, qg_ref, kvg_ref, wuq_ref, wk_ref, wv_ref, ct_ref, st_ref, q_o, k_o, v_o):
    h = _dot(xb, w_ref[...])
    c_q = _rms_norm(h[:, :MLA_Q_RANK], qg_ref[...]).astype(_MXU)
    c_kv = _rms_norm(h[:, MLA_Q_RANK:MLA_Q_RANK + MLA_KV_RANK], kvg_ref[...]).astype(_MXU)
    k_rope_raw = h[:, MLA_Q_RANK + MLA_KV_RANK:]
    q = _dot(c_q, wuq_ref[...])
    k_nope = _dot(c_kv, wk_ref[...])
    v = _dot(c_kv, wv_ref[...])
    cos_t = ct_ref[...]
    sin_t = st_ref[...]
    lane = lax.broadcasted_iota(jnp.int32, cos_t.shape, 1)
    half = MLA_ROPE // 2

    def rope(xr):
        rot = jnp.where(lane < half, pltpu.roll(xr, LANES - half, 1), pltpu.roll(xr, half, 1))
        return xr * cos_t + rot * sin_t

    k_rope = rope(k_rope_raw)
    one_col = jnp.where(lane == 0, 1.0, 0.0)
    scale = MLA_QK ** -0.5 * LOG2_E
    for hh in range(MLA_HEADS):
        base = hh * MLA_QK_PAD
        q_h = jnp.concatenate([q[:, base:base + MLA_NOPE], rope(q[:, base + MLA_NOPE:base + MLA_QK_PAD])],
                              axis=1)
        q_o[hh] = (q_h * scale).astype(q_o.dtype)
        k_o[hh] = jnp.concatenate([k_nope[:, hh * MLA_NOPE:(hh + 1) * MLA_NOPE], k_rope],
                                  axis=1).astype(k_o.dtype)
        v_o[hh] = jnp.concatenate([v[:, hh * MLA_V:(hh + 1) * MLA_V], one_col], axis=1).astype(v_o.dtype)


def _mixers_kernel(lid, x_ref,
                   wc_in, wc, wg_in, wg_gate, bg_gate, ng, wm_in, qg, kvg, wuq, wuk, wuv, cos_t, sin_t,
                   ya_o, yb_o, q_o, k_o, v_o, u_scr, st_scr):
    @pl.when(pl.program_id(0) == 0)
    def _():
        u_scr[0:8, :] = jnp.zeros((8, SC_WIDTH), _F32)
        st_scr[...] = jnp.zeros(st_scr.shape, _F32)

    xb = x_ref[...].astype(_MXU)
    _conv_compute(xb, wc_in, wc, ya_o, u_scr)
    _mla_compute(xb, wm_in, qg, kvg, wuq, wuk, wuv, cos_t, sin_t, q_o, k_o, v_o)
    _gla_compute(xb, wg_in, wg_gate, bg_gate, ng, yb_o, st_scr)


def _mixers(lid, x, w_conv_in, w_conv, w_gla_in, w_gate, b_gate, norm_g,
            w_mla_in, q_norm_g, kv_norm_g, w_uq, w_uk, w_uv, cos_t, sin_t):
    S = x.shape[0]
    tm = ROW_TILE
    H = MLA_HEADS
    return pl.pallas_call(
        _mixers_kernel,
        name="mixers",
        grid_spec=pltpu.PrefetchScalarGridSpec(
            num_scalar_prefetch=1,
            grid=(S // tm,),
            in_specs=[pl.BlockSpec((tm, D_MODEL), lambda i, l: (i, 0)),
                      pl.BlockSpec((None, D_MODEL, 3 * SC_WIDTH), lambda i, l: (l[0], 0, 0)),
                      pl.BlockSpec((None, SC_CONV, SC_WIDTH), lambda i, l: (l[0], 0, 0)),
                      pl.BlockSpec((None, D_MODEL, _GLA_IN), lambda i, l: (l[0], 0, 0)),
                      pl.BlockSpec((None, LANES, GLA_QK), lambda i, l: (l[0], 0, 0)),
                      pl.BlockSpec((None, 1, GLA_QK), lambda i, l: (l[0], 0, 0)),
                      pl.BlockSpec((None, 1, GLA_DV), lambda i, l: (l[0], 0, 0)),
                      pl.BlockSpec((None, D_MODEL, _MLA_IN), lambda i, l: (l[0], 0, 0)),
                      pl.BlockSpec((None, 1, MLA_Q_RANK), lambda i, l: (l[0], 0, 0)),
                      pl.BlockSpec((None, 1, MLA_KV_RANK), lambda i, l: (l[0], 0, 0)),
                      pl.BlockSpec((None, MLA_Q_RANK, H * MLA_QK_PAD), lambda i, l: (l[0], 0, 0)),
                      pl.BlockSpec((None, MLA_KV_RANK, H * MLA_NOPE), lambda i, l: (l[0], 0, 0)),
                      pl.BlockSpec((None, MLA_KV_RANK, H * MLA_V), lambda i, l: (l[0], 0, 0)),
                      pl.BlockSpec((tm, LANES), lambda i, l: (i, 0)),
                      pl.BlockSpec((tm, LANES), lambda i, l: (i, 0))],
            out_specs=[pl.BlockSpec((tm, SC_WIDTH), lambda i, l: (i, 0)),
                       pl.BlockSpec((tm, GLA_VW), lambda i, l: (i, 0)),
                       pl.BlockSpec((H, tm, MLA_QK_PAD), lambda i, l: (0, i, 0)),
                       pl.BlockSpec((H, tm, MLA_QK_PAD), lambda i, l: (0, i, 0)),
                       pl.BlockSpec((H, tm, MLA_V + LANES), lambda i, l: (0, i, 0))],
            scratch_shapes=[pltpu.VMEM((tm + 8, SC_WIDTH), _F32),
                            pltpu.VMEM((GLA_HEADS, GLA_DV, GLA_DK), _F32)]),
        out_shape=[jax.ShapeDtypeStruct((S, SC_WIDTH), _MXU),
                   jax.ShapeDtypeStruct((S, GLA_VW), _MXU),
                   jax.ShapeDtypeStruct((H, S, MLA_QK_PAD), _MXU),
                   jax.ShapeDtypeStruct((H, S, MLA_QK_PAD), _MXU),
                   jax.ShapeDtypeStruct((H, S, MLA_V + LANES), _MXU)],
        compiler_params=_params(),
    )(lid, x, w_conv_in, w_conv, w_gla_in, w_gate, b_gate, norm_g,
      w_mla_in, q_norm_g, kv_norm_g, w_uq, w_uk, w_uv, cos_t, sin_t)


def _attn_kernel(q_ref, k_ref, v_ref, o_ref, s_a, s_b, m_scr, acc_scr):
    tq = q_ref.shape[0]
    tk = ATTN_TK
    rb = ATTN_ROWS
    n_rb = tq // rb
    qi = pl.program_id(1)
    m_scr[...] = jnp.full(m_scr.shape, -jnp.inf, _F32)
    acc_scr[...] = jnp.zeros(acc_scr.shape, _F32)

    def scores(dst, start, first_rb=0):
        rows = slice(first_rb * rb, tq)
        dst[rows, :] = _dot_nt(q_ref[rows, :], k_ref[pl.ds(start, tk), :])

    def consume(src, start, col0=None):
        for r in range(n_rb):
            row0 = r * rb
            rows = slice(row0, row0 + rb)
            n_cols = tk
            if col0 is not None:
                n_cols = min(tk, row0 + rb - col0)
                if n_cols <= 0:
                    continue
            s = src[rows, :n_cols]
            if col0 is not None and col0 + n_cols - 1 > row0:
                ri = lax.broadcasted_iota(jnp.int32, s.shape, 0) + row0
                ci = lax.broadcasted_iota(jnp.int32, s.shape, 1) + col0
                s = jnp.where(ci <= ri, s, -jnp.inf)
            v = v_ref[pl.ds(start, n_cols), :]
            m_prev = m_scr[rows, :]
            m_new = jnp.maximum(m_prev, jnp.max(s, axis=1, keepdims=True))
            alpha = jnp.exp2(m_prev - m_new)
            p = jnp.exp2(s - m_new[:, :1]).astype(v.dtype)
            acc_scr[rows, :] = jnp.concatenate([alpha, alpha], axis=1) * acc_scr[rows, :] + _dot(p, v)
            m_scr[rows, :] = m_new

    col = lambda j: pl.multiple_of(j * tk, tk)
    n_diag = tq // tk
    scores(s_a, col(0))

    def body(i, carry):
        scores(s_b, col(2 * i + 1))
        consume(s_a, col(2 * i))
        scores(s_a, col(2 * i + 2))
        consume(s_b, col(2 * i + 1))
        return carry

    lax.fori_loop(0, qi * (n_diag // 2), body, 0)
    bufs = (s_a, s_b)
    for d in range(n_diag):
        if d + 1 < n_diag:
            scores(bufs[(d + 1) % 2], col(n_diag * qi + d + 1), first_rb=(d + 1) * tk // rb)
        consume(bufs[d % 2], col(n_diag * qi + d), col0=d * tk)
    o_ref[...] = (acc_scr[:, :MLA_V] / acc_scr[:, MLA_V:MLA_V + 1]).astype(o_ref.dtype)


def _attention(q, k, v):
    H, S, _ = q.shape
    tq = ATTN_TQ
    return pl.pallas_call(
        _attn_kernel,
        name="mla_attention",
        grid=(H, S // tq),
        in_specs=[pl.BlockSpec((None, tq, MLA_QK_PAD), lambda h, i: (h, i, 0)),
                  pl.BlockSpec((None, S, MLA_QK_PAD), lambda h, i: (h, 0, 0)),
                  pl.BlockSpec((None, S, MLA_V + LANES), lambda h, i: (h, 0, 0))],
        out_specs=pl.BlockSpec((tq, MLA_V), lambda h, i: (i, h)),
        out_shape=jax.ShapeDtypeStruct((S, MLA_VW), _MXU),
        scratch_shapes=[pltpu.VMEM((tq, ATTN_TK), _F32),
                        pltpu.VMEM((tq, ATTN_TK), _F32),
                        pltpu.VMEM((tq, LANES), _F32),
                        pltpu.VMEM((tq, MLA_V + LANES), _F32)],
        compiler_params=_params(2),
    )(q, k, v)


def _merge_kernel(lid, x_ref, ya_ref, yb_ref, yc_ref, wgt_ref, wbr_ref, wo_ref, g_ref, b_ref,
                  wr_ref, br_ref, tril_ref, x_o, xt_o, route_o, cnt_o, cnt_scr):
    tm = x_ref.shape[0]

    @pl.when(pl.program_id(0) == 0)
    def _():
        cnt_scr[...] = jnp.zeros(cnt_scr.shape, _F32)

    x = x_ref[...]
    gates = _dot(x.astype(_MXU), wgt_ref[...])
    wbr = wbr_ref
    merged = (_sigmoid(gates[:, :D_MODEL]) * _dot(ya_ref[...], wbr[0:SC_WIDTH, :])
              + _sigmoid(gates[:, D_MODEL:2 * D_MODEL])
              * _dot(yb_ref[...], wbr[SC_WIDTH:SC_WIDTH + GLA_VW, :])
              + _sigmoid(gates[:, 2 * D_MODEL:]) * _dot(yc_ref[...], wbr[SC_WIDTH + GLA_VW:, :]))
    x1 = _layer_norm(DN_ALPHA * x + _dot(merged.astype(_MXU), wo_ref[...]), g_ref[...], b_ref[...])
    x_o[...] = x1
    for c in range(TOKEN_ROWS):
        xt_o[pl.ds(c, tm, stride=TOKEN_ROWS), :] = x1[:, c * LANES:(c + 1) * LANES]

    logits = _dot(x1.astype(_MXU), wr_ref[...]) + br_ref[...]
    lane = lax.broadcasted_iota(jnp.int32, logits.shape, 1)
    neg = -jnp.inf
    gl = jnp.where(lane < N_GROUPS, logits, neg)
    g_max = jnp.max(gl, axis=1, keepdims=True)
    g_top = jnp.min(jnp.where(gl == g_max, lane, LANES), axis=1, keepdims=True)
    p_g = 1.0 / jnp.sum(jnp.where(lane < N_GROUPS, jnp.exp(logits - g_max), 0.0), axis=1, keepdims=True)
    lo = N_GROUPS + g_top * EXPERTS_PER_GROUP
    sl = jnp.where((lane >= lo) & (lane < lo + EXPERTS_PER_GROUP), logits, neg)
    v0 = jnp.max(sl, axis=1, keepdims=True)
    i0 = jnp.min(jnp.where(sl == v0, lane, LANES), axis=1, keepdims=True)
    sl = jnp.where(lane == i0, neg, sl)
    v1 = jnp.max(sl, axis=1, keepdims=True)
    i1 = jnp.min(jnp.where(sl == v1, lane, LANES), axis=1, keepdims=True)
    e1 = jnp.exp(v1 - v0)
    w0 = p_g / (1.0 + e1)
    w1 = p_g * e1 / (1.0 + e1)
    oh0 = lane == i0
    oh1 = lane == i1
    ohs = jnp.where(oh0, 1.0, jnp.where(oh1, 1.0, 0.0))
    before = _dot(tril_ref[...], ohs.astype(_MXU)) + cnt_scr[0:1, :]
    rank0 = jnp.sum(jnp.where(oh0, before, 0.0), axis=1, keepdims=True)
    rank1 = jnp.sum(jnp.where(oh1, before, 0.0), axis=1, keepdims=True)
    cnt_scr[...] = cnt_scr[...] + jnp.sum(ohs, axis=0, keepdims=True)
    cnt_o[...] = cnt_scr[...]
    route_o[...] = jnp.where(
        lane == 0, (i0 - N_GROUPS).astype(_F32),
        jnp.where(lane == 1, (i1 - N_GROUPS).astype(_F32),
                  jnp.where(lane == 2, w0,
                            jnp.where(lane == 3, w1,
                                      jnp.where(lane == 4, rank0, jnp.where(lane == 5, rank1, 0.0))))))


def _merge(lid, x, y_a, y_b, y_c, w_gt, w_br, w_o, ln_g, ln_b, w_route, b_route):
    S = x.shape[0]
    tm = ROW_TILE
    tril = jnp.tril(jnp.ones((tm, tm), _F32), -1).astype(_MXU)
    br_w = SC_WIDTH + GLA_VW + MLA_VW
    return pl.pallas_call(
        _merge_kernel,
        name="merge_route",
        grid_spec=pltpu.PrefetchScalarGridSpec(
            num_scalar_prefetch=1,
            grid=(S // tm,),
            in_specs=[pl.BlockSpec((tm, D_MODEL), lambda i, l: (i, 0)),
                      pl.BlockSpec((tm, SC_WIDTH), lambda i, l: (i, 0)),
                      pl.BlockSpec((tm, GLA_VW), lambda i, l: (i, 0)),
                      pl.BlockSpec((tm, MLA_VW), lambda i, l: (i, 0)),
                      pl.BlockSpec((None, D_MODEL, 3 * D_MODEL), lambda i, l: (l[0], 0, 0)),
                      pl.BlockSpec((None, br_w, D_MODEL), lambda i, l: (l[0], 0, 0)),
                      pl.BlockSpec((None, D_MODEL, D_MODEL), lambda i, l: (l[0], 0, 0)),
                      pl.BlockSpec((None, 1, D_MODEL), lambda i, l: (l[0], 0, 0)),
                      pl.BlockSpec((None, 1, D_MODEL), lambda i, l: (l[0], 0, 0)),
                      pl.BlockSpec((None, D_MODEL, LANES), lambda i, l: (l[0], 0, 0)),
                      pl.BlockSpec((None, 1, LANES), lambda i, l: (l[0], 0, 0)),
                      pl.BlockSpec((tm, tm), lambda i, l: (0, 0))],
            out_specs=[pl.BlockSpec((tm, D_MODEL), lambda i, l: (i, 0)),
                       pl.BlockSpec((tm * TOKEN_ROWS, LANES), lambda i, l: (i, 0)),
                       pl.BlockSpec((tm, LANES), lambda i, l: (i, 0)),
                       pl.BlockSpec((8, LANES), lambda i, l: (0, 0))],
            scratch_shapes=[pltpu.VMEM((8, LANES), _F32)]),
        out_shape=[jax.ShapeDtypeStruct((S, D_MODEL), _F32),
                   jax.ShapeDtypeStruct((S * TOKEN_ROWS, LANES), _F32),
                   jax.ShapeDtypeStruct((S, LANES), _F32),
                   jax.ShapeDtypeStruct((8, LANES), _F32)],
        compiler_params=_params(),
    )(lid, x, y_a, y_b, y_c, w_gt, w_br, w_o, ln_g, ln_b, w_route, b_route, tril)


def _dispatch_plan(route, counts_row, n_tokens):
    counts = counts_row[0, N_GROUPS:N_GROUPS + N_EXPERTS].astype(jnp.int32)
    padded = (counts + MOE_BM - 1) // MOE_BM * MOE_BM
    pad_end = jnp.cumsum(padded)
    pad_start = pad_end - padded
    expert_id = route[:, :TOP_K].astype(jnp.int32)
    rank = route[:, 4:4 + TOP_K].astype(jnp.int32)
    start_of = jnp.sum(jnp.where(expert_id[..., None] == jnp.arange(N_EXPERTS, dtype=jnp.int32),
                                 pad_start.astype(jnp.int32), 0), axis=-1)
    dest = (start_of + rank).reshape(n_tokens * TOP_K)
    nb = n_tokens * TOP_K // MOE_BM + N_EXPERTS
    block_row = jnp.arange(nb, dtype=jnp.int32) * MOE_BM
    block_expert = jnp.minimum(
        jnp.sum((pad_end[None, :] <= block_row[:, None]).astype(jnp.int32), axis=1),
        N_EXPERTS - 1).astype(jnp.int32)
    n_active = (pad_end[-1] // MOE_BM).astype(jnp.int32)
    first = jnp.concatenate([jnp.ones((1,), jnp.int32),
                             (block_expert[1:] != block_expert[:-1]).astype(jnp.int32)])
    meta = jnp.concatenate([n_active.reshape(1), jnp.zeros((1,), jnp.int32)])
    ids = jnp.arange(N_EXPERTS, dtype=jnp.int32)
    present = counts > 0
    later = jnp.where(present[None, :] & (ids[None, :] > ids[:, None]), ids[None, :], N_EXPERTS)
    next_of = jnp.min(later, axis=1)
    next_of = jnp.where(next_of == N_EXPERTS, -1, next_of).astype(jnp.int32)
    parity_of = ((jnp.cumsum(present.astype(jnp.int32)) - 1) % 2).astype(jnp.int32)
    onehot_be = (block_expert[:, None] == ids[None, :])
    next_expert = jnp.sum(jnp.where(onehot_be, next_of[None, :], 0), axis=1).astype(jnp.int32)
    slot_of = jnp.sum(jnp.where(onehot_be, parity_of[None, :], 0), axis=1).astype(jnp.int32)
    return dest, counts, pad_start.astype(jnp.int32), block_expert, first, next_expert, slot_of, meta


def _token_copy(src_hbm, src_row, dst_ref, dst_row, sem):
    return pltpu.make_async_copy(src_hbm.at[pl.ds(src_row * TOKEN_ROWS, TOKEN_ROWS), :],
                                 dst_ref.at[pl.ds(dst_row * TOKEN_ROWS, TOKEN_ROWS), :], sem)


def _dispatch_kernel(cnt, pstart, meta, dest_ref, x_ref, xs_hbm, zero_scr, sem):
    i = pl.program_id(0)
    tt = DISPATCH_TILE
    blk = MOE_BM * TOKEN_ROWS
    n_blocks = xs_hbm.shape[0] // blk

    def spare_blocks(fn):
        def one(b, c):
            fn(pltpu.make_async_copy(zero_scr, xs_hbm.at[pl.ds(b * blk, blk), :], sem.at[2]))
            return c

        lax.fori_loop(meta[0], n_blocks, one, 0)

    @pl.when(i == 0)
    def _():
        zero_scr[...] = jnp.zeros(zero_scr.shape, _F32)
        spare_blocks(lambda cp: cp.start())

    def start(j, c):
        for kk in range(TOP_K):
            _token_copy(x_ref, j, xs_hbm, dest_ref[0, 0, TOP_K * j + kk], sem.at[0]).start(priority=kk)
        return c

    lax.fori_loop(0, tt, start, 0, unroll=8)

    def pad_rows(fn):
        def per_expert(e, c):
            n = cnt[e]
            n_fill = (n + MOE_BM - 1) // MOE_BM * MOE_BM - n
            row = pstart[e] + n
            for bit in range(MOE_BM.bit_length() - 1):
                run = 1 << bit

                @pl.when((n_fill & run) != 0)
                def _():
                    first = row + (n_fill & (run - 1))
                    fn(pltpu.make_async_copy(zero_scr.at[pl.ds(0, run * TOKEN_ROWS), :],
                                             xs_hbm.at[pl.ds(first * TOKEN_ROWS, run * TOKEN_ROWS), :],
                                             sem.at[1]))
            return c

        lax.fori_loop(0, N_EXPERTS, per_expert, 0)

    @pl.when(i == 0)
    def _():
        pad_rows(lambda cp: cp.start())
        pad_rows(lambda cp: cp.wait())
        spare_blocks(lambda cp: cp.wait())

    def wait(j, c):
        for kk in range(TOP_K):
            _token_copy(x_ref, 0, xs_hbm, 0, sem.at[0]).wait()
        return c

    lax.fori_loop(0, tt, wait, 0, unroll=8)


def _dispatch(counts, pad_start, meta, dest, x_tok, n_rows):
    n_tokens = x_tok.shape[0] // TOKEN_ROWS
    tt = DISPATCH_TILE
    return pl.pallas_call(
        _dispatch_kernel,
        name="moe_dispatch",
        grid_spec=pltpu.PrefetchScalarGridSpec(
            num_scalar_prefetch=3,
            grid=(n_tokens // tt,),
            in_specs=[pl.BlockSpec((1, 1, TOP_K * tt), lambda i, c, s, m: (i, 0, 0),
                                   memory_space=pltpu.SMEM),
                      pl.BlockSpec((tt * TOKEN_ROWS, LANES), lambda i, c, s, m: (i, 0))],
            out_specs=pl.BlockSpec(memory_space=pl.ANY),
            scratch_shapes=[pltpu.VMEM((MOE_BM * TOKEN_ROWS, LANES), _F32),
                            pltpu.SemaphoreType.DMA((3,))]),
        out_shape=jax.ShapeDtypeStruct((n_rows * TOKEN_ROWS, LANES), _F32),
        compiler_params=_params(),
    )(counts, pad_start, meta, dest.reshape(n_tokens // tt, 1, TOP_K * tt), x_tok)


def _from_token_tiles(ref, n):
    return jnp.concatenate([ref[pl.ds(c, n, stride=TOKEN_ROWS), :] for c in range(TOKEN_ROWS)], axis=1)


def _expert_kernel(lid, bexp, first, nxt, slot_of, meta, xs_ref, wg_hbm, wu_hbm, wd_hbm, o_ref,
                   wg_f, wu_f, wd_f, wg_b, wu_b, wd_b, sem):
    b = pl.program_id(0)
    layer = lid[0]

    def weight_copies(e, s):
        return (pltpu.make_async_copy(wg_hbm.at[layer, e], wg_f.at[s], sem.at[s, 0]),
                pltpu.make_async_copy(wu_hbm.at[layer, e], wu_f.at[s], sem.at[s, 1]),
                pltpu.make_async_copy(wd_hbm.at[layer, e], wd_f.at[s], sem.at[s, 2]))

    @pl.when(b == 0)
    def _():
        for cp in weight_copies(bexp[0], 0):
            cp.start()

    @pl.when(b >= meta[0])
    def _():
        o_ref[...] = jnp.zeros(o_ref.shape, o_ref.dtype)

    @pl.when(b < meta[0])
    def _():
        @pl.when(first[b] == 1)
        def _():
            s = slot_of[b]
            for cp in weight_copies(bexp[b], s):
                cp.wait()
            wg_b[...] = wg_f[s].astype(_MXU)
            wu_b[...] = wu_f[s].astype(_MXU)
            wd_b[...] = wd_f[s].astype(_MXU)

            @pl.when(nxt[b] >= 0)
            def _():
                for cp in weight_copies(nxt[b], 1 - s):
                    cp.start()

        xb = _from_token_tiles(xs_ref, MOE_BM).astype(_MXU)
        hg = _dot(xb, wg_b[...])
        hu = _dot(xb, wu_b[...])
        hid = (hg * _sigmoid(hg)) * hu
        y = _dot(hid.astype(_MXU), wd_b[...])
        for c in range(TOKEN_ROWS):
            o_ref[pl.ds(c, MOE_BM, stride=TOKEN_ROWS), :] = y[:, c * LANES:(c + 1) * LANES]


def _experts(lid, block_expert, first, next_expert, slot_of, meta, xs_tok, w_gate, w_up, w_down):
    rows = MOE_BM * TOKEN_ROWS
    nb = xs_tok.shape[0] // rows
    live = lambda b, m: jnp.minimum(b, m[0] - 1)
    hbm = pl.BlockSpec(memory_space=pl.ANY)
    return pl.pallas_call(
        _expert_kernel,
        name="moe_experts",
        grid_spec=pltpu.PrefetchScalarGridSpec(
            num_scalar_prefetch=6,
            grid=(nb,),
            in_specs=[pl.BlockSpec((rows, LANES), lambda b, l, e, f, n, s, m: (live(b, m), 0)),
                      hbm, hbm, hbm],
            out_specs=pl.BlockSpec((rows, LANES), lambda b, l, e, f, n, s, m: (b, 0)),
            scratch_shapes=[pltpu.VMEM((2, D_MODEL, EXPERT_HIDDEN), _F32),
                            pltpu.VMEM((2, D_MODEL, EXPERT_HIDDEN), _F32),
                            pltpu.VMEM((2, EXPERT_HIDDEN, D_MODEL), _F32),
                            pltpu.VMEM((D_MODEL, EXPERT_HIDDEN), _MXU),
                            pltpu.VMEM((D_MODEL, EXPERT_HIDDEN), _MXU),
                            pltpu.VMEM((EXPERT_HIDDEN, D_MODEL), _MXU),
                            pltpu.SemaphoreType.DMA((2, 3))]),
        out_shape=jax.ShapeDtypeStruct(xs_tok.shape, _F32),
        compiler_params=_params(),
    )(lid, block_expert, first, next_expert, slot_of, meta, xs_tok, w_gate, w_up, w_down)


def _combine_kernel(lid, pos_ref, pos_next_ref, y_hbm, x_ref, route_ref, g2_ref, b2_ref, wpg_ref, bpg_ref,
                    p_ref, wpu_ref, g3_ref, b3_ref, o_ref, ybuf, sem):
    tc = x_ref.shape[0]
    i = pl.program_id(0)
    slot = lax.rem(i, 2)

    def gather(rows_ref, s, fn):
        def body(r, c):
            for kk in range(TOP_K):
                fn(_token_copy(y_hbm, rows_ref[0, 0, TOP_K * r + kk], ybuf.at[s, kk], r, sem.at[s]), kk)
            return c

        lax.fori_loop(0, tc, body, 0, unroll=8)

    start = lambda cp, kk: cp.start(priority=kk)

    @pl.when(i == 0)
    def _():
        gather(pos_ref, 0, start)

    @pl.when(i + 1 < pl.num_programs(0))
    def _():
        gather(pos_next_ref, 1 - slot, start)

    gather(pos_ref, slot, lambda cp, kk: cp.wait())
    route = route_ref[...]
    y = (route[:, 2:3] * _from_token_tiles(ybuf.at[slot, 0], tc)
         + route[:, 3:4] * _from_token_tiles(ybuf.at[slot, 1], tc))
    x2 = _layer_norm(DN_ALPHA * x_ref[...] + y, g2_ref[...], b2_ref[...])
    gate = _sigmoid(_dot(x2.astype(_MXU), wpg_ref[...]) + bpg_ref[...])
    up = _dot(p_ref[...].astype(_MXU), wpu_ref[...])
    o_ref[...] = _layer_norm(DN_ALPHA * x2 + gate * up, g3_ref[...], b3_ref[...])


def _combine(lid, dest, y_tok, x, route, ln2_g, ln2_b, w_pg, b_pg, p, w_pu, ln3_g, ln3_b):
    S = x.shape[0]
    tc = COMBINE_TILE
    n_tiles = S // tc
    pos = dest.reshape(n_tiles, 1, TOP_K * tc)
    vec = lambda n: pl.BlockSpec((None, 1, n), lambda i, l: (l[0], 0, 0))
    return pl.pallas_call(
        _combine_kernel,
        name="combine_ple",
        grid_spec=pltpu.PrefetchScalarGridSpec(
            num_scalar_prefetch=1,
            grid=(S // tc,),
            in_specs=[pl.BlockSpec((1, 1, TOP_K * tc), lambda i, l: (i, 0, 0), memory_space=pltpu.SMEM),
                      pl.BlockSpec((1, 1, TOP_K * tc), lambda i, l: (jnp.minimum(i + 1, n_tiles - 1), 0, 0),
                                   memory_space=pltpu.SMEM),
                      pl.BlockSpec(memory_space=pl.ANY),
                      pl.BlockSpec((tc, D_MODEL), lambda i, l: (i, 0)),
                      pl.BlockSpec((tc, LANES), lambda i, l: (i, 0)),
                      vec(D_MODEL), vec(D_MODEL),
                      pl.BlockSpec((None, D_MODEL, D_MODEL), lambda i, l: (l[0], 0, 0)),
                      vec(D_MODEL),
                      pl.BlockSpec((None, tc, PLE_DIM), lambda i, l: (l[0], i, 0)),
                      pl.BlockSpec((None, PLE_DIM, D_MODEL), lambda i, l: (l[0], 0, 0)),
                      vec(D_MODEL), vec(D_MODEL)],
            out_specs=pl.BlockSpec((tc, D_MODEL), lambda i, l: (i, 0)),
            scratch_shapes=[pltpu.VMEM((2, TOP_K, tc * TOKEN_ROWS, LANES), _F32),
                            pltpu.SemaphoreType.DMA((2,))]),
        out_shape=jax.ShapeDtypeStruct((S, D_MODEL), _F32),
        compiler_params=_params(),
    )(lid, pos, pos, y_tok, x, route, ln2_g, ln2_b, w_pg, b_pg, p, w_pu, ln3_g, ln3_b)


def _split_in_proj(w_in):
    sizes = (SC_WIDTH, SC_WIDTH, SC_WIDTH, GLA_QK, GLA_QK, GLA_VW, GLA_GATE_RANK, GLA_VW,
             MLA_Q_RANK, MLA_KV_RANK, MLA_ROPE, D_MODEL, D_MODEL, D_MODEL)
    offs = [0]
    for s in sizes:
        offs.append(offs[-1] + s)
    col = lambda j: w_in[:, :, offs[j]:offs[j + 1]]
    L = w_in.shape[0]
    zeros = lambda n: jnp.zeros((L, D_MODEL, n), w_in.dtype)
    w_conv_in = jnp.concatenate([col(0), col(1), col(2)], axis=-1)
    w_gla_in = jnp.concatenate([col(3), col(4), col(5), col(7), col(6), zeros(LANES - GLA_GATE_RANK)], axis=-1)
    w_mla_in = jnp.concatenate([col(8), col(9), col(10), zeros(LANES - MLA_ROPE)], axis=-1)
    w_gates = jnp.concatenate([col(11), col(12), col(13)], axis=-1)
    return [w.astype(_MXU) for w in (w_conv_in, w_gla_in, w_mla_in, w_gates)]


def kernel(x, p, positions, ln0_g, ln0_b, w_in, w_conv, w_gla_gate, b_gla_gate, gla_norm_g, mla_q_norm_g, mla_kv_norm_g, w_uq, w_ukv, w_br, w_o, ln1_g, ln1_b, w_grp, b_grp, w_exp, b_exp, w_gate, w_up, w_down, ln2_g, ln2_b, w_ple_gate, b_ple_gate, w_ple_up, ln3_g, ln3_b):
    B, S, D = x.shape
    L = w_in.shape[0]
    assert B == 1 and D == D_MODEL and S % ROW_TILE == 0 and S % ATTN_TQ == 0
    H = MLA_HEADS

    w_conv_in, w_gla_in, w_mla_in, w_gates = _split_in_proj(w_in)
    w_gg = jnp.pad(w_gla_gate, ((0, 0), (0, LANES - GLA_GATE_RANK), (0, 0))).astype(_MXU)
    w_uq_p = jnp.pad(w_uq.reshape(L, MLA_Q_RANK, H, MLA_QK),
                     ((0, 0), (0, 0), (0, 0), (0, MLA_QK_PAD - MLA_QK))
                     ).reshape(L, MLA_Q_RANK, H * MLA_QK_PAD).astype(_MXU)
    w_ukv4 = w_ukv.reshape(L, MLA_KV_RANK, H, MLA_NOPE + MLA_V)
    w_uk = w_ukv4[..., :MLA_NOPE].reshape(L, MLA_KV_RANK, H * MLA_NOPE).astype(_MXU)
    w_uv = w_ukv4[..., MLA_NOPE:].reshape(L, MLA_KV_RANK, H * MLA_V).astype(_MXU)
    w_route = jnp.concatenate(
        [w_grp, w_exp.reshape(L, D, N_EXPERTS),
         jnp.zeros((L, D, LANES - N_GROUPS - N_EXPERTS), _F32)], axis=-1).astype(_MXU)
    b_route = jnp.concatenate(
        [b_grp, b_exp.reshape(L, N_EXPERTS),
         jnp.zeros((L, LANES - N_GROUPS - N_EXPERTS), _F32)], axis=-1).reshape(L, 1, LANES)
    w_br_c = w_br.astype(_MXU)
    w_o_c = w_o.astype(_MXU)
    w_pg_c = w_ple_gate.astype(_MXU)
    w_pu_c = w_ple_up.astype(_MXU)
    row = lambda a: a.reshape(L, 1, a.shape[-1])

    cos_t, sin_t = _rope_tables(positions.reshape(S))
    x0 = _entry_norm(x.reshape(S, D), ln0_g, ln0_b)
    p3 = p.reshape(L, S, PLE_DIM)

    def layer(xc, i):
        lid = jnp.reshape(i, (1,)).astype(jnp.int32)
        y_a, y_b, q, k, v = _mixers(lid, xc, w_conv_in, w_conv, w_gla_in, w_gg, row(b_gla_gate),
                                    row(gla_norm_g), w_mla_in, row(mla_q_norm_g), row(mla_kv_norm_g),
                                    w_uq_p, w_uk, w_uv, cos_t, sin_t)
        y_c = _attention(q, k, v)
        x1, x1_tok, route, counts_row = _merge(lid, xc, y_a, y_b, y_c, w_gates, w_br_c, w_o_c,
                                               row(ln1_g), row(ln1_b), w_route, b_route)
        (dest, counts, pad_start, block_expert, first, next_expert, slot_of,
         meta) = _dispatch_plan(route, counts_row, S)
        n_rows = (S * TOP_K // MOE_BM + N_EXPERTS) * MOE_BM
        xs_tok = _dispatch(counts, pad_start, meta, dest, x1_tok, n_rows)
        y_tok = _experts(lid, block_expert, first, next_expert, slot_of, meta, xs_tok, w_gate, w_up, w_down)
        x3 = _combine(lid, dest, y_tok, x1, route, row(ln2_g), row(ln2_b), w_pg_c, row(b_ple_gate),
                      p3, w_pu_c, row(ln3_g), row(ln3_b))
        return x3, None

    out, _ = lax.scan(layer, x0, jnp.arange(L, dtype=jnp.int32))
    return out.reshape(B, S, D)
```

```python
import functools

import jax
import jax.numpy as jnp
from jax import lax
from jax.experimental import pallas as pl
from jax.experimental.pallas import tpu as pltpu

D_MODEL = 1024
DEPTH = 4
PLE_DIM = 256
SC_WIDTH = 512
SC_CONV = 3
GLA_HEADS = 4
GLA_DK = 64
GLA_DV = 128
GLA_GATE_RANK = 16
GLA_TAU = 16.0
GLA_CHUNK = 64
MLA_HEADS = 4
MLA_NOPE = 128
MLA_ROPE = 64
MLA_V = 128
MLA_Q_RANK = 256
MLA_KV_RANK = 128
ROPE_THETA = 10000.0
N_GROUPS = 8
EXPERTS_PER_GROUP = 8
N_EXPERTS = N_GROUPS * EXPERTS_PER_GROUP
TOP_K = 2
EXPERT_HIDDEN = 256
DN_ALPHA = (2 * DEPTH) ** 0.25
LN_EPS = 1e-5
RMS_EPS = 1e-6

GLA_QK = GLA_HEADS * GLA_DK
GLA_VW = GLA_HEADS * GLA_DV
MLA_QK = MLA_NOPE + MLA_ROPE
MLA_VW = MLA_HEADS * MLA_V

LANES = 128
MLA_QK_PAD = 2 * LANES
VMEM_LIMIT = 56 * 1024 * 1024

ROW_TILE = 512
ATTN_TQ = 2048
ATTN_TK = 1024
ATTN_ROWS = 256
LOG2_E = 1.4426950408889634
MOE_BM = 256
COMBINE_TILE = 256
DISPATCH_TILE = 1024
TOKEN_ROWS = D_MODEL // LANES

_MXU = jnp.bfloat16
_F32 = jnp.float32


def _dot(a, b):
    return jnp.dot(a, b, preferred_element_type=_F32)


def _dot_nt(a, b):
    return lax.dot_general(a, b, (((1,), (1,)), ((), ())), preferred_element_type=_F32)


def _dot_tn(a, b):
    return lax.dot_general(a, b, (((0,), (0,)), ((), ())), preferred_element_type=_F32)


def _layer_norm(x, g, b):
    mu = jnp.mean(x, axis=-1, keepdims=True)
    xc = x - mu
    var = jnp.mean(xc * xc, axis=-1, keepdims=True)
    return xc * lax.rsqrt(var + LN_EPS) * g + b


def _rms_norm(x, g):
    return x * lax.rsqrt(jnp.mean(x * x, axis=-1, keepdims=True) + RMS_EPS) * g


def _sigmoid(x):
    return 1.0 / (1.0 + jnp.exp(-x))


def _params(n_axes=1):
    return pltpu.CompilerParams(dimension_semantics=("arbitrary",) * n_axes,
                                vmem_limit_bytes=VMEM_LIMIT)


def _ln0_kernel(x_ref, g_ref, b_ref, o_ref):
    o_ref[...] = _layer_norm(x_ref[...], g_ref[...], b_ref[...])


def _entry_norm(x, g, b):
    S = x.shape[0]
    return pl.pallas_call(
        _ln0_kernel,
        name="entry_ln",
        grid=(S // ROW_TILE,),
        in_specs=[pl.BlockSpec((ROW_TILE, D_MODEL), lambda i: (i, 0)),
                  pl.BlockSpec((1, D_MODEL), lambda i: (0, 0)),
                  pl.BlockSpec((1, D_MODEL), lambda i: (0, 0))],
        out_specs=pl.BlockSpec((ROW_TILE, D_MODEL), lambda i: (i, 0)),
        out_shape=jax.ShapeDtypeStruct((S, D_MODEL), _F32),
        compiler_params=_params(),
    )(x, g.reshape(1, D_MODEL), b.reshape(1, D_MODEL))


def _rope_kernel(pos_ref, inv_ref, c_ref, s_ref):
    ang = pos_ref[...].astype(_F32) * inv_ref[...]
    lane = lax.broadcasted_iota(jnp.int32, ang.shape, 1)
    half = MLA_ROPE // 2
    cosv = jnp.cos(ang)
    sinv = jnp.sin(ang)
    c_ref[...] = jnp.where(lane < MLA_ROPE, cosv, 0.0)
    s_ref[...] = jnp.where(lane < half, -sinv, jnp.where(lane < MLA_ROPE, sinv, 0.0))


def _rope_tables(positions):
    S = positions.shape[0]
    half = MLA_ROPE // 2
    inv = 1.0 / (ROPE_THETA ** (jnp.arange(0, MLA_ROPE, 2, dtype=_F32) / MLA_ROPE))
    inv = jnp.tile(inv, LANES // half).reshape(1, LANES)
    return pl.pallas_call(
        _rope_kernel,
        name="rope_tables",
        grid=(S // ROW_TILE,),
        in_specs=[pl.BlockSpec((ROW_TILE, 1), lambda i: (i, 0)),
                  pl.BlockSpec((1, LANES), lambda i: (0, 0))],
        out_specs=[pl.BlockSpec((ROW_TILE, LANES), lambda i: (i, 0)),
                   pl.BlockSpec((ROW_TILE, LANES), lambda i: (i, 0))],
        out_shape=[jax.ShapeDtypeStruct((S, LANES), _F32)] * 2,
        compiler_params=_params(),
    )(positions.reshape(S, 1), inv)


def _conv_compute(xb, w_ref, wc_ref, o_ref, u_scr):
    tm = xb.shape[0]
    h = _dot(xb, w_ref[...])
    a_b = h[:, :SC_WIDTH]
    u = h[:, SC_WIDTH:2 * SC_WIDTH] * h[:, 2 * SC_WIDTH:]
    u_scr[8:8 + tm, :] = u
    wc = wc_ref[...]
    y = wc[2:3, :] * u + wc[1:2, :] * u_scr[7:7 + tm, :] + wc[0:1, :] * u_scr[6:6 + tm, :]
    o_ref[...] = (a_b * y).astype(o_ref.dtype)
    u_scr[0:8, :] = u_scr[tm:tm + 8, :]


_GLA_IN = 2 * GLA_QK + 2 * GLA_VW + LANES


def _gla_compute(xb, w_ref, wg_ref, bg_ref, ng_ref, o_ref, st_scr):
    tm = xb.shape[0]
    C = GLA_CHUNK
    h = _dot(xb, w_ref[...])
    q = h[:, :GLA_QK]
    k = h[:, GLA_QK:2 * GLA_QK]
    v = h[:, 2 * GLA_QK:2 * GLA_QK + GLA_VW]
    r = h[:, 2 * GLA_QK + GLA_VW:2 * GLA_QK + 2 * GLA_VW]
    g_lr = h[:, 2 * GLA_QK + 2 * GLA_VW:]
    z = _dot(g_lr.astype(_MXU), wg_ref[...]) + bg_ref[...]
    log_a = (jnp.minimum(z, 0.0) - jnp.log1p(jnp.exp(-jnp.abs(z)))) * (1.0 / GLA_TAU)

    row_in_chunk = lax.broadcasted_iota(jnp.int32, log_a.shape, 0) & (C - 1)
    b = log_a
    d = 1
    while d < C:
        b = b + jnp.where(row_in_chunk >= d, pltpu.roll(b, d, 0), 0.0)
        d *= 2

    ri = lax.broadcasted_iota(jnp.int32, (C, C), 0)
    ci = lax.broadcasted_iota(jnp.int32, (C, C), 1)
    causal = ci <= ri
    scale = GLA_DK ** -0.5
    ng = ng_ref[...]
    for c in range(tm // C):
        rows = slice(c * C, (c + 1) * C)
        bc = b[rows]
        b_last = bc[C - 1:C, :]
        kc = k[rows]
        q_e = (q[rows] * scale) * jnp.exp(bc)
        k_e = kc * jnp.exp(-bc)
        k_t = kc * jnp.exp(b_last - bc)
        decay = jnp.exp(b_last)
        for hh in range(GLA_HEADS):
            ks = slice(hh * GLA_DK, (hh + 1) * GLA_DK)
            vs = slice(hh * GLA_DV, (hh + 1) * GLA_DV)
            qh = q_e[:, ks].astype(_MXU)
            vh = v[rows, vs].astype(_MXU)
            a = jnp.where(causal, _dot_nt(qh, k_e[:, ks].astype(_MXU)), 0.0)
            st = st_scr[hh]
            o = _dot(a.astype(_MXU), vh) + _dot_nt(qh, st.astype(_MXU))
            st_scr[hh] = st * decay[:, ks] + _dot_tn(vh, k_t[:, ks].astype(_MXU))
            rr = r[rows, vs]
            o_ref[rows, vs] = (_rms_norm(o, ng) * (rr * _sigmoid(rr))).astype(o_ref.dtype)


_MLA_IN = MLA_Q_RANK + MLA_KV_RANK + LANES


def _mla_compute(xb, w_ref, qg_ref, kvg_ref, wuq_ref, wk_ref, wv_ref, ct_ref, st_ref, q_o, k_o, v_o):
    h = _dot(xb, w_ref[...])
    c_q = _rms_norm(h[:, :MLA_Q_RANK], qg_ref[...]).astype(_MXU)
    c_kv = _rms_norm(h[:, MLA_Q_RANK:MLA_Q_RANK + MLA_KV_RANK], kvg_ref[...]).astype(_MXU)
    k_rope_raw = h[:, MLA_Q_RANK + MLA_KV_RANK:]
    q = _dot(c_q, wuq_ref[...])
    k_nope = _dot(c_kv, wk_ref[...])
    v = _dot(c_kv, wv_ref[...])
    cos_t = ct_ref[...]
    sin_t = st_ref[...]
    lane = lax.broadcasted_iota(jnp.int32, cos_t.shape, 1)
    half = MLA_ROPE // 2

    def rope(xr):
        rot = jnp.where(lane < half, pltpu.roll(xr, LANES - half, 1), pltpu.roll(xr, half, 1))
        return xr * cos_t + rot * sin_t

    k_rope = rope(k_rope_raw)
    one_col = jnp.where(lane == 0, 1.0, 0.0)
    scale = MLA_QK ** -0.5 * LOG2_E
    for hh in range(MLA_HEADS):
        base = hh * MLA_QK_PAD
        q_h = jnp.concatenate([q[:, base:base + MLA_NOPE], rope(q[:, base + MLA_NOPE:base + MLA_QK_PAD])],
                              axis=1)
        q_o[hh] = (q_h * scale).astype(q_o.dtype)
        k_o[hh] = jnp.concatenate([k_nope[:, hh * MLA_NOPE:(hh + 1) * MLA_NOPE], k_rope],
                                  axis=1).astype(k_o.dtype)
        v_o[hh] = jnp.concatenate([v[:, hh * MLA_V:(hh + 1) * MLA_V], one_col], axis=1).astype(v_o.dtype)


def _mixers_kernel(lid, x_ref,
                   wc_in, wc, wg_in, wg_gate, bg_gate, ng, wm_in, qg, kvg, wuq, wuk, wuv, cos_t, sin_t,
                   ya_o, yb_o, q_o, k_o, v_o, u_scr, st_scr):
    @pl.when(pl.program_id(0) == 0)
    def _():
        u_scr[0:8, :] = jnp.zeros((8, SC_WIDTH), _F32)
        st_scr[...] = jnp.zeros(st_scr.shape, _F32)

    xb = x_ref[...].astype(_MXU)
    _conv_compute(xb, wc_in, wc, ya_o, u_scr)
    _mla_compute(xb, wm_in, qg, kvg, wuq, wuk, wuv, cos_t, sin_t, q_o, k_o, v_o)
    _gla_compute(xb, wg_in, wg_gate, bg_gate, ng, yb_o, st_scr)


def _mixers(lid, x, w_conv_in, w_conv, w_gla_in, w_gate, b_gate, norm_g,
            w_mla_in, q_norm_g, kv_norm_g, w_uq, w_uk, w_uv, cos_t, sin_t):
    S = x.shape[0]
    tm = ROW_TILE
    H = MLA_HEADS
    return pl.pallas_call(
        _mixers_kernel,
        name="mixers",
        grid_spec=pltpu.PrefetchScalarGridSpec(
            num_scalar_prefetch=1,
            grid=(S // tm,),
            in_specs=[pl.BlockSpec((tm, D_MODEL), lambda i, l: (i, 0)),
                      pl.BlockSpec((None, D_MODEL, 3 * SC_WIDTH), lambda i, l: (l[0], 0, 0)),
                      pl.BlockSpec((None, SC_CONV, SC_WIDTH), lambda i, l: (l[0], 0, 0)),
                      pl.BlockSpec((None, D_MODEL, _GLA_IN), lambda i, l: (l[0], 0, 0)),
                      pl.BlockSpec((None, LANES, GLA_QK), lambda i, l: (l[0], 0, 0)),
                      pl.BlockSpec((None, 1, GLA_QK), lambda i, l: (l[0], 0, 0)),
                      pl.BlockSpec((None, 1, GLA_DV), lambda i, l: (l[0], 0, 0)),
                      pl.BlockSpec((None, D_MODEL, _MLA_IN), lambda i, l: (l[0], 0, 0)),
                      pl.BlockSpec((None, 1, MLA_Q_RANK), lambda i, l: (l[0], 0, 0)),
                      pl.BlockSpec((None, 1, MLA_KV_RANK), lambda i, l: (l[0], 0, 0)),
                      pl.BlockSpec((None, MLA_Q_RANK, H * MLA_QK_PAD), lambda i, l: (l[0], 0, 0)),
                      pl.BlockSpec((None, MLA_KV_RANK, H * MLA_NOPE), lambda i, l: (l[0], 0, 0)),
                      pl.BlockSpec((None, MLA_KV_RANK, H * MLA_V), lambda i, l: (l[0], 0, 0)),
                      pl.BlockSpec((tm, LANES), lambda i, l: (i, 0)),
                      pl.BlockSpec((tm, LANES), lambda i, l: (i, 0))],
            out_specs=[pl.BlockSpec((tm, SC_WIDTH), lambda i, l: (i, 0)),
                       pl.BlockSpec((tm, GLA_VW), lambda i, l: (i, 0)),
                       pl.BlockSpec((H, tm, MLA_QK_PAD), lambda i, l: (0, i, 0)),
                       pl.BlockSpec((H, tm, MLA_QK_PAD), lambda i, l: (0, i, 0)),
                       pl.BlockSpec((H, tm, MLA_V + LANES), lambda i, l: (0, i, 0))],
            scratch_shapes=[pltpu.VMEM((tm + 8, SC_WIDTH), _F32),
                            pltpu.VMEM((GLA_HEADS, GLA_DV, GLA_DK), _F32)]),
        out_shape=[jax.ShapeDtypeStruct((S, SC_WIDTH), _MXU),
                   jax.ShapeDtypeStruct((S, GLA_VW), _MXU),
                   jax.ShapeDtypeStruct((H, S, MLA_QK_PAD), _MXU),
                   jax.ShapeDtypeStruct((H, S, MLA_QK_PAD), _MXU),
                   jax.ShapeDtypeStruct((H, S, MLA_V + LANES), _MXU)],
        compiler_params=_params(),
    )(lid, x, w_conv_in, w_conv, w_gla_in, w_gate, b_gate, norm_g,
      w_mla_in, q_norm_g, kv_norm_g, w_uq, w_uk, w_uv, cos_t, sin_t)


def _attn_kernel(q_ref, k_ref, v_ref, o_ref, s_a, s_b, m_scr, acc_scr):
    tq = q_ref.shape[0]
    tk = ATTN_TK
    rb = ATTN_ROWS
    n_rb = tq // rb
    qi = pl.program_id(1)
    m_scr[...] = jnp.full(m_scr.shape, -jnp.inf, _F32)
    acc_scr[...] = jnp.zeros(acc_scr.shape, _F32)

    def scores(dst, start, first_rb=0):
        rows = slice(first_rb * rb, tq)
        dst[rows, :] = _dot_nt(q_ref[rows, :], k_ref[pl.ds(start, tk), :])

    def consume(src, start, col0=None):
        for r in range(n_rb):
            row0 = r * rb
            rows = slice(row0, row0 + rb)
            n_cols = tk
            if col0 is not None:
                n_cols = min(tk, row0 + rb - col0)
                if n_cols <= 0:
                    continue
            s = src[rows, :n_cols]
            if col0 is not None and col0 + n_cols - 1 > row0:
                ri = lax.broadcasted_iota(jnp.int32, s.shape, 0) + row0
                ci = lax.broadcasted_iota(jnp.int32, s.shape, 1) + col0
                s = jnp.where(ci <= ri, s, -jnp.inf)
            v = v_ref[pl.ds(start, n_cols), :]
            m_prev = m_scr[rows, :]
            m_new = jnp.maximum(m_prev, jnp.max(s, axis=1, keepdims=True))
            alpha = jnp.exp2(m_prev - m_new)
            p = jnp.exp2(s - m_new[:, :1]).astype(v.dtype)
            acc_scr[rows, :] = jnp.concatenate([alpha, alpha], axis=1) * acc_scr[rows, :] + _dot(p, v)
            m_scr[rows, :] = m_new

    col = lambda j: pl.multiple_of(j * tk, tk)
    n_diag = tq // tk
    scores(s_a, col(0))

    def body(i, carry):
        scores(s_b, col(2 * i + 1))
        consume(s_a, col(2 * i))
        scores(s_a, col(2 * i + 2))
        consume(s_b, col(2 * i + 1))
        return carry

    lax.fori_loop(0, qi * (n_diag // 2), body, 0)
    bufs = (s_a, s_b)
    for d in range(n_diag):
        if d + 1 < n_diag:
            scores(bufs[(d + 1) % 2], col(n_diag * qi + d + 1), first_rb=(d + 1) * tk // rb)
        consume(bufs[d % 2], col(n_diag * qi + d), col0=d * tk)
    o_ref[...] = (acc_scr[:, :MLA_V] / acc_scr[:, MLA_V:MLA_V + 1]).astype(o_ref.dtype)


def _attention(q, k, v):
    H, S, _ = q.shape
    tq = ATTN_TQ
    return pl.pallas_call(
        _attn_kernel,
        name="mla_attention",
        grid=(H, S // tq),
        in_specs=[pl.BlockSpec((None, tq, MLA_QK_PAD), lambda h, i: (h, i, 0)),
                  pl.BlockSpec((None, S, MLA_QK_PAD), lambda h, i: (h, 0, 0), pipeline_mode=pl.Buffered(1)),
                  pl.BlockSpec((None, S, MLA_V + LANES), lambda h, i: (h, 0, 0), pipeline_mode=pl.Buffered(1))],
        out_specs=pl.BlockSpec((tq, MLA_V), lambda h, i: (i, h)),
        out_shape=jax.ShapeDtypeStruct((S, MLA_VW), _MXU),
        scratch_shapes=[pltpu.VMEM((tq, ATTN_TK), _F32),
                        pltpu.VMEM((tq, ATTN_TK), _F32),
                        pltpu.VMEM((tq, LANES), _F32),
                        pltpu.VMEM((tq, MLA_V + LANES), _F32)],
        compiler_params=_params(2),
    )(q, k, v)


def _merge_kernel(lid, x_ref, ya_ref, yb_ref, yc_ref, wgt_ref, wbr_ref, wo_ref, g_ref, b_ref,
                  wr_ref, br_ref, tril_ref, x_o, xt_o, route_o, cnt_o, cnt_scr):
    tm = x_ref.shape[0]

    @pl.when(pl.program_id(0) == 0)
    def _():
        cnt_scr[...] = jnp.zeros(cnt_scr.shape, _F32)

    x = x_ref[...]
    gates = _dot(x.astype(_MXU), wgt_ref[...])
    wbr = wbr_ref
    merged = (_sigmoid(gates[:, :D_MODEL]) * _dot(ya_ref[...], wbr[0:SC_WIDTH, :])
              + _sigmoid(gates[:, D_MODEL:2 * D_MODEL])
              * _dot(yb_ref[...], wbr[SC_WIDTH:SC_WIDTH + GLA_VW, :])
              + _sigmoid(gates[:, 2 * D_MODEL:]) * _dot(yc_ref[...], wbr[SC_WIDTH + GLA_VW:, :]))
    x1 = _layer_norm(DN_ALPHA * x + _dot(merged.astype(_MXU), wo_ref[...]), g_ref[...], b_ref[...])
    x_o[...] = x1
    for c in range(TOKEN_ROWS):
        xt_o[pl.ds(c, tm, stride=TOKEN_ROWS), :] = x1[:, c * LANES:(c + 1) * LANES]

    logits = _dot(x1.astype(_MXU), wr_ref[...]) + br_ref[...]
    lane = lax.broadcasted_iota(jnp.int32, logits.shape, 1)
    neg = -jnp.inf
    gl = jnp.where(lane < N_GROUPS, logits, neg)
    g_max = jnp.max(gl, axis=1, keepdims=True)
    g_top = jnp.min(jnp.where(gl == g_max, lane, LANES), axis=1, keepdims=True)
    p_g = 1.0 / jnp.sum(jnp.where(lane < N_GROUPS, jnp.exp(logits - g_max), 0.0), axis=1, keepdims=True)
    lo = N_GROUPS + g_top * EXPERTS_PER_GROUP
    sl = jnp.where((lane >= lo) & (lane < lo + EXPERTS_PER_GROUP), logits, neg)
    v0 = jnp.max(sl, axis=1, keepdims=True)
    i0 = jnp.min(jnp.where(sl == v0, lane, LANES), axis=1, keepdims=True)
    sl = jnp.where(lane == i0, neg, sl)
    v1 = jnp.max(sl, axis=1, keepdims=True)
    i1 = jnp.min(jnp.where(sl == v1, lane, LANES), axis=1, keepdims=True)
    e1 = jnp.exp(v1 - v0)
    w0 = p_g / (1.0 + e1)
    w1 = p_g * e1 / (1.0 + e1)
    oh0 = lane == i0
    oh1 = lane == i1
    ohs = jnp.where(oh0, 1.0, jnp.where(oh1, 1.0, 0.0))
    before = _dot(tril_ref[...], ohs.astype(_MXU)) + cnt_scr[0:1, :]
    rank0 = jnp.sum(jnp.where(oh0, before, 0.0), axis=1, keepdims=True)
    rank1 = jnp.sum(jnp.where(oh1, before, 0.0), axis=1, keepdims=True)
    cnt_scr[...] = cnt_scr[...] + jnp.sum(ohs, axis=0, keepdims=True)
    cnt_o[...] = cnt_scr[...]
    route_o[...] = jnp.where(
        lane == 0, (i0 - N_GROUPS).astype(_F32),
        jnp.where(lane == 1, (i1 - N_GROUPS).astype(_F32),
                  jnp.where(lane == 2, w0,
                            jnp.where(lane == 3, w1,
                                      jnp.where(lane == 4, rank0, jnp.where(lane == 5, rank1, 0.0))))))


def _merge(lid, x, y_a, y_b, y_c, w_gt, w_br, w_o, ln_g, ln_b, w_route, b_route):
    S = x.shape[0]
    tm = ROW_TILE
    tril = jnp.tril(jnp.ones((tm, tm), _F32), -1).astype(_MXU)
    br_w = SC_WIDTH + GLA_VW + MLA_VW
    return pl.pallas_call(
        _merge_kernel,
        name="merge_route",
        grid_spec=pltpu.PrefetchScalarGridSpec(
            num_scalar_prefetch=1,
            grid=(S // tm,),
            in_specs=[pl.BlockSpec((tm, D_MODEL), lambda i, l: (i, 0)),
                      pl.BlockSpec((tm, SC_WIDTH), lambda i, l: (i, 0)),
                      pl.BlockSpec((tm, GLA_VW), lambda i, l: (i, 0)),
                      pl.BlockSpec((tm, MLA_VW), lambda i, l: (i, 0)),
                      pl.BlockSpec((None, D_MODEL, 3 * D_MODEL), lambda i, l: (l[0], 0, 0)),
                      pl.BlockSpec((None, br_w, D_MODEL), lambda i, l: (l[0], 0, 0)),
                      pl.BlockSpec((None, D_MODEL, D_MODEL), lambda i, l: (l[0], 0, 0)),
                      pl.BlockSpec((None, 1, D_MODEL), lambda i, l: (l[0], 0, 0)),
                      pl.BlockSpec((None, 1, D_MODEL), lambda i, l: (l[0], 0, 0)),
                      pl.BlockSpec((None, D_MODEL, LANES), lambda i, l: (l[0], 0, 0)),
                      pl.BlockSpec((None, 1, LANES), lambda i, l: (l[0], 0, 0)),
                      pl.BlockSpec((tm, tm), lambda i, l: (0, 0))],
            out_specs=[pl.BlockSpec((tm, D_MODEL), lambda i, l: (i, 0)),
                       pl.BlockSpec((tm * TOKEN_ROWS, LANES), lambda i, l: (i, 0)),
                       pl.BlockSpec((tm, LANES), lambda i, l: (i, 0)),
                       pl.BlockSpec((8, LANES), lambda i, l: (0, 0))],
            scratch_shapes=[pltpu.VMEM((8, LANES), _F32)]),
        out_shape=[jax.ShapeDtypeStruct((S, D_MODEL), _F32),
                   jax.ShapeDtypeStruct((S * TOKEN_ROWS, LANES), _F32),
                   jax.ShapeDtypeStruct((S, LANES), _F32),
                   jax.ShapeDtypeStruct((8, LANES), _F32)],
        compiler_params=_params(),
    )(lid, x, y_a, y_b, y_c, w_gt, w_br, w_o, ln_g, ln_b, w_route, b_route, tril)


def _dispatch_plan(route, counts_row, n_tokens):
    counts = counts_row[0, N_GROUPS:N_GROUPS + N_EXPERTS].astype(jnp.int32)
    padded = (counts + MOE_BM - 1) // MOE_BM * MOE_BM
    pad_end = jnp.cumsum(padded)
    pad_start = pad_end - padded
    expert_id = route[:, :TOP_K].astype(jnp.int32)
    rank = route[:, 4:4 + TOP_K].astype(jnp.int32)
    start_of = jnp.sum(jnp.where(expert_id[..., None] == jnp.arange(N_EXPERTS, dtype=jnp.int32),
                                 pad_start.astype(jnp.int32), 0), axis=-1)
    dest = (start_of + rank).reshape(n_tokens * TOP_K)
    nb = n_tokens * TOP_K // MOE_BM + N_EXPERTS
    block_row = jnp.arange(nb, dtype=jnp.int32) * MOE_BM
    block_expert = jnp.minimum(
        jnp.sum((pad_end[None, :] <= block_row[:, None]).astype(jnp.int32), axis=1),
        N_EXPERTS - 1).astype(jnp.int32)
    n_active = (pad_end[-1] // MOE_BM).astype(jnp.int32)
    first = jnp.concatenate([jnp.ones((1,), jnp.int32),
                             (block_expert[1:] != block_expert[:-1]).astype(jnp.int32)])
    meta = jnp.concatenate([n_active.reshape(1), jnp.zeros((1,), jnp.int32)])
    ids = jnp.arange(N_EXPERTS, dtype=jnp.int32)
    present = counts > 0
    later = jnp.where(present[None, :] & (ids[None, :] > ids[:, None]), ids[None, :], N_EXPERTS)
    next_of = jnp.min(later, axis=1)
    next_of = jnp.where(next_of == N_EXPERTS, -1, next_of).astype(jnp.int32)
    parity_of = ((jnp.cumsum(present.astype(jnp.int32)) - 1) % 2).astype(jnp.int32)
    onehot_be = (block_expert[:, None] == ids[None, :])
    next_expert = jnp.sum(jnp.where(onehot_be, next_of[None, :], 0), axis=1).astype(jnp.int32)
    slot_of = jnp.sum(jnp.where(onehot_be, parity_of[None, :], 0), axis=1).astype(jnp.int32)
    return dest, counts, pad_start.astype(jnp.int32), block_expert, first, next_expert, slot_of, meta


def _token_copy(src_hbm, src_row, dst_ref, dst_row, sem):
    return pltpu.make_async_copy(src_hbm.at[pl.ds(src_row * TOKEN_ROWS, TOKEN_ROWS), :],
                                 dst_ref.at[pl.ds(dst_row * TOKEN_ROWS, TOKEN_ROWS), :], sem)


def _dispatch_kernel(cnt, pstart, meta, dest_ref, x_ref, xs_hbm, zero_scr, sem):
    i = pl.program_id(0)
    tt = DISPATCH_TILE
    blk = MOE_BM * TOKEN_ROWS
    n_blocks = xs_hbm.shape[0] // blk

    def spare_blocks(fn):
        def one(b, c):
            fn(pltpu.make_async_copy(zero_scr, xs_hbm.at[pl.ds(b * blk, blk), :], sem.at[2]))
            return c

        lax.fori_loop(meta[0], n_blocks, one, 0)

    @pl.when(i == 0)
    def _():
        zero_scr[...] = jnp.zeros(zero_scr.shape, _F32)
        spare_blocks(lambda cp: cp.start())

    def start(j, c):
        for kk in range(TOP_K):
            _token_copy(x_ref, j, xs_hbm, dest_ref[0, 0, TOP_K * j + kk], sem.at[0]).start(priority=kk)
        return c

    lax.fori_loop(0, tt, start, 0, unroll=8)

    def pad_rows(fn):
        def per_expert(e, c):
            n = cnt[e]
            n_fill = (n + MOE_BM - 1) // MOE_BM * MOE_BM - n
            row = pstart[e] + n
            for bit in range(MOE_BM.bit_length() - 1):
                run = 1 << bit

                @pl.when((n_fill & run) != 0)
                def _():
                    first = row + (n_fill & (run - 1))
                    fn(pltpu.make_async_copy(zero_scr.at[pl.ds(0, run * TOKEN_ROWS), :],
                                             xs_hbm.at[pl.ds(first * TOKEN_ROWS, run * TOKEN_ROWS), :],
                                             sem.at[1]))
            return c

        lax.fori_loop(0, N_EXPERTS, per_expert, 0)

    @pl.when(i == 0)
    def _():
        pad_rows(lambda cp: cp.start())
        pad_rows(lambda cp: cp.wait())
        spare_blocks(lambda cp: cp.wait())

    def wait(j, c):
        for kk in range(TOP_K):
            _token_copy(x_ref, 0, xs_hbm, 0, sem.at[0]).wait()
        return c

    lax.fori_loop(0, tt, wait, 0, unroll=8)


def _dispatch(counts, pad_start, meta, dest, x_tok, n_rows):
    n_tokens = x_tok.shape[0] // TOKEN_ROWS
    tt = DISPATCH_TILE
    return pl.pallas_call(
        _dispatch_kernel,
        name="moe_dispatch",
        grid_spec=pltpu.PrefetchScalarGridSpec(
            num_scalar_prefetch=3,
            grid=(n_tokens // tt,),
            in_specs=[pl.BlockSpec((1, 1, TOP_K * tt), lambda i, c, s, m: (i, 0, 0),
                                   memory_space=pltpu.SMEM),
                      pl.BlockSpec((tt * TOKEN_ROWS, LANES), lambda i, c, s, m: (i, 0))],
            out_specs=pl.BlockSpec(memory_space=pl.ANY),
            scratch_shapes=[pltpu.VMEM((MOE_BM * TOKEN_ROWS, LANES), _F32),
                            pltpu.SemaphoreType.DMA((3,))]),
        out_shape=jax.ShapeDtypeStruct((n_rows * TOKEN_ROWS, LANES), _F32),
        compiler_params=_params(),
    )(counts, pad_start, meta, dest.reshape(n_tokens // tt, 1, TOP_K * tt), x_tok)


def _from_token_tiles(ref, n):
    return jnp.concatenate([ref[pl.ds(c, n, stride=TOKEN_ROWS), :] for c in range(TOKEN_ROWS)], axis=1)


def _expert_kernel(lid, bexp, first, nxt, slot_of, meta, xs_ref, wg_hbm, wu_hbm, wd_hbm, o_ref,
                   wg_f, wu_f, wd_f, wg_b, wu_b, wd_b, sem):
    b = pl.program_id(0)
    layer = lid[0]

    def weight_copies(e, s):
        return (pltpu.make_async_copy(wg_hbm.at[layer, e], wg_f.at[s], sem.at[s, 0]),
                pltpu.make_async_copy(wu_hbm.at[layer, e], wu_f.at[s], sem.at[s, 1]),
                pltpu.make_async_copy(wd_hbm.at[layer, e], wd_f.at[s], sem.at[s, 2]))

    @pl.when(b == 0)
    def _():
        for cp in weight_copies(bexp[0], 0):
            cp.start()

    @pl.when(b >= meta[0])
    def _():
        o_ref[...] = jnp.zeros(o_ref.shape, o_ref.dtype)

    @pl.when(b < meta[0])
    def _():
        @pl.when(first[b] == 1)
        def _():
            s = slot_of[b]
            for cp in weight_copies(bexp[b], s):
                cp.wait()
            wg_b[...] = wg_f[s].astype(_MXU)
            wu_b[...] = wu_f[s].astype(_MXU)
            wd_b[...] = wd_f[s].astype(_MXU)

            @pl.when(nxt[b] >= 0)
            def _():
                for cp in weight_copies(nxt[b], 1 - s):
                    cp.start()

        xb = _from_token_tiles(xs_ref, MOE_BM).astype(_MXU)
        hg = _dot(xb, wg_b[...])
        hu = _dot(xb, wu_b[...])
        hid = (hg * _sigmoid(hg)) * hu
        y = _dot(hid.astype(_MXU), wd_b[...])
        for c in range(TOKEN_ROWS):
            o_ref[pl.ds(c, MOE_BM, stride=TOKEN_ROWS), :] = y[:, c * LANES:(c + 1) * LANES]


def _experts(lid, block_expert, first, next_expert, slot_of, meta, xs_tok, w_gate, w_up, w_down):
    rows = MOE_BM * TOKEN_ROWS
    nb = xs_tok.shape[0] // rows
    live = lambda b, m: jnp.minimum(b, m[0] - 1)
    hbm = pl.BlockSpec(memory_space=pl.ANY)
    return pl.pallas_call(
        _expert_kernel,
        name="moe_experts",
        grid_spec=pltpu.PrefetchScalarGridSpec(
            num_scalar_prefetch=6,
            grid=(nb,),
            in_specs=[pl.BlockSpec((rows, LANES), lambda b, l, e, f, n, s, m: (live(b, m), 0)),
                      hbm, hbm, hbm],
            out_specs=pl.BlockSpec((rows, LANES), lambda b, l, e, f, n, s, m: (b, 0)),
            scratch_shapes=[pltpu.VMEM((2, D_MODEL, EXPERT_HIDDEN), _F32),
                            pltpu.VMEM((2, D_MODEL, EXPERT_HIDDEN), _F32),
                            pltpu.VMEM((2, EXPERT_HIDDEN, D_MODEL), _F32),
                            pltpu.VMEM((D_MODEL, EXPERT_HIDDEN), _MXU),
                            pltpu.VMEM((D_MODEL, EXPERT_HIDDEN), _MXU),
                            pltpu.VMEM((EXPERT_HIDDEN, D_MODEL), _MXU),
                            pltpu.SemaphoreType.DMA((2, 3))]),
        out_shape=jax.ShapeDtypeStruct(xs_tok.shape, _F32),
        compiler_params=_params(),
    )(lid, block_expert, first, next_expert, slot_of, meta, xs_tok, w_gate, w_up, w_down)


def _combine_kernel(lid, pos_ref, pos_next_ref, y_hbm, x_ref, route_ref, g2_ref, b2_ref, wpg_ref, bpg_ref,
                    p_ref, wpu_ref, g3_ref, b3_ref, o_ref, ybuf, sem):
    tc = x_ref.shape[0]
    i = pl.program_id(0)
    slot = lax.rem(i, 2)

    def gather(rows_ref, s, fn):
        def body(r, c):
            for kk in range(TOP_K):
                fn(_token_copy(y_hbm, rows_ref[0, 0, TOP_K * r + kk], ybuf.at[s, kk], r, sem.at[s]), kk)
            return c

        lax.fori_loop(0, tc, body, 0, unroll=8)

    start = lambda cp, kk: cp.start(priority=kk)

    @pl.when(i == 0)
    def _():
        gather(pos_ref, 0, start)

    @pl.when(i + 1 < pl.num_programs(0))
    def _():
        gather(pos_next_ref, 1 - slot, start)

    gather(pos_ref, slot, lambda cp, kk: cp.wait())
    route = route_ref[...]
    y = (route[:, 2:3] * _from_token_tiles(ybuf.at[slot, 0], tc)
         + route[:, 3:4] * _from_token_tiles(ybuf.at[slot, 1], tc))
    x2 = _layer_norm(DN_ALPHA * x_ref[...] + y, g2_ref[...], b2_ref[...])
    gate = _sigmoid(_dot(x2.astype(_MXU), wpg_ref[...]) + bpg_ref[...])
    up = _dot(p_ref[...].astype(_MXU), wpu_ref[...])
    o_ref[...] = _layer_norm(DN_ALPHA * x2 + gate * up, g3_ref[...], b3_ref[...])


def _combine(lid, dest, y_tok, x, route, ln2_g, ln2_b, w_pg, b_pg, p, w_pu, ln3_g, ln3_b):
    S = x.shape[0]
    tc = COMBINE_TILE
    n_tiles = S // tc
    pos = dest.reshape(n_tiles, 1, TOP_K * tc)
    vec = lambda n: pl.BlockSpec((None, 1, n), lambda i, l: (l[0], 0, 0))
    return pl.pallas_call(
        _combine_kernel,
        name="combine_ple",
        grid_spec=pltpu.PrefetchScalarGridSpec(
            num_scalar_prefetch=1,
            grid=(S // tc,),
            in_specs=[pl.BlockSpec((1, 1, TOP_K * tc), lambda i, l: (i, 0, 0), memory_space=pltpu.SMEM),
                      pl.BlockSpec((1, 1, TOP_K * tc), lambda i, l: (jnp.minimum(i + 1, n_tiles - 1), 0, 0),
                                   memory_space=pltpu.SMEM),
                      pl.BlockSpec(memory_space=pl.ANY),
                      pl.BlockSpec((tc, D_MODEL), lambda i, l: (i, 0)),
                      pl.BlockSpec((tc, LANES), lambda i, l: (i, 0)),
                      vec(D_MODEL), vec(D_MODEL),
                      pl.BlockSpec((None, D_MODEL, D_MODEL), lambda i, l: (l[0], 0, 0)),
                      vec(D_MODEL),
                      pl.BlockSpec((None, tc, PLE_DIM), lambda i, l: (l[0], i, 0)),
                      pl.BlockSpec((None, PLE_DIM, D_MODEL), lambda i, l: (l[0], 0, 0)),
                      vec(D_MODEL), vec(D_MODEL)],
            out_specs=pl.BlockSpec((tc, D_MODEL), lambda i, l: (i, 0)),
            scratch_shapes=[pltpu.VMEM((2, TOP_K, tc * TOKEN_ROWS, LANES), _F32),
                            pltpu.SemaphoreType.DMA((2,))]),
        out_shape=jax.ShapeDtypeStruct((S, D_MODEL), _F32),
        compiler_params=_params(),
    )(lid, pos, pos, y_tok, x, route, ln2_g, ln2_b, w_pg, b_pg, p, w_pu, ln3_g, ln3_b)


def _split_in_proj(w_in):
    sizes = (SC_WIDTH, SC_WIDTH, SC_WIDTH, GLA_QK, GLA_QK, GLA_VW, GLA_GATE_RANK, GLA_VW,
             MLA_Q_RANK, MLA_KV_RANK, MLA_ROPE, D_MODEL, D_MODEL, D_MODEL)
    offs = [0]
    for s in sizes:
        offs.append(offs[-1] + s)
    col = lambda j: w_in[:, :, offs[j]:offs[j + 1]]
    L = w_in.shape[0]
    zeros = lambda n: jnp.zeros((L, D_MODEL, n), w_in.dtype)
    w_conv_in = jnp.concatenate([col(0), col(1), col(2)], axis=-1)
    w_gla_in = jnp.concatenate([col(3), col(4), col(5), col(7), col(6), zeros(LANES - GLA_GATE_RANK)], axis=-1)
    w_mla_in = jnp.concatenate([col(8), col(9), col(10), zeros(LANES - MLA_ROPE)], axis=-1)
    w_gates = jnp.concatenate([col(11), col(12), col(13)], axis=-1)
    return [w.astype(_MXU) for w in (w_conv_in, w_gla_in, w_mla_in, w_gates)]


def kernel(x, p, positions, ln0_g, ln0_b, w_in, w_conv, w_gla_gate, b_gla_gate, gla_norm_g, mla_q_norm_g, mla_kv_norm_g, w_uq, w_ukv, w_br, w_o, ln1_g, ln1_b, w_grp, b_grp, w_exp, b_exp, w_gate, w_up, w_down, ln2_g, ln2_b, w_ple_gate, b_ple_gate, w_ple_up, ln3_g, ln3_b):
    B, S, D = x.shape
    L = w_in.shape[0]
    assert B == 1 and D == D_MODEL and S % ROW_TILE == 0 and S % ATTN_TQ == 0
    H = MLA_HEADS

    w_conv_in, w_gla_in, w_mla_in, w_gates = _split_in_proj(w_in)
    w_gg = jnp.pad(w_gla_gate, ((0, 0), (0, LANES - GLA_GATE_RANK), (0, 0))).astype(_MXU)
    w_uq_p = jnp.pad(w_uq.reshape(L, MLA_Q_RANK, H, MLA_QK),
                     ((0, 0), (0, 0), (0, 0), (0, MLA_QK_PAD - MLA_QK))
                     ).reshape(L, MLA_Q_RANK, H * MLA_QK_PAD).astype(_MXU)
    w_ukv4 = w_ukv.reshape(L, MLA_KV_RANK, H, MLA_NOPE + MLA_V)
    w_uk = w_ukv4[..., :MLA_NOPE].reshape(L, MLA_KV_RANK, H * MLA_NOPE).astype(_MXU)
    w_uv = w_ukv4[..., MLA_NOPE:].reshape(L, MLA_KV_RANK, H * MLA_V).astype(_MXU)
    w_route = jnp.concatenate(
        [w_grp, w_exp.reshape(L, D, N_EXPERTS),
         jnp.zeros((L, D, LANES - N_GROUPS - N_EXPERTS), _F32)], axis=-1).astype(_MXU)
    b_route = jnp.concatenate(
        [b_grp, b_exp.reshape(L, N_EXPERTS),
         jnp.zeros((L, LANES - N_GROUPS - N_EXPERTS), _F32)], axis=-1).reshape(L, 1, LANES)
    w_br_c = w_br.astype(_MXU)
    w_o_c = w_o.astype(_MXU)
    w_pg_c = w_ple_gate.astype(_MXU)
    w_pu_c = w_ple_up.astype(_MXU)
    row = lambda a: a.reshape(L, 1, a.shape[-1])

    cos_t, sin_t = _rope_tables(positions.reshape(S))
    x0 = _entry_norm(x.reshape(S, D), ln0_g, ln0_b)
    p3 = p.reshape(L, S, PLE_DIM)

    def layer(xc, i):
        lid = jnp.reshape(i, (1,)).astype(jnp.int32)
        y_a, y_b, q, k, v = _mixers(lid, xc, w_conv_in, w_conv, w_gla_in, w_gg, row(b_gla_gate),
                                    row(gla_norm_g), w_mla_in, row(mla_q_norm_g), row(mla_kv_norm_g),
                                    w_uq_p, w_uk, w_uv, cos_t, sin_t)
        y_c = _attention(q, k, v)
        x1, x1_tok, route, counts_row = _merge(lid, xc, y_a, y_b, y_c, w_gates, w_br_c, w_o_c,
                                               row(ln1_g), row(ln1_b), w_route, b_route)
        (dest, counts, pad_start, block_expert, first, next_expert, slot_of,
         meta) = _dispatch_plan(route, counts_row, S)
        n_rows = (S * TOP_K // MOE_BM + N_EXPERTS) * MOE_BM
        xs_tok = _dispatch(counts, pad_start, meta, dest, x1_tok, n_rows)
        y_tok = _experts(lid, block_expert, first, next_expert, slot_of, meta, xs_tok, w_gate, w_up, w_down)
        x3 = _combine(lid, dest, y_tok, x1, route, row(ln2_g), row(ln2_b), w_pg_c, row(b_ple_gate),
                      p3, w_pu_c, row(ln3_g), row(ln3_b))
        return x3, None

    out, _ = lax.scan(layer, x0, jnp.arange(L, dtype=jnp.int32))
    return out.reshape(B, S, D)
```

```python
import jax
import jax.numpy as jnp
from jax import lax
from jax.experimental import pallas as pl
from jax.experimental.pallas import tpu as pltpu

D_MODEL = 1024
DEPTH = 4
PLE_DIM = 256
SC_WIDTH = 512
SC_CONV = 3
GLA_HEADS = 4
GLA_DK = 64
GLA_DV = 128
GLA_GATE_RANK = 16
GLA_TAU = 16.0
GLA_CHUNK = 64
MLA_HEADS = 4
MLA_NOPE = 128
MLA_ROPE = 64
MLA_V = 128
MLA_Q_RANK = 256
MLA_KV_RANK = 128
ROPE_THETA = 10000.0
N_GROUPS = 8
EXPERTS_PER_GROUP = 8
N_EXPERTS = N_GROUPS * EXPERTS_PER_GROUP
TOP_K = 2
EXPERT_HIDDEN = 256
DN_ALPHA = (2 * DEPTH) ** 0.25
LN_EPS = 1e-5
RMS_EPS = 1e-6

GLA_QK = GLA_HEADS * GLA_DK
GLA_VW = GLA_HEADS * GLA_DV
MLA_QK = MLA_NOPE + MLA_ROPE
MLA_VW = MLA_HEADS * MLA_V

LANES = 128
SUBLANES = 8
MLA_QK_PAD = 2 * LANES
VMEM_LIMIT = 56 * 1024 * 1024

ROW_TILE = 1024
ATTN_TQ = 2048
ATTN_TK = 1024
ATTN_ROWS = 256
LOG2_E = 1.4426950408889634
MOE_BM = 256
COMBINE_TILE = 512
DISPATCH_TILE = 1024
TOKEN_ROWS = D_MODEL // LANES

_MXU = jnp.bfloat16
_F32 = jnp.float32


def _dot(a, b):
    return jnp.dot(a, b, preferred_element_type=_F32)


def _dot_nt(a, b):
    return lax.dot_general(a, b, (((1,), (1,)), ((), ())), preferred_element_type=_F32)


def _dot_tn(a, b):
    return lax.dot_general(a, b, (((0,), (0,)), ((), ())), preferred_element_type=_F32)


def _layer_norm(x, g, b):
    mu = jnp.mean(x, axis=-1, keepdims=True)
    xc = x - mu
    var = jnp.mean(xc * xc, axis=-1, keepdims=True)
    return xc * lax.rsqrt(var + LN_EPS) * g + b


def _rms_norm(x, g):
    return x * lax.rsqrt(jnp.mean(x * x, axis=-1, keepdims=True) + RMS_EPS) * g


def _sigmoid(x):
    return 1.0 / (1.0 + jnp.exp(-x))


def _params(n_axes=1):
    return pltpu.CompilerParams(dimension_semantics=("arbitrary",) * n_axes,
                                vmem_limit_bytes=VMEM_LIMIT)


def _ln0_kernel(x_ref, g_ref, b_ref, o_ref):
    o_ref[...] = _layer_norm(x_ref[...], g_ref[...], b_ref[...])


def _entry_norm(x, g, b):
    S = x.shape[0]
    return pl.pallas_call(
        _ln0_kernel,
        name="entry_ln",
        grid=(S // ROW_TILE,),
        in_specs=[pl.BlockSpec((ROW_TILE, D_MODEL), lambda i: (i, 0)),
                  pl.BlockSpec((1, D_MODEL), lambda i: (0, 0)),
                  pl.BlockSpec((1, D_MODEL), lambda i: (0, 0))],
        out_specs=pl.BlockSpec((ROW_TILE, D_MODEL), lambda i: (i, 0)),
        out_shape=jax.ShapeDtypeStruct((S, D_MODEL), _F32),
        compiler_params=_params(),
    )(x, g.reshape(1, D_MODEL), b.reshape(1, D_MODEL))


def _rope_kernel(pos_ref, inv_ref, c_ref, s_ref):
    ang = pos_ref[...].astype(_F32) * inv_ref[...]
    lane = lax.broadcasted_iota(jnp.int32, ang.shape, 1)
    half = MLA_ROPE // 2
    cosv = jnp.cos(ang)
    sinv = jnp.sin(ang)
    c_ref[...] = jnp.where(lane < MLA_ROPE, cosv, 0.0)
    s_ref[...] = jnp.where(lane < half, -sinv, jnp.where(lane < MLA_ROPE, sinv, 0.0))


def _rope_tables(positions):
    S = positions.shape[0]
    half = MLA_ROPE // 2
    inv = 1.0 / (ROPE_THETA ** (jnp.arange(0, MLA_ROPE, 2, dtype=_F32) / MLA_ROPE))
    inv = jnp.tile(inv, LANES // half).reshape(1, LANES)
    return pl.pallas_call(
        _rope_kernel,
        name="rope_tables",
        grid=(S // ROW_TILE,),
        in_specs=[pl.BlockSpec((ROW_TILE, 1), lambda i: (i, 0)),
                  pl.BlockSpec((1, LANES), lambda i: (0, 0))],
        out_specs=[pl.BlockSpec((ROW_TILE, LANES), lambda i: (i, 0)),
                   pl.BlockSpec((ROW_TILE, LANES), lambda i: (i, 0))],
        out_shape=[jax.ShapeDtypeStruct((S, LANES), _F32)] * 2,
        compiler_params=_params(),
    )(positions.reshape(S, 1), inv)


def _conv_compute(xb, w_ref, wc_ref, o_ref, u_scr):
    tm = xb.shape[0]
    h = _dot(xb, w_ref[...])
    a_b = h[:, :SC_WIDTH]
    u = h[:, SC_WIDTH:2 * SC_WIDTH] * h[:, 2 * SC_WIDTH:]
    pad = SUBLANES
    u_scr[pad:pad + tm, :] = u
    wc = wc_ref[...]
    y = (wc[2:3, :] * u + wc[1:2, :] * u_scr[pad - 1:pad - 1 + tm, :]
         + wc[0:1, :] * u_scr[pad - 2:pad - 2 + tm, :])
    o_ref[...] = (a_b * y).astype(o_ref.dtype)
    u_scr[0:pad, :] = u_scr[tm:tm + pad, :]


_GLA_IN = 2 * GLA_QK + 2 * GLA_VW + LANES


def _gla_project(xb, w_ref, wg_ref, bg_ref):
    C = GLA_CHUNK
    h = _dot(xb, w_ref[...])
    q = h[:, :GLA_QK]
    k = h[:, GLA_QK:2 * GLA_QK]
    v = h[:, 2 * GLA_QK:2 * GLA_QK + GLA_VW]
    r = h[:, 2 * GLA_QK + GLA_VW:2 * GLA_QK + 2 * GLA_VW]
    g_lr = h[:, 2 * GLA_QK + 2 * GLA_VW:]
    z = _dot(g_lr.astype(_MXU), wg_ref[...]) + bg_ref[...]
    log_a = (jnp.minimum(z, 0.0) - jnp.log1p(jnp.exp(-jnp.abs(z)))) * (1.0 / GLA_TAU)

    row_in_chunk = lax.broadcasted_iota(jnp.int32, log_a.shape, 0) & (C - 1)
    b = log_a
    d = 1
    while d < C:
        b = b + jnp.where(row_in_chunk >= d, pltpu.roll(b, d, 0), 0.0)
        d *= 2
    return q, k, v, r, b


def _gla_recur(q, k, v, r, b, ng_ref, o_ref, st_scr):
    tm = q.shape[0]
    C = GLA_CHUNK
    ri = lax.broadcasted_iota(jnp.int32, (C, C), 0)
    ci = lax.broadcasted_iota(jnp.int32, (C, C), 1)
    causal = ci <= ri
    scale = GLA_DK ** -0.5
    ng = ng_ref[...]
    for c in range(tm // C):
        rows = slice(c * C, (c + 1) * C)
        bc = b[rows]
        b_last = bc[C - 1:C, :]
        kc = k[rows]
        q_e = (q[rows] * scale) * jnp.exp(bc)
        k_e = kc * jnp.exp(-bc)
        k_t = kc * jnp.exp(b_last - bc)
        decay = jnp.exp(b_last)
        for hh in range(GLA_HEADS):
            ks = slice(hh * GLA_DK, (hh + 1) * GLA_DK)
            vs = slice(hh * GLA_DV, (hh + 1) * GLA_DV)
            qh = q_e[:, ks].astype(_MXU)
            vh = v[rows, vs].astype(_MXU)
            a = jnp.where(causal, _dot_nt(qh, k_e[:, ks].astype(_MXU)), 0.0)
            st = st_scr[hh]
            o = _dot(a.astype(_MXU), vh) + _dot_nt(qh, st.astype(_MXU))
            st_scr[hh] = st * decay[:, ks] + _dot_tn(vh, k_t[:, ks].astype(_MXU))
            rr = r[rows, vs]
            o_ref[rows, vs] = (_rms_norm(o, ng) * (rr * _sigmoid(rr))).astype(o_ref.dtype)


_MLA_IN = MLA_Q_RANK + MLA_KV_RANK + LANES


def _mla_compute(xb, w_ref, qg_ref, kvg_ref, wuq_ref, wk_ref, wv_ref, ct_ref, st_ref, q_o, k_o, v_o):
    h = _dot(xb, w_ref[...])
    c_q = _rms_norm(h[:, :MLA_Q_RANK], qg_ref[...]).astype(_MXU)
    c_kv = _rms_norm(h[:, MLA_Q_RANK:MLA_Q_RANK + MLA_KV_RANK], kvg_ref[...]).astype(_MXU)
    k_rope_raw = h[:, MLA_Q_RANK + MLA_KV_RANK:]
    q = _dot(c_q, wuq_ref[...])
    k_nope = _dot(c_kv, wk_ref[...])
    v = _dot(c_kv, wv_ref[...])
    cos_t = ct_ref[...]
    sin_t = st_ref[...]
    lane = lax.broadcasted_iota(jnp.int32, cos_t.shape, 1)
    half = MLA_ROPE // 2

    def rope(xr):
        rot = jnp.where(lane < half, pltpu.roll(xr, LANES - half, 1), pltpu.roll(xr, half, 1))
        return xr * cos_t + rot * sin_t

    k_rope = rope(k_rope_raw)
    one_col = jnp.where(lane == 0, 1.0, 0.0)
    scale = MLA_QK ** -0.5 * LOG2_E
    for hh in range(MLA_HEADS):
        base = hh * MLA_QK_PAD
        q_h = jnp.concatenate([q[:, base:base + MLA_NOPE], rope(q[:, base + MLA_NOPE:base + MLA_QK_PAD])],
                              axis=1)
        q_o[hh] = (q_h * scale).astype(q_o.dtype)
        k_o[hh] = jnp.concatenate([k_nope[:, hh * MLA_NOPE:(hh + 1) * MLA_NOPE], k_rope],
                                  axis=1).astype(k_o.dtype)
        v_o[hh] = jnp.concatenate([v[:, hh * MLA_V:(hh + 1) * MLA_V], one_col], axis=1).astype(v_o.dtype)


def _mixers_kernel(lid, x_ref,
                   wc_in, wc, wg_in, wg_gate, bg_gate, ng, wm_in, qg, kvg, wuq, wuk, wuv, cos_t, sin_t,
                   ya_o, yb_o, q_o, k_o, v_o, u_scr, st_scr):
    @pl.when(pl.program_id(0) == 0)
    def _():
        u_scr[0:SUBLANES, :] = jnp.zeros((SUBLANES, SC_WIDTH), _F32)
        st_scr[...] = jnp.zeros(st_scr.shape, _F32)

    xb = x_ref[...].astype(_MXU)
    gla = _gla_project(xb, wg_in, wg_gate, bg_gate)
    _conv_compute(xb, wc_in, wc, ya_o, u_scr)
    _mla_compute(xb, wm_in, qg, kvg, wuq, wuk, wuv, cos_t, sin_t, q_o, k_o, v_o)
    _gla_recur(*gla, ng, yb_o, st_scr)


def _mixers(lid, x, w_conv_in, w_conv, w_gla_in, w_gate, b_gate, norm_g,
            w_mla_in, q_norm_g, kv_norm_g, w_uq, w_uk, w_uv, cos_t, sin_t):
    S = x.shape[0]
    tm = ROW_TILE
    H = MLA_HEADS
    return pl.pallas_call(
        _mixers_kernel,
        name="mixers",
        grid_spec=pltpu.PrefetchScalarGridSpec(
            num_scalar_prefetch=1,
            grid=(S // tm,),
            in_specs=[pl.BlockSpec((tm, D_MODEL), lambda i, l: (i, 0)),
                      pl.BlockSpec((None, D_MODEL, 3 * SC_WIDTH), lambda i, l: (l[0], 0, 0)),
                      pl.BlockSpec((None, SC_CONV, SC_WIDTH), lambda i, l: (l[0], 0, 0)),
                      pl.BlockSpec((None, D_MODEL, _GLA_IN), lambda i, l: (l[0], 0, 0)),
                      pl.BlockSpec((None, LANES, GLA_QK), lambda i, l: (l[0], 0, 0)),
                      pl.BlockSpec((None, 1, GLA_QK), lambda i, l: (l[0], 0, 0)),
                      pl.BlockSpec((None, 1, GLA_DV), lambda i, l: (l[0], 0, 0)),
                      pl.BlockSpec((None, D_MODEL, _MLA_IN), lambda i, l: (l[0], 0, 0)),
                      pl.BlockSpec((None, 1, MLA_Q_RANK), lambda i, l: (l[0], 0, 0)),
                      pl.BlockSpec((None, 1, MLA_KV_RANK), lambda i, l: (l[0], 0, 0)),
                      pl.BlockSpec((None, MLA_Q_RANK, H * MLA_QK_PAD), lambda i, l: (l[0], 0, 0)),
                      pl.BlockSpec((None, MLA_KV_RANK, H * MLA_NOPE), lambda i, l: (l[0], 0, 0)),
                      pl.BlockSpec((None, MLA_KV_RANK, H * MLA_V), lambda i, l: (l[0], 0, 0)),
                      pl.BlockSpec((tm, LANES), lambda i, l: (i, 0)),
                      pl.BlockSpec((tm, LANES), lambda i, l: (i, 0))],
            out_specs=[pl.BlockSpec((tm, SC_WIDTH), lambda i, l: (i, 0)),
                       pl.BlockSpec((tm, GLA_VW), lambda i, l: (i, 0)),
                       pl.BlockSpec((H, tm, MLA_QK_PAD), lambda i, l: (0, i, 0)),
                       pl.BlockSpec((H, tm, MLA_QK_PAD), lambda i, l: (0, i, 0)),
                       pl.BlockSpec((H, tm, MLA_V + LANES), lambda i, l: (0, i, 0))],
            scratch_shapes=[pltpu.VMEM((tm + SUBLANES, SC_WIDTH), _F32),
                            pltpu.VMEM((GLA_HEADS, GLA_DV, GLA_DK), _F32)]),
        out_shape=[jax.ShapeDtypeStruct((S, SC_WIDTH), _MXU),
                   jax.ShapeDtypeStruct((S, GLA_VW), _MXU),
                   jax.ShapeDtypeStruct((H, S, MLA_QK_PAD), _MXU),
                   jax.ShapeDtypeStruct((H, S, MLA_QK_PAD), _MXU),
                   jax.ShapeDtypeStruct((H, S, MLA_V + LANES), _MXU)],
        compiler_params=_params(),
    )(lid, x, w_conv_in, w_conv, w_gla_in, w_gate, b_gate, norm_g,
      w_mla_in, q_norm_g, kv_norm_g, w_uq, w_uk, w_uv, cos_t, sin_t)


def _attn_kernel(q_ref, k_ref, v_ref, o_ref, s_a, s_b, m_scr, acc_scr):
    tq = q_ref.shape[0]
    tk = ATTN_TK
    rb = ATTN_ROWS
    n_rb = tq // rb
    qi = pl.program_id(1)
    m_scr[...] = jnp.full(m_scr.shape, -jnp.inf, _F32)
    acc_scr[...] = jnp.zeros(acc_scr.shape, _F32)

    def scores(dst, start, first_rb=0):
        rows = slice(first_rb * rb, tq)
        dst[rows, :] = _dot_nt(q_ref[rows, :], k_ref[pl.ds(start, tk), :])

    def consume(src, start, col0=None):
        for r in range(n_rb):
            row0 = r * rb
            rows = slice(row0, row0 + rb)
            n_cols = tk
            if col0 is not None:
                n_cols = min(tk, row0 + rb - col0)
                if n_cols <= 0:
                    continue
            s = src[rows, :n_cols]
            if col0 is not None and col0 + n_cols - 1 > row0:
                ri = lax.broadcasted_iota(jnp.int32, s.shape, 0) + row0
                ci = lax.broadcasted_iota(jnp.int32, s.shape, 1) + col0
                s = jnp.where(ci <= ri, s, -jnp.inf)
            v = v_ref[pl.ds(start, n_cols), :]
            m_prev = m_scr[rows, :]
            m_new = jnp.maximum(m_prev, jnp.max(s, axis=1, keepdims=True))
            alpha = jnp.exp2(m_prev - m_new)
            p = jnp.exp2(s - m_new[:, :1]).astype(v.dtype)
            acc_scr[rows, :] = jnp.concatenate([alpha, alpha], axis=1) * acc_scr[rows, :] + _dot(p, v)
            m_scr[rows, :] = m_new

    col = lambda j: pl.multiple_of(j * tk, tk)
    n_diag = tq // tk
    scores(s_a, col(0))

    def body(i, carry):
        scores(s_b, col(2 * i + 1))
        consume(s_a, col(2 * i))
        scores(s_a, col(2 * i + 2))
        consume(s_b, col(2 * i + 1))
        return carry

    lax.fori_loop(0, qi * (n_diag // 2), body, 0)
    bufs = (s_a, s_b)
    for d in range(n_diag):
        if d + 1 < n_diag:
            scores(bufs[(d + 1) % 2], col(n_diag * qi + d + 1), first_rb=(d + 1) * tk // rb)
        consume(bufs[d % 2], col(n_diag * qi + d), col0=d * tk)
    o_ref[...] = (acc_scr[:, :MLA_V] / acc_scr[:, MLA_V:MLA_V + 1]).astype(o_ref.dtype)


def _attention(q, k, v):
    H, S, _ = q.shape
    tq = ATTN_TQ
    return pl.pallas_call(
        _attn_kernel,
        name="mla_attention",
        grid=(H, S // tq),
        in_specs=[pl.BlockSpec((None, tq, MLA_QK_PAD), lambda h, i: (h, i, 0)),
                  pl.BlockSpec((None, S, MLA_QK_PAD), lambda h, i: (h, 0, 0), pipeline_mode=pl.Buffered(1)),
                  pl.BlockSpec((None, S, MLA_V + LANES), lambda h, i: (h, 0, 0), pipeline_mode=pl.Buffered(1))],
        out_specs=pl.BlockSpec((tq, MLA_V), lambda h, i: (i, h)),
        out_shape=jax.ShapeDtypeStruct((S, MLA_VW), _MXU),
        scratch_shapes=[pltpu.VMEM((tq, ATTN_TK), _F32),
                        pltpu.VMEM((tq, ATTN_TK), _F32),
                        pltpu.VMEM((tq, LANES), _F32),
                        pltpu.VMEM((tq, MLA_V + LANES), _F32)],
        compiler_params=_params(2),
    )(q, k, v)


def _merge_kernel(lid, x_ref, ya_ref, yb_ref, yc_ref, wgt_ref, wbr_ref, wo_ref, g_ref, b_ref,
                  wr_ref, br_ref, tril_ref, x_o, xt_o, route_o, cnt_o, cnt_scr):
    tm = x_ref.shape[0]

    @pl.when(pl.program_id(0) == 0)
    def _():
        cnt_scr[...] = jnp.zeros(cnt_scr.shape, _F32)

    x = x_ref[...]
    gates = _dot(x.astype(_MXU), wgt_ref[...])
    wbr = wbr_ref
    merged = (_sigmoid(gates[:, :D_MODEL]) * _dot(ya_ref[...], wbr[0:SC_WIDTH, :])
              + _sigmoid(gates[:, D_MODEL:2 * D_MODEL])
              * _dot(yb_ref[...], wbr[SC_WIDTH:SC_WIDTH + GLA_VW, :])
              + _sigmoid(gates[:, 2 * D_MODEL:]) * _dot(yc_ref[...], wbr[SC_WIDTH + GLA_VW:, :]))
    x1 = _layer_norm(DN_ALPHA * x + _dot(merged.astype(_MXU), wo_ref[...]), g_ref[...], b_ref[...])
    x_o[...] = x1
    for c in range(TOKEN_ROWS):
        xt_o[pl.ds(c, tm, stride=TOKEN_ROWS), :] = x1[:, c * LANES:(c + 1) * LANES]

    logits = _dot(x1.astype(_MXU), wr_ref[...]) + br_ref[...]
    lane = lax.broadcasted_iota(jnp.int32, logits.shape, 1)
    neg = -jnp.inf
    gl = jnp.where(lane < N_GROUPS, logits, neg)
    g_max = jnp.max(gl, axis=1, keepdims=True)
    g_top = jnp.min(jnp.where(gl == g_max, lane, LANES), axis=1, keepdims=True)
    p_g = 1.0 / jnp.sum(jnp.where(lane < N_GROUPS, jnp.exp(logits - g_max), 0.0), axis=1, keepdims=True)
    lo = N_GROUPS + g_top * EXPERTS_PER_GROUP
    sl = jnp.where((lane >= lo) & (lane < lo + EXPERTS_PER_GROUP), logits, neg)
    v0 = jnp.max(sl, axis=1, keepdims=True)
    i0 = jnp.min(jnp.where(sl == v0, lane, LANES), axis=1, keepdims=True)
    sl = jnp.where(lane == i0, neg, sl)
    v1 = jnp.max(sl, axis=1, keepdims=True)
    i1 = jnp.min(jnp.where(sl == v1, lane, LANES), axis=1, keepdims=True)
    e1 = jnp.exp(v1 - v0)
    w0 = p_g / (1.0 + e1)
    w1 = p_g * e1 / (1.0 + e1)
    oh0 = lane == i0
    oh1 = lane == i1
    ohs = jnp.where(oh0, 1.0, jnp.where(oh1, 1.0, 0.0))
    before = _dot(tril_ref[...], ohs.astype(_MXU)) + cnt_scr[0:1, :]
    rank0 = jnp.sum(jnp.where(oh0, before, 0.0), axis=1, keepdims=True)
    rank1 = jnp.sum(jnp.where(oh1, before, 0.0), axis=1, keepdims=True)
    cnt_scr[...] = cnt_scr[...] + jnp.sum(ohs, axis=0, keepdims=True)
    cnt_o[...] = cnt_scr[...]
    route_o[...] = jnp.where(
        lane == 0, (i0 - N_GROUPS).astype(_F32),
        jnp.where(lane == 1, (i1 - N_GROUPS).astype(_F32),
                  jnp.where(lane == 2, w0,
                            jnp.where(lane == 3, w1,
                                      jnp.where(lane == 4, rank0, jnp.where(lane == 5, rank1, 0.0))))))


def _merge(lid, x, y_a, y_b, y_c, w_gt, w_br, w_o, ln_g, ln_b, w_route, b_route):
    S = x.shape[0]
    tm = ROW_TILE
    tril = jnp.tril(jnp.ones((tm, tm), _F32), -1).astype(_MXU)
    br_w = SC_WIDTH + GLA_VW + MLA_VW
    return pl.pallas_call(
        _merge_kernel,
        name="merge_route",
        grid_spec=pltpu.PrefetchScalarGridSpec(
            num_scalar_prefetch=1,
            grid=(S // tm,),
            in_specs=[pl.BlockSpec((tm, D_MODEL), lambda i, l: (i, 0)),
                      pl.BlockSpec((tm, SC_WIDTH), lambda i, l: (i, 0)),
                      pl.BlockSpec((tm, GLA_VW), lambda i, l: (i, 0)),
                      pl.BlockSpec((tm, MLA_VW), lambda i, l: (i, 0)),
                      pl.BlockSpec((None, D_MODEL, 3 * D_MODEL), lambda i, l: (l[0], 0, 0)),
                      pl.BlockSpec((None, br_w, D_MODEL), lambda i, l: (l[0], 0, 0)),
                      pl.BlockSpec((None, D_MODEL, D_MODEL), lambda i, l: (l[0], 0, 0)),
                      pl.BlockSpec((None, 1, D_MODEL), lambda i, l: (l[0], 0, 0)),
                      pl.BlockSpec((None, 1, D_MODEL), lambda i, l: (l[0], 0, 0)),
                      pl.BlockSpec((None, D_MODEL, LANES), lambda i, l: (l[0], 0, 0)),
                      pl.BlockSpec((None, 1, LANES), lambda i, l: (l[0], 0, 0)),
                      pl.BlockSpec((tm, tm), lambda i, l: (0, 0))],
            out_specs=[pl.BlockSpec((tm, D_MODEL), lambda i, l: (i, 0)),
                       pl.BlockSpec((tm * TOKEN_ROWS, LANES), lambda i, l: (i, 0)),
                       pl.BlockSpec((tm, LANES), lambda i, l: (i, 0)),
                       pl.BlockSpec((SUBLANES, LANES), lambda i, l: (0, 0))],
            scratch_shapes=[pltpu.VMEM((SUBLANES, LANES), _F32)]),
        out_shape=[jax.ShapeDtypeStruct((S, D_MODEL), _F32),
                   jax.ShapeDtypeStruct((S * TOKEN_ROWS, LANES), _F32),
                   jax.ShapeDtypeStruct((S, LANES), _F32),
                   jax.ShapeDtypeStruct((SUBLANES, LANES), _F32)],
        compiler_params=_params(),
    )(lid, x, y_a, y_b, y_c, w_gt, w_br, w_o, ln_g, ln_b, w_route, b_route, tril)


def _dispatch_plan(route, counts_row, n_tokens):
    counts = counts_row[0, N_GROUPS:N_GROUPS + N_EXPERTS].astype(jnp.int32)
    padded = (counts + MOE_BM - 1) // MOE_BM * MOE_BM
    pad_end = jnp.cumsum(padded)
    pad_start = pad_end - padded
    expert_id = route[:, :TOP_K].astype(jnp.int32)
    rank = route[:, 4:4 + TOP_K].astype(jnp.int32)
    start_of = jnp.sum(jnp.where(expert_id[..., None] == jnp.arange(N_EXPERTS, dtype=jnp.int32),
                                 pad_start.astype(jnp.int32), 0), axis=-1)
    dest = (start_of + rank).reshape(n_tokens * TOP_K)
    nb = n_tokens * TOP_K // MOE_BM + N_EXPERTS
    block_row = jnp.arange(nb, dtype=jnp.int32) * MOE_BM
    block_expert = jnp.minimum(
        jnp.sum((pad_end[None, :] <= block_row[:, None]).astype(jnp.int32), axis=1),
        N_EXPERTS - 1).astype(jnp.int32)
    n_active = (pad_end[-1] // MOE_BM).astype(jnp.int32)
    first = jnp.concatenate([jnp.ones((1,), jnp.int32),
                             (block_expert[1:] != block_expert[:-1]).astype(jnp.int32)])
    meta = jnp.concatenate([n_active.reshape(1), jnp.zeros((1,), jnp.int32)])
    ids = jnp.arange(N_EXPERTS, dtype=jnp.int32)
    present = counts > 0
    later = jnp.where(present[None, :] & (ids[None, :] > ids[:, None]), ids[None, :], N_EXPERTS)
    next_of = jnp.min(later, axis=1)
    next_of = jnp.where(next_of == N_EXPERTS, -1, next_of).astype(jnp.int32)
    parity_of = ((jnp.cumsum(present.astype(jnp.int32)) - 1) % 2).astype(jnp.int32)
    onehot_be = (block_expert[:, None] == ids[None, :])
    next_expert = jnp.sum(jnp.where(onehot_be, next_of[None, :], 0), axis=1).astype(jnp.int32)
    slot_of = jnp.sum(jnp.where(onehot_be, parity_of[None, :], 0), axis=1).astype(jnp.int32)
    return dest, counts, pad_start.astype(jnp.int32), block_expert, first, next_expert, slot_of, meta


def _token_copy(src_hbm, src_row, dst_ref, dst_row, sem):
    return pltpu.make_async_copy(src_hbm.at[pl.ds(src_row * TOKEN_ROWS, TOKEN_ROWS), :],
                                 dst_ref.at[pl.ds(dst_row * TOKEN_ROWS, TOKEN_ROWS), :], sem)


def _dispatch_kernel(cnt, pstart, meta, dest_ref, x_ref, xs_hbm, zero_scr, sem):
    i = pl.program_id(0)
    tt = DISPATCH_TILE
    blk = MOE_BM * TOKEN_ROWS
    n_blocks = xs_hbm.shape[0] // blk

    def spare_blocks(fn):
        def one(b, c):
            fn(pltpu.make_async_copy(zero_scr, xs_hbm.at[pl.ds(b * blk, blk), :], sem.at[2]))
            return c

        lax.fori_loop(meta[0], n_blocks, one, 0)

    @pl.when(i == 0)
    def _():
        zero_scr[...] = jnp.zeros(zero_scr.shape, _F32)
        spare_blocks(lambda cp: cp.start())

    def start(j, c):
        for kk in range(TOP_K):
            _token_copy(x_ref, j, xs_hbm, dest_ref[0, 0, TOP_K * j + kk], sem.at[0]).start(priority=kk)
        return c

    lax.fori_loop(0, tt, start, 0, unroll=8)

    def pad_rows(fn):
        def per_expert(e, c):
            n = cnt[e]
            n_fill = (n + MOE_BM - 1) // MOE_BM * MOE_BM - n
            row = pstart[e] + n
            for bit in range(MOE_BM.bit_length() - 1):
                run = 1 << bit

                @pl.when((n_fill & run) != 0)
                def _():
                    first = row + (n_fill & (run - 1))
                    fn(pltpu.make_async_copy(zero_scr.at[pl.ds(0, run * TOKEN_ROWS), :],
                                             xs_hbm.at[pl.ds(first * TOKEN_ROWS, run * TOKEN_ROWS), :],
                                             sem.at[1]))
            return c

        lax.fori_loop(0, N_EXPERTS, per_expert, 0)

    @pl.when(i == 0)
    def _():
        pad_rows(lambda cp: cp.start())
        pad_rows(lambda cp: cp.wait())
        spare_blocks(lambda cp: cp.wait())

    def wait(j, c):
        for kk in range(TOP_K):
            _token_copy(x_ref, 0, xs_hbm, 0, sem.at[0]).wait()
        return c

    lax.fori_loop(0, tt, wait, 0, unroll=8)


def _dispatch(counts, pad_start, meta, dest, x_tok, n_rows):
    n_tokens = x_tok.shape[0] // TOKEN_ROWS
    tt = DISPATCH_TILE
    return pl.pallas_call(
        _dispatch_kernel,
        name="moe_dispatch",
        grid_spec=pltpu.PrefetchScalarGridSpec(
            num_scalar_prefetch=3,
            grid=(n_tokens // tt,),
            in_specs=[pl.BlockSpec((1, 1, TOP_K * tt), lambda i, c, s, m: (i, 0, 0),
                                   memory_space=pltpu.SMEM),
                      pl.BlockSpec((tt * TOKEN_ROWS, LANES), lambda i, c, s, m: (i, 0))],
            out_specs=pl.BlockSpec(memory_space=pl.ANY),
            scratch_shapes=[pltpu.VMEM((MOE_BM * TOKEN_ROWS, LANES), _F32),
                            pltpu.SemaphoreType.DMA((3,))]),
        out_shape=jax.ShapeDtypeStruct((n_rows * TOKEN_ROWS, LANES), _F32),
        compiler_params=_params(),
    )(counts, pad_start, meta, dest.reshape(n_tokens // tt, 1, TOP_K * tt), x_tok)


def _from_token_tiles(ref, n):
    return jnp.concatenate([ref[pl.ds(c, n, stride=TOKEN_ROWS), :] for c in range(TOKEN_ROWS)], axis=1)


def _expert_kernel(lid, bexp, first, nxt, slot_of, meta, xs_ref, wg_hbm, wu_hbm, wd_hbm, o_ref,
                   wg_f, wu_f, wd_f, wg_b, wu_b, wd_b, sem):
    b = pl.program_id(0)
    layer = lid[0]

    def weight_copies(e, s):
        return (pltpu.make_async_copy(wg_hbm.at[layer, e], wg_f.at[s], sem.at[s, 0]),
                pltpu.make_async_copy(wu_hbm.at[layer, e], wu_f.at[s], sem.at[s, 1]),
                pltpu.make_async_copy(wd_hbm.at[layer, e], wd_f.at[s], sem.at[s, 2]))

    @pl.when(b == 0)
    def _():
        for cp in weight_copies(bexp[0], 0):
            cp.start()

    @pl.when(b >= meta[0])
    def _():
        o_ref[...] = jnp.zeros(o_ref.shape, o_ref.dtype)

    @pl.when(b < meta[0])
    def _():
        @pl.when(first[b] == 1)
        def _():
            s = slot_of[b]
            for cp in weight_copies(bexp[b], s):
                cp.wait()
            wg_b[...] = wg_f[s].astype(_MXU)
            wu_b[...] = wu_f[s].astype(_MXU)
            wd_b[...] = wd_f[s].astype(_MXU)

            @pl.when(nxt[b] >= 0)
            def _():
                for cp in weight_copies(nxt[b], 1 - s):
                    cp.start()

        xb = _from_token_tiles(xs_ref, MOE_BM).astype(_MXU)
        hg = _dot(xb, wg_b[...])
        hu = _dot(xb, wu_b[...])
        hid = (hg * _sigmoid(hg)) * hu
        y = _dot(hid.astype(_MXU), wd_b[...])
        for c in range(TOKEN_ROWS):
            o_ref[pl.ds(c, MOE_BM, stride=TOKEN_ROWS), :] = y[:, c * LANES:(c + 1) * LANES]


def _experts(lid, block_expert, first, next_expert, slot_of, meta, xs_tok, w_gate, w_up, w_down):
    rows = MOE_BM * TOKEN_ROWS
    nb = xs_tok.shape[0] // rows
    live = lambda b, m: jnp.minimum(b, m[0] - 1)
    hbm = pl.BlockSpec(memory_space=pl.ANY)
    return pl.pallas_call(
        _expert_kernel,
        name="moe_experts",
        grid_spec=pltpu.PrefetchScalarGridSpec(
            num_scalar_prefetch=6,
            grid=(nb,),
            in_specs=[pl.BlockSpec((rows, LANES), lambda b, l, e, f, n, s, m: (live(b, m), 0)),
                      hbm, hbm, hbm],
            out_specs=pl.BlockSpec((rows, LANES), lambda b, l, e, f, n, s, m: (b, 0)),
            scratch_shapes=[pltpu.VMEM((2, D_MODEL, EXPERT_HIDDEN), _F32),
                            pltpu.VMEM((2, D_MODEL, EXPERT_HIDDEN), _F32),
                            pltpu.VMEM((2, EXPERT_HIDDEN, D_MODEL), _F32),
                            pltpu.VMEM((D_MODEL, EXPERT_HIDDEN), _MXU),
                            pltpu.VMEM((D_MODEL, EXPERT_HIDDEN), _MXU),
                            pltpu.VMEM((EXPERT_HIDDEN, D_MODEL), _MXU),
                            pltpu.SemaphoreType.DMA((2, 3))]),
        out_shape=jax.ShapeDtypeStruct(xs_tok.shape, _F32),
        compiler_params=_params(),
    )(lid, block_expert, first, next_expert, slot_of, meta, xs_tok, w_gate, w_up, w_down)


def _combine_kernel(lid, pos_ref, pos_next_ref, y_hbm, x_ref, route_ref, g2_ref, b2_ref, wpg_ref, bpg_ref,
                    p_ref, wpu_ref, g3_ref, b3_ref, o_ref, ybuf, sem):
    tc = x_ref.shape[0]
    i = pl.program_id(0)
    slot = lax.rem(i, 2)

    def gather(rows_ref, s, fn):
        def body(r, c):
            for kk in range(TOP_K):
                fn(_token_copy(y_hbm, rows_ref[0, 0, TOP_K * r + kk], ybuf.at[s, kk], r, sem.at[s]), kk)
            return c

        lax.fori_loop(0, tc, body, 0, unroll=8)

    start = lambda cp, kk: cp.start(priority=kk)

    @pl.when(i == 0)
    def _():
        gather(pos_ref, 0, start)

    @pl.when(i + 1 < pl.num_programs(0))
    def _():
        gather(pos_next_ref, 1 - slot, start)

    gather(pos_ref, slot, lambda cp, kk: cp.wait())
    route = route_ref[...]
    y = (route[:, 2:3] * _from_token_tiles(ybuf.at[slot, 0], tc)
         + route[:, 3:4] * _from_token_tiles(ybuf.at[slot, 1], tc))
    x2 = _layer_norm(DN_ALPHA * x_ref[...] + y, g2_ref[...], b2_ref[...])
    gate = _sigmoid(_dot(x2.astype(_MXU), wpg_ref[...]) + bpg_ref[...])
    up = _dot(p_ref[...].astype(_MXU), wpu_ref[...])
    o_ref[...] = _layer_norm(DN_ALPHA * x2 + gate * up, g3_ref[...], b3_ref[...])


def _combine(lid, dest, y_tok, x, route, ln2_g, ln2_b, w_pg, b_pg, p, w_pu, ln3_g, ln3_b):
    S = x.shape[0]
    tc = COMBINE_TILE
    n_tiles = S // tc
    pos = dest.reshape(n_tiles, 1, TOP_K * tc)
    vec = lambda n: pl.BlockSpec((None, 1, n), lambda i, l: (l[0], 0, 0))
    return pl.pallas_call(
        _combine_kernel,
        name="combine_ple",
        grid_spec=pltpu.PrefetchScalarGridSpec(
            num_scalar_prefetch=1,
            grid=(S // tc,),
            in_specs=[pl.BlockSpec((1, 1, TOP_K * tc), lambda i, l: (i, 0, 0), memory_space=pltpu.SMEM),
                      pl.BlockSpec((1, 1, TOP_K * tc), lambda i, l: (jnp.minimum(i + 1, n_tiles - 1), 0, 0),
                                   memory_space=pltpu.SMEM),
                      pl.BlockSpec(memory_space=pl.ANY),
                      pl.BlockSpec((tc, D_MODEL), lambda i, l: (i, 0)),
                      pl.BlockSpec((tc, LANES), lambda i, l: (i, 0)),
                      vec(D_MODEL), vec(D_MODEL),
                      pl.BlockSpec((None, D_MODEL, D_MODEL), lambda i, l: (l[0], 0, 0)),
                      vec(D_MODEL),
                      pl.BlockSpec((None, tc, PLE_DIM), lambda i, l: (l[0], i, 0)),
                      pl.BlockSpec((None, PLE_DIM, D_MODEL), lambda i, l: (l[0], 0, 0)),
                      vec(D_MODEL), vec(D_MODEL)],
            out_specs=pl.BlockSpec((tc, D_MODEL), lambda i, l: (i, 0)),
            scratch_shapes=[pltpu.VMEM((2, TOP_K, tc * TOKEN_ROWS, LANES), _F32),
                            pltpu.SemaphoreType.DMA((2,))]),
        out_shape=jax.ShapeDtypeStruct((S, D_MODEL), _F32),
        compiler_params=_params(),
    )(lid, pos, pos, y_tok, x, route, ln2_g, ln2_b, w_pg, b_pg, p, w_pu, ln3_g, ln3_b)


def _split_in_proj(w_in):
    sizes = (SC_WIDTH, SC_WIDTH, SC_WIDTH, GLA_QK, GLA_QK, GLA_VW, GLA_GATE_RANK, GLA_VW,
             MLA_Q_RANK, MLA_KV_RANK, MLA_ROPE, D_MODEL, D_MODEL, D_MODEL)
    offs = [0]
    for s in sizes:
        offs.append(offs[-1] + s)
    col = lambda j: w_in[:, :, offs[j]:offs[j + 1]]
    L = w_in.shape[0]
    zeros = lambda n: jnp.zeros((L, D_MODEL, n), w_in.dtype)
    w_conv_in = jnp.concatenate([col(0), col(1), col(2)], axis=-1)
    w_gla_in = jnp.concatenate([col(3), col(4), col(5), col(7), col(6), zeros(LANES - GLA_GATE_RANK)], axis=-1)
    w_mla_in = jnp.concatenate([col(8), col(9), col(10), zeros(LANES - MLA_ROPE)], axis=-1)
    w_gates = jnp.concatenate([col(11), col(12), col(13)], axis=-1)
    return [w.astype(_MXU) for w in (w_conv_in, w_gla_in, w_mla_in, w_gates)]


def kernel(x, p, positions, ln0_g, ln0_b, w_in, w_conv, w_gla_gate, b_gla_gate, gla_norm_g, mla_q_norm_g, mla_kv_norm_g, w_uq, w_ukv, w_br, w_o, ln1_g, ln1_b, w_grp, b_grp, w_exp, b_exp, w_gate, w_up, w_down, ln2_g, ln2_b, w_ple_gate, b_ple_gate, w_ple_up, ln3_g, ln3_b):
    B, S, D = x.shape
    L = w_in.shape[0]
    assert B == 1 and D == D_MODEL
    assert all(S % t == 0 for t in (ROW_TILE, ATTN_TQ, DISPATCH_TILE, COMBINE_TILE, MOE_BM))
    assert ATTN_TQ % (2 * ATTN_TK) == 0 and ATTN_TK % ATTN_ROWS == 0
    H = MLA_HEADS

    w_conv_in, w_gla_in, w_mla_in, w_gates = _split_in_proj(w_in)
    w_gg = jnp.pad(w_gla_gate, ((0, 0), (0, LANES - GLA_GATE_RANK), (0, 0))).astype(_MXU)
    w_uq_p = jnp.pad(w_uq.reshape(L, MLA_Q_RANK, H, MLA_QK),
                     ((0, 0), (0, 0), (0, 0), (0, MLA_QK_PAD - MLA_QK))
                     ).reshape(L, MLA_Q_RANK, H * MLA_QK_PAD).astype(_MXU)
    w_ukv4 = w_ukv.reshape(L, MLA_KV_RANK, H, MLA_NOPE + MLA_V)
    w_uk = w_ukv4[..., :MLA_NOPE].reshape(L, MLA_KV_RANK, H * MLA_NOPE).astype(_MXU)
    w_uv = w_ukv4[..., MLA_NOPE:].reshape(L, MLA_KV_RANK, H * MLA_V).astype(_MXU)
    w_route = jnp.concatenate(
        [w_grp, w_exp.reshape(L, D, N_EXPERTS),
         jnp.zeros((L, D, LANES - N_GROUPS - N_EXPERTS), _F32)], axis=-1).astype(_MXU)
    b_route = jnp.concatenate(
        [b_grp, b_exp.reshape(L, N_EXPERTS),
         jnp.zeros((L, LANES - N_GROUPS - N_EXPERTS), _F32)], axis=-1).reshape(L, 1, LANES)
    w_br_c = w_br.astype(_MXU)
    w_o_c = w_o.astype(_MXU)
    w_pg_c = w_ple_gate.astype(_MXU)
    w_pu_c = w_ple_up.astype(_MXU)
    row = lambda a: a.reshape(L, 1, a.shape[-1])

    cos_t, sin_t = _rope_tables(positions.reshape(S))
    x0 = _entry_norm(x.reshape(S, D), ln0_g, ln0_b)
    p3 = p.reshape(L, S, PLE_DIM)

    def layer(xc, i):
        lid = jnp.reshape(i, (1,)).astype(jnp.int32)
        y_a, y_b, q, k, v = _mixers(lid, xc, w_conv_in, w_conv, w_gla_in, w_gg, row(b_gla_gate),
                                    row(gla_norm_g), w_mla_in, row(mla_q_norm_g), row(mla_kv_norm_g),
                                    w_uq_p, w_uk, w_uv, cos_t, sin_t)
        y_c = _attention(q, k, v)
        x1, x1_tok, route, counts_row = _merge(lid, xc, y_a, y_b, y_c, w_gates, w_br_c, w_o_c,
                                               row(ln1_g), row(ln1_b), w_route, b_route)
        (dest, counts, pad_start, block_expert, first, next_expert, slot_of,
         meta) = _dispatch_plan(route, counts_row, S)
        n_rows = (S * TOP_K // MOE_BM + N_EXPERTS) * MOE_BM
        xs_tok = _dispatch(counts, pad_start, meta, dest, x1_tok, n_rows)
        y_tok = _experts(lid, block_expert, first, next_expert, slot_of, meta, xs_tok, w_gate, w_up, w_down)
        x3 = _combine(lid, dest, y_tok, x1, route, row(ln2_g), row(ln2_b), w_pg_c, row(b_ple_gate),
                      p3, w_pu_c, row(ln3_g), row(ln3_b))
        return x3, None

    out, _ = lax.scan(layer, x0, jnp.arange(L, dtype=jnp.int32))
    return out.reshape(B, S, D)
```

```python
import jax
import jax.numpy as jnp
from jax import lax
from jax.experimental import pallas as pl
from jax.experimental.pallas import tpu as pltpu

D_MODEL = 1024
DEPTH = 4
PLE_DIM = 256
SC_WIDTH = 512
SC_CONV = 3
GLA_HEADS = 4
GLA_DK = 64
GLA_DV = 128
GLA_GATE_RANK = 16
GLA_TAU = 16.0
GLA_CHUNK = 64
MLA_HEADS = 4
MLA_NOPE = 128
MLA_ROPE = 64
MLA_V = 128
MLA_Q_RANK = 256
MLA_KV_RANK = 128
ROPE_THETA = 10000.0
N_GROUPS = 8
EXPERTS_PER_GROUP = 8
N_EXPERTS = N_GROUPS * EXPERTS_PER_GROUP
TOP_K = 2
EXPERT_HIDDEN = 256
DN_ALPHA = (2 * DEPTH) ** 0.25
LN_EPS = 1e-5
RMS_EPS = 1e-6

GLA_QK = GLA_HEADS * GLA_DK
GLA_VW = GLA_HEADS * GLA_DV
MLA_QK = MLA_NOPE + MLA_ROPE
MLA_VW = MLA_HEADS * MLA_V

LANES = 128
SUBLANES = 8
MLA_QK_PAD = 2 * LANES
VMEM_LIMIT = 56 * 1024 * 1024

ROW_TILE = 1024
ATTN_TQ = 2048
ATTN_TK = 1024
ATTN_ROWS = 256
LOG2_E = 1.4426950408889634
MOE_BM = 256
COMBINE_TILE = 512
FUSED_TILE = 512
DISPATCH_TILE = 1024
TOKEN_ROWS = D_MODEL // LANES

_MXU = jnp.bfloat16
_F32 = jnp.float32


def _dot(a, b):
    return jnp.dot(a, b, preferred_element_type=_F32)


def _dot_nt(a, b):
    return lax.dot_general(a, b, (((1,), (1,)), ((), ())), preferred_element_type=_F32)


def _dot_tn(a, b):
    return lax.dot_general(a, b, (((0,), (0,)), ((), ())), preferred_element_type=_F32)


def _layer_norm(x, g, b):
    mu = jnp.mean(x, axis=-1, keepdims=True)
    xc = x - mu
    var = jnp.mean(xc * xc, axis=-1, keepdims=True)
    return xc * lax.rsqrt(var + LN_EPS) * g + b


def _rms_norm(x, g):
    return x * lax.rsqrt(jnp.mean(x * x, axis=-1, keepdims=True) + RMS_EPS) * g


def _sigmoid(x):
    return 1.0 / (1.0 + jnp.exp(-x))


def _params(n_axes=1):
    return pltpu.CompilerParams(dimension_semantics=("arbitrary",) * n_axes,
                                vmem_limit_bytes=VMEM_LIMIT)


def _ln0_kernel(x_ref, g_ref, b_ref, o_ref):
    o_ref[...] = _layer_norm(x_ref[...], g_ref[...], b_ref[...])


def _entry_norm(x, g, b):
    S = x.shape[0]
    return pl.pallas_call(
        _ln0_kernel,
        name="entry_ln",
        grid=(S // ROW_TILE,),
        in_specs=[pl.BlockSpec((ROW_TILE, D_MODEL), lambda i: (i, 0)),
                  pl.BlockSpec((1, D_MODEL), lambda i: (0, 0)),
                  pl.BlockSpec((1, D_MODEL), lambda i: (0, 0))],
        out_specs=pl.BlockSpec((ROW_TILE, D_MODEL), lambda i: (i, 0)),
        out_shape=jax.ShapeDtypeStruct((S, D_MODEL), _F32),
        compiler_params=_params(),
    )(x, g.reshape(1, D_MODEL), b.reshape(1, D_MODEL))


def _rope_kernel(pos_ref, inv_ref, c_ref, s_ref):
    ang = pos_ref[...].astype(_F32) * inv_ref[...]
    lane = lax.broadcasted_iota(jnp.int32, ang.shape, 1)
    half = MLA_ROPE // 2
    cosv = jnp.cos(ang)
    sinv = jnp.sin(ang)
    c_ref[...] = jnp.where(lane < MLA_ROPE, cosv, 0.0)
    s_ref[...] = jnp.where(lane < half, -sinv, jnp.where(lane < MLA_ROPE, sinv, 0.0))


def _rope_tables(positions):
    S = positions.shape[0]
    half = MLA_ROPE // 2
    inv = 1.0 / (ROPE_THETA ** (jnp.arange(0, MLA_ROPE, 2, dtype=_F32) / MLA_ROPE))
    inv = jnp.tile(inv, LANES // half).reshape(1, LANES)
    return pl.pallas_call(
        _rope_kernel,
        name="rope_tables",
        grid=(S // ROW_TILE,),
        in_specs=[pl.BlockSpec((ROW_TILE, 1), lambda i: (i, 0)),
                  pl.BlockSpec((1, LANES), lambda i: (0, 0))],
        out_specs=[pl.BlockSpec((ROW_TILE, LANES), lambda i: (i, 0)),
                   pl.BlockSpec((ROW_TILE, LANES), lambda i: (i, 0))],
        out_shape=[jax.ShapeDtypeStruct((S, LANES), _F32)] * 2,
        compiler_params=_params(),
    )(positions.reshape(S, 1), inv)


def _conv_compute(xb, w_ref, wc_ref, o_ref, u_scr):
    tm = xb.shape[0]
    h = _dot(xb, w_ref[...])
    a_b = h[:, :SC_WIDTH]
    u = h[:, SC_WIDTH:2 * SC_WIDTH] * h[:, 2 * SC_WIDTH:]
    pad = SUBLANES
    u_scr[pad:pad + tm, :] = u
    wc = wc_ref[...]
    y = (wc[2:3, :] * u + wc[1:2, :] * u_scr[pad - 1:pad - 1 + tm, :]
         + wc[0:1, :] * u_scr[pad - 2:pad - 2 + tm, :])
    o_ref[...] = (a_b * y).astype(o_ref.dtype)
    u_scr[0:pad, :] = u_scr[tm:tm + pad, :]


_GLA_IN = 2 * GLA_QK + 2 * GLA_VW + LANES


def _gla_project(xb, w_ref, wg_ref, bg_ref):
    C = GLA_CHUNK
    h = _dot(xb, w_ref[...])
    q = h[:, :GLA_QK]
    k = h[:, GLA_QK:2 * GLA_QK]
    v = h[:, 2 * GLA_QK:2 * GLA_QK + GLA_VW]
    r = h[:, 2 * GLA_QK + GLA_VW:2 * GLA_QK + 2 * GLA_VW]
    g_lr = h[:, 2 * GLA_QK + 2 * GLA_VW:]
    z = _dot(g_lr.astype(_MXU), wg_ref[...]) + bg_ref[...]
    log_a = (jnp.minimum(z, 0.0) - jnp.log1p(jnp.exp(-jnp.abs(z)))) * (1.0 / GLA_TAU)

    row_in_chunk = lax.broadcasted_iota(jnp.int32, log_a.shape, 0) & (C - 1)
    b = log_a
    d = 1
    while d < C:
        b = b + jnp.where(row_in_chunk >= d, pltpu.roll(b, d, 0), 0.0)
        d *= 2
    return q, k, v, r, b


def _gla_recur(q, k, v, r, b, ng_ref, o_ref, st_scr):
    tm = q.shape[0]
    C = GLA_CHUNK
    ri = lax.broadcasted_iota(jnp.int32, (C, C), 0)
    ci = lax.broadcasted_iota(jnp.int32, (C, C), 1)
    causal = ci <= ri
    scale = GLA_DK ** -0.5
    ng = ng_ref[...]
    for c in range(tm // C):
        rows = slice(c * C, (c + 1) * C)
        bc = b[rows]
        b_last = bc[C - 1:C, :]
        kc = k[rows]
        q_e = (q[rows] * scale) * jnp.exp(bc)
        k_e = kc * jnp.exp(-bc)
        k_t = kc * jnp.exp(b_last - bc)
        decay = jnp.exp(b_last)
        for hh in range(GLA_HEADS):
            ks = slice(hh * GLA_DK, (hh + 1) * GLA_DK)
            vs = slice(hh * GLA_DV, (hh + 1) * GLA_DV)
            qh = q_e[:, ks].astype(_MXU)
            vh = v[rows, vs].astype(_MXU)
            a = jnp.where(causal, _dot_nt(qh, k_e[:, ks].astype(_MXU)), 0.0)
            st = st_scr[hh]
            o = _dot(a.astype(_MXU), vh) + _dot_nt(qh, st.astype(_MXU))
            st_scr[hh] = st * decay[:, ks] + _dot_tn(vh, k_t[:, ks].astype(_MXU))
            rr = r[rows, vs]
            o_ref[rows, vs] = (_rms_norm(o, ng) * (rr * _sigmoid(rr))).astype(o_ref.dtype)


_MLA_IN = MLA_Q_RANK + MLA_KV_RANK + LANES


def _mla_compute(xb, w_ref, qg_ref, kvg_ref, wuq_ref, wk_ref, wv_ref, ct_ref, st_ref, q_o, k_o, v_o):
    h = _dot(xb, w_ref[...])
    c_q = _rms_norm(h[:, :MLA_Q_RANK], qg_ref[...]).astype(_MXU)
    c_kv = _rms_norm(h[:, MLA_Q_RANK:MLA_Q_RANK + MLA_KV_RANK], kvg_ref[...]).astype(_MXU)
    k_rope_raw = h[:, MLA_Q_RANK + MLA_KV_RANK:]
    q = _dot(c_q, wuq_ref[...])
    k_nope = _dot(c_kv, wk_ref[...])
    v = _dot(c_kv, wv_ref[...])
    cos_t = ct_ref[...]
    sin_t = st_ref[...]
    lane = lax.broadcasted_iota(jnp.int32, cos_t.shape, 1)
    half = MLA_ROPE // 2

    def rope(xr):
        rot = jnp.where(lane < half, pltpu.roll(xr, LANES - half, 1), pltpu.roll(xr, half, 1))
        return xr * cos_t + rot * sin_t

    k_rope = rope(k_rope_raw)
    one_col = jnp.where(lane == 0, 1.0, 0.0)
    scale = MLA_QK ** -0.5 * LOG2_E
    for hh in range(MLA_HEADS):
        base = hh * MLA_QK_PAD
        q_h = jnp.concatenate([q[:, base:base + MLA_NOPE], rope(q[:, base + MLA_NOPE:base + MLA_QK_PAD])],
                              axis=1)
        q_o[hh] = (q_h * scale).astype(q_o.dtype)
        k_o[hh] = jnp.concatenate([k_nope[:, hh * MLA_NOPE:(hh + 1) * MLA_NOPE], k_rope],
                                  axis=1).astype(k_o.dtype)
        v_o[hh] = jnp.concatenate([v[:, hh * MLA_V:(hh + 1) * MLA_V], one_col], axis=1).astype(v_o.dtype)


def _mixers_kernel(lid, x_ref,
                   wc_in, wc, wg_in, wg_gate, bg_gate, ng, wm_in, qg, kvg, wuq, wuk, wuv, cos_t, sin_t,
                   ya_o, yb_o, q_o, k_o, v_o, u_scr, st_scr):
    @pl.when(pl.program_id(0) == 0)
    def _():
        u_scr[0:SUBLANES, :] = jnp.zeros((SUBLANES, SC_WIDTH), _F32)
        st_scr[...] = jnp.zeros(st_scr.shape, _F32)

    _mixers_compute(x_ref[...].astype(_MXU), wc_in, wc, wg_in, wg_gate, bg_gate, ng, wm_in, qg, kvg,
                    wuq, wuk, wuv, cos_t, sin_t, ya_o, yb_o, q_o, k_o, v_o, u_scr, st_scr)


def _mixers_compute(xb, wc_in, wc, wg_in, wg_gate, bg_gate, ng, wm_in, qg, kvg, wuq, wuk, wuv, cos_t, sin_t,
                    ya_o, yb_o, q_o, k_o, v_o, u_scr, st_scr):
    gla = _gla_project(xb, wg_in, wg_gate, bg_gate)
    _conv_compute(xb, wc_in, wc, ya_o, u_scr)
    _mla_compute(xb, wm_in, qg, kvg, wuq, wuk, wuv, cos_t, sin_t, q_o, k_o, v_o)
    _gla_recur(*gla, ng, yb_o, st_scr)


def _layer_block(layer, *blk, **kw):
    return pl.BlockSpec((None,) + blk, lambda i, *pf: (layer(*pf),) + (0,) * len(blk), **kw)


def _mixer_in_specs(tm, layer, **kw):
    H = MLA_HEADS
    w = lambda *blk: _layer_block(layer, *blk, **kw)
    rows = pl.BlockSpec((tm, LANES), lambda i, *pf: (i, 0))
    return [w(D_MODEL, 3 * SC_WIDTH), w(SC_CONV, SC_WIDTH),
            w(D_MODEL, _GLA_IN), w(LANES, GLA_QK), w(1, GLA_QK), w(1, GLA_DV),
            w(D_MODEL, _MLA_IN), w(1, MLA_Q_RANK), w(1, MLA_KV_RANK),
            w(MLA_Q_RANK, H * MLA_QK_PAD), w(MLA_KV_RANK, H * MLA_NOPE), w(MLA_KV_RANK, H * MLA_V),
            rows, rows]


def _mixer_outs(S, tm):
    H = MLA_HEADS
    specs = [pl.BlockSpec((tm, SC_WIDTH), lambda i, *pf: (i, 0)),
             pl.BlockSpec((tm, GLA_VW), lambda i, *pf: (i, 0)),
             pl.BlockSpec((H, tm, MLA_QK_PAD), lambda i, *pf: (0, i, 0)),
             pl.BlockSpec((H, tm, MLA_QK_PAD), lambda i, *pf: (0, i, 0)),
             pl.BlockSpec((H, tm, MLA_V + LANES), lambda i, *pf: (0, i, 0))]
    shapes = [jax.ShapeDtypeStruct((S, SC_WIDTH), _MXU),
              jax.ShapeDtypeStruct((S, GLA_VW), _MXU),
              jax.ShapeDtypeStruct((H, S, MLA_QK_PAD), _MXU),
              jax.ShapeDtypeStruct((H, S, MLA_QK_PAD), _MXU),
              jax.ShapeDtypeStruct((H, S, MLA_V + LANES), _MXU)]
    scratch = [pltpu.VMEM((tm + SUBLANES, SC_WIDTH), _F32),
               pltpu.VMEM((GLA_HEADS, GLA_DV, GLA_DK), _F32)]
    return specs, shapes, scratch


def _mixers(lid, x, *mixer_operands):
    S = x.shape[0]
    tm = ROW_TILE
    out_specs, out_shapes, scratch = _mixer_outs(S, tm)
    return pl.pallas_call(
        _mixers_kernel,
        name="mixers",
        grid_spec=pltpu.PrefetchScalarGridSpec(
            num_scalar_prefetch=1,
            grid=(S // tm,),
            in_specs=[pl.BlockSpec((tm, D_MODEL), lambda i, l: (i, 0))]
            + _mixer_in_specs(tm, lambda l: l[0]),
            out_specs=out_specs,
            scratch_shapes=scratch),
        out_shape=out_shapes,
        compiler_params=_params(),
    )(lid, x, *mixer_operands)


def _attn_kernel(q_ref, k_ref, v_ref, o_ref, s_a, s_b, m_scr, acc_scr):
    tq = q_ref.shape[0]
    tk = ATTN_TK
    rb = ATTN_ROWS
    n_rb = tq // rb
    qi = pl.program_id(1)
    m_scr[...] = jnp.full(m_scr.shape, -jnp.inf, _F32)
    acc_scr[...] = jnp.zeros(acc_scr.shape, _F32)

    def scores(dst, start, first_rb=0):
        rows = slice(first_rb * rb, tq)
        dst[rows, :] = _dot_nt(q_ref[rows, :], k_ref[pl.ds(start, tk), :])

    def consume(src, start, col0=None):
        for r in range(n_rb):
            row0 = r * rb
            rows = slice(row0, row0 + rb)
            n_cols = tk
            if col0 is not None:
                n_cols = min(tk, row0 + rb - col0)
                if n_cols <= 0:
                    continue
            s = src[rows, :n_cols]
            if col0 is not None and col0 + n_cols - 1 > row0:
                ri = lax.broadcasted_iota(jnp.int32, s.shape, 0) + row0
                ci = lax.broadcasted_iota(jnp.int32, s.shape, 1) + col0
                s = jnp.where(ci <= ri, s, -jnp.inf)
            v = v_ref[pl.ds(start, n_cols), :]
            m_prev = m_scr[rows, :]
            m_new = jnp.maximum(m_prev, jnp.max(s, axis=1, keepdims=True))
            alpha = jnp.exp2(m_prev - m_new)
            p = jnp.exp2(s - m_new[:, :1]).astype(v.dtype)
            acc_scr[rows, :] = jnp.concatenate([alpha, alpha], axis=1) * acc_scr[rows, :] + _dot(p, v)
            m_scr[rows, :] = m_new

    col = lambda j: pl.multiple_of(j * tk, tk)
    n_diag = tq // tk
    scores(s_a, col(0))

    def body(i, carry):
        scores(s_b, col(2 * i + 1))
        consume(s_a, col(2 * i))
        scores(s_a, col(2 * i + 2))
        consume(s_b, col(2 * i + 1))
        return carry

    lax.fori_loop(0, qi * (n_diag // 2), body, 0)
    bufs = (s_a, s_b)
    for d in range(n_diag):
        if d + 1 < n_diag:
            scores(bufs[(d + 1) % 2], col(n_diag * qi + d + 1), first_rb=(d + 1) * tk // rb)
        consume(bufs[d % 2], col(n_diag * qi + d), col0=d * tk)
    o_ref[...] = (acc_scr[:, :MLA_V] / acc_scr[:, MLA_V:MLA_V + 1]).astype(o_ref.dtype)


def _attention(q, k, v):
    H, S, _ = q.shape
    tq = ATTN_TQ
    return pl.pallas_call(
        _attn_kernel,
        name="mla_attention",
        grid=(H, S // tq),
        in_specs=[pl.BlockSpec((None, tq, MLA_QK_PAD), lambda h, i: (h, i, 0)),
                  pl.BlockSpec((None, S, MLA_QK_PAD), lambda h, i: (h, 0, 0), pipeline_mode=pl.Buffered(1)),
                  pl.BlockSpec((None, S, MLA_V + LANES), lambda h, i: (h, 0, 0), pipeline_mode=pl.Buffered(1))],
        out_specs=pl.BlockSpec((tq, MLA_V), lambda h, i: (i, h)),
        out_shape=jax.ShapeDtypeStruct((S, MLA_VW), _MXU),
        scratch_shapes=[pltpu.VMEM((tq, ATTN_TK), _F32),
                        pltpu.VMEM((tq, ATTN_TK), _F32),
                        pltpu.VMEM((tq, LANES), _F32),
                        pltpu.VMEM((tq, MLA_V + LANES), _F32)],
        compiler_params=_params(2),
    )(q, k, v)


def _merge_kernel(lid, x_ref, ya_ref, yb_ref, yc_ref, wgt_ref, wbr_ref, wo_ref, g_ref, b_ref,
                  wr_ref, br_ref, tril_ref, x_o, xt_o, route_o, cnt_o, cnt_scr):
    tm = x_ref.shape[0]

    @pl.when(pl.program_id(0) == 0)
    def _():
        cnt_scr[...] = jnp.zeros(cnt_scr.shape, _F32)

    x = x_ref[...]
    gates = _dot(x.astype(_MXU), wgt_ref[...])
    wbr = wbr_ref
    merged = (_sigmoid(gates[:, :D_MODEL]) * _dot(ya_ref[...], wbr[0:SC_WIDTH, :])
              + _sigmoid(gates[:, D_MODEL:2 * D_MODEL])
              * _dot(yb_ref[...], wbr[SC_WIDTH:SC_WIDTH + GLA_VW, :])
              + _sigmoid(gates[:, 2 * D_MODEL:]) * _dot(yc_ref[...], wbr[SC_WIDTH + GLA_VW:, :]))
    x1 = _layer_norm(DN_ALPHA * x + _dot(merged.astype(_MXU), wo_ref[...]), g_ref[...], b_ref[...])
    x_o[...] = x1
    for c in range(TOKEN_ROWS):
        xt_o[pl.ds(c, tm, stride=TOKEN_ROWS), :] = x1[:, c * LANES:(c + 1) * LANES]

    logits = _dot(x1.astype(_MXU), wr_ref[...]) + br_ref[...]
    lane = lax.broadcasted_iota(jnp.int32, logits.shape, 1)
    neg = -jnp.inf
    gl = jnp.where(lane < N_GROUPS, logits, neg)
    g_max = jnp.max(gl, axis=1, keepdims=True)
    g_top = jnp.min(jnp.where(gl == g_max, lane, LANES), axis=1, keepdims=True)
    p_g = 1.0 / jnp.sum(jnp.where(lane < N_GROUPS, jnp.exp(logits - g_max), 0.0), axis=1, keepdims=True)
    lo = N_GROUPS + g_top * EXPERTS_PER_GROUP
    sl = jnp.where((lane >= lo) & (lane < lo + EXPERTS_PER_GROUP), logits, neg)
    v0 = jnp.max(sl, axis=1, keepdims=True)
    i0 = jnp.min(jnp.where(sl == v0, lane, LANES), axis=1, keepdims=True)
    sl = jnp.where(lane == i0, neg, sl)
    v1 = jnp.max(sl, axis=1, keepdims=True)
    i1 = jnp.min(jnp.where(sl == v1, lane, LANES), axis=1, keepdims=True)
    e1 = jnp.exp(v1 - v0)
    w0 = p_g / (1.0 + e1)
    w1 = p_g * e1 / (1.0 + e1)
    oh0 = lane == i0
    oh1 = lane == i1
    ohs = jnp.where(oh0, 1.0, jnp.where(oh1, 1.0, 0.0))
    before = _dot(tril_ref[...], ohs.astype(_MXU)) + cnt_scr[0:1, :]
    rank0 = jnp.sum(jnp.where(oh0, before, 0.0), axis=1, keepdims=True)
    rank1 = jnp.sum(jnp.where(oh1, before, 0.0), axis=1, keepdims=True)
    cnt_scr[...] = cnt_scr[...] + jnp.sum(ohs, axis=0, keepdims=True)
    cnt_o[...] = cnt_scr[...]
    route_o[...] = jnp.where(
        lane == 0, (i0 - N_GROUPS).astype(_F32),
        jnp.where(lane == 1, (i1 - N_GROUPS).astype(_F32),
                  jnp.where(lane == 2, w0,
                            jnp.where(lane == 3, w1,
                                      jnp.where(lane == 4, rank0, jnp.where(lane == 5, rank1, 0.0))))))


def _merge(lid, x, y_a, y_b, y_c, w_gt, w_br, w_o, ln_g, ln_b, w_route, b_route):
    S = x.shape[0]
    tm = ROW_TILE
    tril = jnp.tril(jnp.ones((tm, tm), _F32), -1).astype(_MXU)
    br_w = SC_WIDTH + GLA_VW + MLA_VW
    return pl.pallas_call(
        _merge_kernel,
        name="merge_route",
        grid_spec=pltpu.PrefetchScalarGridSpec(
            num_scalar_prefetch=1,
            grid=(S // tm,),
            in_specs=[pl.BlockSpec((tm, D_MODEL), lambda i, l: (i, 0)),
                      pl.BlockSpec((tm, SC_WIDTH), lambda i, l: (i, 0)),
                      pl.BlockSpec((tm, GLA_VW), lambda i, l: (i, 0)),
                      pl.BlockSpec((tm, MLA_VW), lambda i, l: (i, 0)),
                      pl.BlockSpec((None, D_MODEL, 3 * D_MODEL), lambda i, l: (l[0], 0, 0)),
                      pl.BlockSpec((None, br_w, D_MODEL), lambda i, l: (l[0], 0, 0)),
                      pl.BlockSpec((None, D_MODEL, D_MODEL), lambda i, l: (l[0], 0, 0)),
                      pl.BlockSpec((None, 1, D_MODEL), lambda i, l: (l[0], 0, 0)),
                      pl.BlockSpec((None, 1, D_MODEL), lambda i, l: (l[0], 0, 0)),
                      pl.BlockSpec((None, D_MODEL, LANES), lambda i, l: (l[0], 0, 0)),
                      pl.BlockSpec((None, 1, LANES), lambda i, l: (l[0], 0, 0)),
                      pl.BlockSpec((tm, tm), lambda i, l: (0, 0))],
            out_specs=[pl.BlockSpec((tm, D_MODEL), lambda i, l: (i, 0)),
                       pl.BlockSpec((tm * TOKEN_ROWS, LANES), lambda i, l: (i, 0)),
                       pl.BlockSpec((tm, LANES), lambda i, l: (i, 0)),
                       pl.BlockSpec((SUBLANES, LANES), lambda i, l: (0, 0))],
            scratch_shapes=[pltpu.VMEM((SUBLANES, LANES), _F32)]),
        out_shape=[jax.ShapeDtypeStruct((S, D_MODEL), _F32),
                   jax.ShapeDtypeStruct((S * TOKEN_ROWS, LANES), _F32),
                   jax.ShapeDtypeStruct((S, LANES), _F32),
                   jax.ShapeDtypeStruct((SUBLANES, LANES), _F32)],
        compiler_params=_params(),
    )(lid, x, y_a, y_b, y_c, w_gt, w_br, w_o, ln_g, ln_b, w_route, b_route, tril)


def _dispatch_plan(route, counts_row, n_tokens):
    counts = counts_row[0, N_GROUPS:N_GROUPS + N_EXPERTS].astype(jnp.int32)
    padded = (counts + MOE_BM - 1) // MOE_BM * MOE_BM
    pad_end = jnp.cumsum(padded)
    pad_start = pad_end - padded
    expert_id = route[:, :TOP_K].astype(jnp.int32)
    rank = route[:, 4:4 + TOP_K].astype(jnp.int32)
    start_of = jnp.sum(jnp.where(expert_id[..., None] == jnp.arange(N_EXPERTS, dtype=jnp.int32),
                                 pad_start.astype(jnp.int32), 0), axis=-1)
    dest = (start_of + rank).reshape(n_tokens * TOP_K)
    nb = n_tokens * TOP_K // MOE_BM + N_EXPERTS
    block_row = jnp.arange(nb, dtype=jnp.int32) * MOE_BM
    block_expert = jnp.minimum(
        jnp.sum((pad_end[None, :] <= block_row[:, None]).astype(jnp.int32), axis=1),
        N_EXPERTS - 1).astype(jnp.int32)
    n_active = (pad_end[-1] // MOE_BM).astype(jnp.int32)
    first = jnp.concatenate([jnp.ones((1,), jnp.int32),
                             (block_expert[1:] != block_expert[:-1]).astype(jnp.int32)])
    meta = jnp.concatenate([n_active.reshape(1), jnp.zeros((1,), jnp.int32)])
    ids = jnp.arange(N_EXPERTS, dtype=jnp.int32)
    present = counts > 0
    later = jnp.where(present[None, :] & (ids[None, :] > ids[:, None]), ids[None, :], N_EXPERTS)
    next_of = jnp.min(later, axis=1)
    next_of = jnp.where(next_of == N_EXPERTS, -1, next_of).astype(jnp.int32)
    parity_of = ((jnp.cumsum(present.astype(jnp.int32)) - 1) % 2).astype(jnp.int32)
    onehot_be = (block_expert[:, None] == ids[None, :])
    next_expert = jnp.sum(jnp.where(onehot_be, next_of[None, :], 0), axis=1).astype(jnp.int32)
    slot_of = jnp.sum(jnp.where(onehot_be, parity_of[None, :], 0), axis=1).astype(jnp.int32)
    return dest, counts, pad_start.astype(jnp.int32), block_expert, first, next_expert, slot_of, meta


def _token_copy(src_hbm, src_row, dst_ref, dst_row, sem):
    return pltpu.make_async_copy(src_hbm.at[pl.ds(src_row * TOKEN_ROWS, TOKEN_ROWS), :],
                                 dst_ref.at[pl.ds(dst_row * TOKEN_ROWS, TOKEN_ROWS), :], sem)


def _dispatch_kernel(cnt, pstart, meta, dest_ref, x_ref, xs_hbm, zero_scr, sem):
    i = pl.program_id(0)
    tt = DISPATCH_TILE
    blk = MOE_BM * TOKEN_ROWS
    n_blocks = xs_hbm.shape[0] // blk

    def spare_blocks(fn):
        def one(b, c):
            fn(pltpu.make_async_copy(zero_scr, xs_hbm.at[pl.ds(b * blk, blk), :], sem.at[2]))
            return c

        lax.fori_loop(meta[0], n_blocks, one, 0)

    @pl.when(i == 0)
    def _():
        zero_scr[...] = jnp.zeros(zero_scr.shape, _F32)
        spare_blocks(lambda cp: cp.start())

    def start(j, c):
        for kk in range(TOP_K):
            _token_copy(x_ref, j, xs_hbm, dest_ref[0, 0, TOP_K * j + kk], sem.at[0]).start(priority=kk)
        return c

    lax.fori_loop(0, tt, start, 0, unroll=8)

    def pad_rows(fn):
        def per_expert(e, c):
            n = cnt[e]
            n_fill = (n + MOE_BM - 1) // MOE_BM * MOE_BM - n
            row = pstart[e] + n
            for bit in range(MOE_BM.bit_length() - 1):
                run = 1 << bit

                @pl.when((n_fill & run) != 0)
                def _():
                    first = row + (n_fill & (run - 1))
                    fn(pltpu.make_async_copy(zero_scr.at[pl.ds(0, run * TOKEN_ROWS), :],
                                             xs_hbm.at[pl.ds(first * TOKEN_ROWS, run * TOKEN_ROWS), :],
                                             sem.at[1]))
            return c

        lax.fori_loop(0, N_EXPERTS, per_expert, 0)

    @pl.when(i == 0)
    def _():
        pad_rows(lambda cp: cp.start())
        pad_rows(lambda cp: cp.wait())
        spare_blocks(lambda cp: cp.wait())

    def wait(j, c):
        for kk in range(TOP_K):
            _token_copy(x_ref, 0, xs_hbm, 0, sem.at[0]).wait()
        return c

    lax.fori_loop(0, tt, wait, 0, unroll=8)


def _dispatch(counts, pad_start, meta, dest, x_tok, n_rows):
    n_tokens = x_tok.shape[0] // TOKEN_ROWS
    tt = DISPATCH_TILE
    return pl.pallas_call(
        _dispatch_kernel,
        name="moe_dispatch",
        grid_spec=pltpu.PrefetchScalarGridSpec(
            num_scalar_prefetch=3,
            grid=(n_tokens // tt,),
            in_specs=[pl.BlockSpec((1, 1, TOP_K * tt), lambda i, c, s, m: (i, 0, 0),
                                   memory_space=pltpu.SMEM),
                      pl.BlockSpec((tt * TOKEN_ROWS, LANES), lambda i, c, s, m: (i, 0))],
            out_specs=pl.BlockSpec(memory_space=pl.ANY),
            scratch_shapes=[pltpu.VMEM((MOE_BM * TOKEN_ROWS, LANES), _F32),
                            pltpu.SemaphoreType.DMA((3,))]),
        out_shape=jax.ShapeDtypeStruct((n_rows * TOKEN_ROWS, LANES), _F32),
        compiler_params=_params(),
    )(counts, pad_start, meta, dest.reshape(n_tokens // tt, 1, TOP_K * tt), x_tok)


def _from_token_tiles(ref, n):
    return jnp.concatenate([ref[pl.ds(c, n, stride=TOKEN_ROWS), :] for c in range(TOKEN_ROWS)], axis=1)


def _expert_kernel(lid, bexp, first, nxt, slot_of, meta, xs_ref, wg_hbm, wu_hbm, wd_hbm, o_ref,
                   wg_f, wu_f, wd_f, wg_b, wu_b, wd_b, sem):
    b = pl.program_id(0)
    layer = lid[0]

    def weight_copies(e, s):
        return (pltpu.make_async_copy(wg_hbm.at[layer, e], wg_f.at[s], sem.at[s, 0]),
                pltpu.make_async_copy(wu_hbm.at[layer, e], wu_f.at[s], sem.at[s, 1]),
                pltpu.make_async_copy(wd_hbm.at[layer, e], wd_f.at[s], sem.at[s, 2]))

    @pl.when(b == 0)
    def _():
        for cp in weight_copies(bexp[0], 0):
            cp.start()

    @pl.when(b >= meta[0])
    def _():
        o_ref[...] = jnp.zeros(o_ref.shape, o_ref.dtype)

    @pl.when(b < meta[0])
    def _():
        @pl.when(first[b] == 1)
        def _():
            s = slot_of[b]
            for cp in weight_copies(bexp[b], s):
                cp.wait()
            wg_b[...] = wg_f[s].astype(_MXU)
            wu_b[...] = wu_f[s].astype(_MXU)
            wd_b[...] = wd_f[s].astype(_MXU)

            @pl.when(nxt[b] >= 0)
            def _():
                for cp in weight_copies(nxt[b], 1 - s):
                    cp.start()

        xb = _from_token_tiles(xs_ref, MOE_BM).astype(_MXU)
        hg = _dot(xb, wg_b[...])
        hu = _dot(xb, wu_b[...])
        hid = (hg * _sigmoid(hg)) * hu
        y = _dot(hid.astype(_MXU), wd_b[...])
        for c in range(TOKEN_ROWS):
            o_ref[pl.ds(c, MOE_BM, stride=TOKEN_ROWS), :] = y[:, c * LANES:(c + 1) * LANES]


def _experts(lid, block_expert, first, next_expert, slot_of, meta, xs_tok, w_gate, w_up, w_down):
    rows = MOE_BM * TOKEN_ROWS
    nb = xs_tok.shape[0] // rows
    live = lambda b, m: jnp.minimum(b, m[0] - 1)
    hbm = pl.BlockSpec(memory_space=pl.ANY)
    return pl.pallas_call(
        _expert_kernel,
        name="moe_experts",
        grid_spec=pltpu.PrefetchScalarGridSpec(
            num_scalar_prefetch=6,
            grid=(nb,),
            in_specs=[pl.BlockSpec((rows, LANES), lambda b, l, e, f, n, s, m: (live(b, m), 0)),
                      hbm, hbm, hbm],
            out_specs=pl.BlockSpec((rows, LANES), lambda b, l, e, f, n, s, m: (b, 0)),
            scratch_shapes=[pltpu.VMEM((2, D_MODEL, EXPERT_HIDDEN), _F32),
                            pltpu.VMEM((2, D_MODEL, EXPERT_HIDDEN), _F32),
                            pltpu.VMEM((2, EXPERT_HIDDEN, D_MODEL), _F32),
                            pltpu.VMEM((D_MODEL, EXPERT_HIDDEN), _MXU),
                            pltpu.VMEM((D_MODEL, EXPERT_HIDDEN), _MXU),
                            pltpu.VMEM((EXPERT_HIDDEN, D_MODEL), _MXU),
                            pltpu.SemaphoreType.DMA((2, 3))]),
        out_shape=jax.ShapeDtypeStruct(xs_tok.shape, _F32),
        compiler_params=_params(),
    )(lid, block_expert, first, next_expert, slot_of, meta, xs_tok, w_gate, w_up, w_down)


def _combine_rows(pos_ref, pos_next_ref, y_hbm, x_ref, route_ref, g2_ref, b2_ref, wpg_ref, bpg_ref,
                  p_ref, wpu_ref, g3_ref, b3_ref, ybuf, sem):
    tc = x_ref.shape[0]
    i = pl.program_id(0)
    slot = lax.rem(i, 2)

    def gather(rows_ref, s, fn):
        def body(r, c):
            for kk in range(TOP_K):
                fn(_token_copy(y_hbm, rows_ref[0, 0, TOP_K * r + kk], ybuf.at[s, kk], r, sem.at[s]), kk)
            return c

        lax.fori_loop(0, tc, body, 0, unroll=8)

    start = lambda cp, kk: cp.start(priority=kk)

    @pl.when(i == 0)
    def _():
        gather(pos_ref, 0, start)

    @pl.when(i + 1 < pl.num_programs(0))
    def _():
        gather(pos_next_ref, 1 - slot, start)

    gather(pos_ref, slot, lambda cp, kk: cp.wait())
    route = route_ref[...]
    y = (route[:, 2:3] * _from_token_tiles(ybuf.at[slot, 0], tc)
         + route[:, 3:4] * _from_token_tiles(ybuf.at[slot, 1], tc))
    x2 = _layer_norm(DN_ALPHA * x_ref[...] + y, g2_ref[...], b2_ref[...])
    gate = _sigmoid(_dot(x2.astype(_MXU), wpg_ref[...]) + bpg_ref[...])
    up = _dot(p_ref[...].astype(_MXU), wpu_ref[...])
    return _layer_norm(DN_ALPHA * x2 + gate * up, g3_ref[...], b3_ref[...])


def _combine_kernel(lid, *refs):
    *ins, o_ref, ybuf, sem = refs
    o_ref[...] = _combine_rows(*ins, ybuf, sem)


def _combine_mixers_kernel(lid, lid_next, *refs):
    combine_in, (x3_o, *mixer_refs) = refs[:13], refs[13 + 14:]
    mixer_in = refs[13:13 + 14]
    *mixer_out, ybuf, sem, u_scr, st_scr = mixer_refs

    @pl.when(pl.program_id(0) == 0)
    def _():
        u_scr[0:SUBLANES, :] = jnp.zeros((SUBLANES, SC_WIDTH), _F32)
        st_scr[...] = jnp.zeros(st_scr.shape, _F32)

    x3 = _combine_rows(*combine_in, ybuf, sem)
    x3_o[...] = x3
    _mixers_compute(x3.astype(_MXU), *mixer_in, *mixer_out, u_scr, st_scr)


def _combine_in_specs(tc, n_tiles, layer, **kw):
    w = lambda *blk: _layer_block(layer, *blk, **kw)
    return [pl.BlockSpec((1, 1, TOP_K * tc), lambda i, *pf: (i, 0, 0), memory_space=pltpu.SMEM),
            pl.BlockSpec((1, 1, TOP_K * tc), lambda i, *pf: (jnp.minimum(i + 1, n_tiles - 1), 0, 0),
                         memory_space=pltpu.SMEM),
            pl.BlockSpec(memory_space=pl.ANY),
            pl.BlockSpec((tc, D_MODEL), lambda i, *pf: (i, 0)),
            pl.BlockSpec((tc, LANES), lambda i, *pf: (i, 0)),
            w(1, D_MODEL), w(1, D_MODEL), w(D_MODEL, D_MODEL), w(1, D_MODEL),
            pl.BlockSpec((None, tc, PLE_DIM), lambda i, *pf: (layer(*pf), i, 0)),
            w(PLE_DIM, D_MODEL), w(1, D_MODEL), w(1, D_MODEL)]


def _combine_scratch(tc):
    return [pltpu.VMEM((2, TOP_K, tc * TOKEN_ROWS, LANES), _F32), pltpu.SemaphoreType.DMA((2,))]


def _combine(lid, dest, y_tok, x, *combine_operands):
    S = x.shape[0]
    tc = COMBINE_TILE
    n_tiles = S // tc
    pos = dest.reshape(n_tiles, 1, TOP_K * tc)
    return pl.pallas_call(
        _combine_kernel,
        name="combine_ple",
        grid_spec=pltpu.PrefetchScalarGridSpec(
            num_scalar_prefetch=1,
            grid=(n_tiles,),
            in_specs=_combine_in_specs(tc, n_tiles, lambda l: l[0]),
            out_specs=pl.BlockSpec((tc, D_MODEL), lambda i, l: (i, 0)),
            scratch_shapes=_combine_scratch(tc)),
        out_shape=jax.ShapeDtypeStruct((S, D_MODEL), _F32),
        compiler_params=_params(),
    )(lid, pos, pos, y_tok, x, *combine_operands)


def _combine_mixers(lid, lid_next, dest, y_tok, x, combine_operands, mixer_operands):
    S = x.shape[0]
    tm = FUSED_TILE
    n_tiles = S // tm
    pos = dest.reshape(n_tiles, 1, TOP_K * tm)
    once = dict(pipeline_mode=pl.Buffered(1))
    out_specs, out_shapes, scratch = _mixer_outs(S, tm)
    return pl.pallas_call(
        _combine_mixers_kernel,
        name="combine_mixers",
        grid_spec=pltpu.PrefetchScalarGridSpec(
            num_scalar_prefetch=2,
            grid=(n_tiles,),
            in_specs=_combine_in_specs(tm, n_tiles, lambda l, n: l[0], **once)
            + _mixer_in_specs(tm, lambda l, n: n[0], **once),
            out_specs=[pl.BlockSpec((tm, D_MODEL), lambda i, l, n: (i, 0))] + out_specs,
            scratch_shapes=_combine_scratch(tm) + scratch),
        out_shape=[jax.ShapeDtypeStruct((S, D_MODEL), _F32)] + out_shapes,
        compiler_params=_params(),
    )(lid, lid_next, pos, pos, y_tok, x, *combine_operands, *mixer_operands)


def _split_in_proj(w_in):
    sizes = (SC_WIDTH, SC_WIDTH, SC_WIDTH, GLA_QK, GLA_QK, GLA_VW, GLA_GATE_RANK, GLA_VW,
             MLA_Q_RANK, MLA_KV_RANK, MLA_ROPE, D_MODEL, D_MODEL, D_MODEL)
    offs = [0]
    for s in sizes:
        offs.append(offs[-1] + s)
    col = lambda j: w_in[:, :, offs[j]:offs[j + 1]]
    L = w_in.shape[0]
    zeros = lambda n: jnp.zeros((L, D_MODEL, n), w_in.dtype)
    w_conv_in = jnp.concatenate([col(0), col(1), col(2)], axis=-1)
    w_gla_in = jnp.concatenate([col(3), col(4), col(5), col(7), col(6), zeros(LANES - GLA_GATE_RANK)], axis=-1)
    w_mla_in = jnp.concatenate([col(8), col(9), col(10), zeros(LANES - MLA_ROPE)], axis=-1)
    w_gates = jnp.concatenate([col(11), col(12), col(13)], axis=-1)
    return [w.astype(_MXU) for w in (w_conv_in, w_gla_in, w_mla_in, w_gates)]


def kernel(x, p, positions, ln0_g, ln0_b, w_in, w_conv, w_gla_gate, b_gla_gate, gla_norm_g, mla_q_norm_g, mla_kv_norm_g, w_uq, w_ukv, w_br, w_o, ln1_g, ln1_b, w_grp, b_grp, w_exp, b_exp, w_gate, w_up, w_down, ln2_g, ln2_b, w_ple_gate, b_ple_gate, w_ple_up, ln3_g, ln3_b):
    B, S, D = x.shape
    L = w_in.shape[0]
    assert B == 1 and D == D_MODEL
    assert all(S % t == 0 for t in (ROW_TILE, ATTN_TQ, DISPATCH_TILE, COMBINE_TILE, FUSED_TILE, MOE_BM))
    assert ATTN_TQ % (2 * ATTN_TK) == 0 and ATTN_TK % ATTN_ROWS == 0
    H = MLA_HEADS

    w_conv_in, w_gla_in, w_mla_in, w_gates = _split_in_proj(w_in)
    w_gg = jnp.pad(w_gla_gate, ((0, 0), (0, LANES - GLA_GATE_RANK), (0, 0))).astype(_MXU)
    w_uq_p = jnp.pad(w_uq.reshape(L, MLA_Q_RANK, H, MLA_QK),
                     ((0, 0), (0, 0), (0, 0), (0, MLA_QK_PAD - MLA_QK))
                     ).reshape(L, MLA_Q_RANK, H * MLA_QK_PAD).astype(_MXU)
    w_ukv4 = w_ukv.reshape(L, MLA_KV_RANK, H, MLA_NOPE + MLA_V)
    w_uk = w_ukv4[..., :MLA_NOPE].reshape(L, MLA_KV_RANK, H * MLA_NOPE).astype(_MXU)
    w_uv = w_ukv4[..., MLA_NOPE:].reshape(L, MLA_KV_RANK, H * MLA_V).astype(_MXU)
    w_route = jnp.concatenate(
        [w_grp, w_exp.reshape(L, D, N_EXPERTS),
         jnp.zeros((L, D, LANES - N_GROUPS - N_EXPERTS), _F32)], axis=-1).astype(_MXU)
    b_route = jnp.concatenate(
        [b_grp, b_exp.reshape(L, N_EXPERTS),
         jnp.zeros((L, LANES - N_GROUPS - N_EXPERTS), _F32)], axis=-1).reshape(L, 1, LANES)
    w_br_c = w_br.astype(_MXU)
    w_o_c = w_o.astype(_MXU)
    w_pg_c = w_ple_gate.astype(_MXU)
    w_pu_c = w_ple_up.astype(_MXU)
    row = lambda a: a.reshape(L, 1, a.shape[-1])

    cos_t, sin_t = _rope_tables(positions.reshape(S))
    x0 = _entry_norm(x.reshape(S, D), ln0_g, ln0_b)
    p3 = p.reshape(L, S, PLE_DIM)

    mixer_operands = (w_conv_in, w_conv, w_gla_in, w_gg, row(b_gla_gate), row(gla_norm_g),
                      w_mla_in, row(mla_q_norm_g), row(mla_kv_norm_g), w_uq_p, w_uk, w_uv, cos_t, sin_t)

    def combine_operands(route):
        return (route, row(ln2_g), row(ln2_b), w_pg_c, row(b_ple_gate), p3, w_pu_c, row(ln3_g), row(ln3_b))

    def token_mix_and_experts(lid, xc, y_a, y_b, q, k, v):
        y_c = _attention(q, k, v)
        x1, x1_tok, route, counts_row = _merge(lid, xc, y_a, y_b, y_c, w_gates, w_br_c, w_o_c,
                                               row(ln1_g), row(ln1_b), w_route, b_route)
        (dest, counts, pad_start, block_expert, first, next_expert, slot_of,
         meta) = _dispatch_plan(route, counts_row, S)
        n_rows = (S * TOP_K // MOE_BM + N_EXPERTS) * MOE_BM
        xs_tok = _dispatch(counts, pad_start, meta, dest, x1_tok, n_rows)
        y_tok = _experts(lid, block_expert, first, next_expert, slot_of, meta, xs_tok, w_gate, w_up, w_down)
        return x1, route, dest, y_tok

    def layer(carry, i):
        lid = jnp.reshape(i, (1,)).astype(jnp.int32)
        x1, route, dest, y_tok = token_mix_and_experts(lid, *carry)
        return tuple(_combine_mixers(lid, lid + 1, dest, y_tok, x1, combine_operands(route),
                                     mixer_operands)), None

    first_lid = jnp.zeros((1,), jnp.int32)
    carry = (x0, *_mixers(first_lid, x0, *mixer_operands))
    carry, _ = lax.scan(layer, carry, jnp.arange(L - 1, dtype=jnp.int32))
    last_lid = jnp.full((1,), L - 1, jnp.int32)
    x1, route, dest, y_tok = token_mix_and_experts(last_lid, *carry)
    out = _combine(last_lid, dest, y_tok, x1, *combine_operands(route))
    return out.reshape(B, S, D)
```

```python
import jax
import jax.numpy as jnp
from jax import lax
from jax.experimental import pallas as pl
from jax.experimental.pallas import tpu as pltpu

D_MODEL = 1024
DEPTH = 4
PLE_DIM = 256
SC_WIDTH = 512
SC_CONV = 3
GLA_HEADS = 4
GLA_DK = 64
GLA_DV = 128
GLA_GATE_RANK = 16
GLA_TAU = 16.0
GLA_CHUNK = 64
MLA_HEADS = 4
MLA_NOPE = 128
MLA_ROPE = 64
MLA_V = 128
MLA_Q_RANK = 256
MLA_KV_RANK = 128
ROPE_THETA = 10000.0
N_GROUPS = 8
EXPERTS_PER_GROUP = 8
N_EXPERTS = N_GROUPS * EXPERTS_PER_GROUP
TOP_K = 2
EXPERT_HIDDEN = 256
DN_ALPHA = (2 * DEPTH) ** 0.25
LN_EPS = 1e-5
RMS_EPS = 1e-6

GLA_QK = GLA_HEADS * GLA_DK
GLA_VW = GLA_HEADS * GLA_DV
MLA_QK = MLA_NOPE + MLA_ROPE
MLA_VW = MLA_HEADS * MLA_V

LANES = 128
SUBLANES = 8
MLA_QK_PAD = 2 * LANES
VMEM_LIMIT = 56 * 1024 * 1024

ROW_TILE = 1024
ATTN_TQ = 2048
ATTN_TK = 1024
ATTN_ROWS = 256
LOG2_E = 1.4426950408889634
MOE_BM = 128
COMBINE_TILE = 512
DISPATCH_TILE = 1024
TOKEN_ROWS = D_MODEL // LANES

_MXU = jnp.bfloat16
_F32 = jnp.float32


def _dot(a, b):
    return jnp.dot(a, b, preferred_element_type=_F32)


def _dot_nt(a, b):
    return lax.dot_general(a, b, (((1,), (1,)), ((), ())), preferred_element_type=_F32)


def _dot_tn(a, b):
    return lax.dot_general(a, b, (((0,), (0,)), ((), ())), preferred_element_type=_F32)


def _layer_norm(x, g, b):
    mu = jnp.mean(x, axis=-1, keepdims=True)
    xc = x - mu
    var = jnp.mean(xc * xc, axis=-1, keepdims=True)
    return xc * lax.rsqrt(var + LN_EPS) * g + b


def _rms_norm(x, g):
    return x * lax.rsqrt(jnp.mean(x * x, axis=-1, keepdims=True) + RMS_EPS) * g


def _sigmoid(x):
    return 1.0 / (1.0 + jnp.exp(-x))


def _params(n_axes=1):
    return pltpu.CompilerParams(dimension_semantics=("arbitrary",) * n_axes,
                                vmem_limit_bytes=VMEM_LIMIT)


def _ln0_kernel(x_ref, g_ref, b_ref, o_ref):
    o_ref[...] = _layer_norm(x_ref[...], g_ref[...], b_ref[...])


def _entry_norm(x, g, b):
    S = x.shape[0]
    return pl.pallas_call(
        _ln0_kernel,
        name="entry_ln",
        grid=(S // ROW_TILE,),
        in_specs=[pl.BlockSpec((ROW_TILE, D_MODEL), lambda i: (i, 0)),
                  pl.BlockSpec((1, D_MODEL), lambda i: (0, 0)),
                  pl.BlockSpec((1, D_MODEL), lambda i: (0, 0))],
        out_specs=pl.BlockSpec((ROW_TILE, D_MODEL), lambda i: (i, 0)),
        out_shape=jax.ShapeDtypeStruct((S, D_MODEL), _F32),
        compiler_params=_params(),
    )(x, g.reshape(1, D_MODEL), b.reshape(1, D_MODEL))


def _rope_kernel(pos_ref, inv_ref, c_ref, s_ref):
    ang = pos_ref[...].astype(_F32) * inv_ref[...]
    lane = lax.broadcasted_iota(jnp.int32, ang.shape, 1)
    half = MLA_ROPE // 2
    cosv = jnp.cos(ang)
    sinv = jnp.sin(ang)
    c_ref[...] = jnp.where(lane < MLA_ROPE, cosv, 0.0)
    s_ref[...] = jnp.where(lane < half, -sinv, jnp.where(lane < MLA_ROPE, sinv, 0.0))


def _rope_tables(positions):
    S = positions.shape[0]
    half = MLA_ROPE // 2
    inv = 1.0 / (ROPE_THETA ** (jnp.arange(0, MLA_ROPE, 2, dtype=_F32) / MLA_ROPE))
    inv = jnp.tile(inv, LANES // half).reshape(1, LANES)
    return pl.pallas_call(
        _rope_kernel,
        name="rope_tables",
        grid=(S // ROW_TILE,),
        in_specs=[pl.BlockSpec((ROW_TILE, 1), lambda i: (i, 0)),
                  pl.BlockSpec((1, LANES), lambda i: (0, 0))],
        out_specs=[pl.BlockSpec((ROW_TILE, LANES), lambda i: (i, 0)),
                   pl.BlockSpec((ROW_TILE, LANES), lambda i: (i, 0))],
        out_shape=[jax.ShapeDtypeStruct((S, LANES), _F32)] * 2,
        compiler_params=_params(),
    )(positions.reshape(S, 1), inv)


def _conv_compute(xb, w_ref, wc_ref, o_ref, u_scr):
    tm = xb.shape[0]
    h = _dot(xb, w_ref[...])
    a_b = h[:, :SC_WIDTH]
    u = h[:, SC_WIDTH:2 * SC_WIDTH] * h[:, 2 * SC_WIDTH:]
    pad = SUBLANES
    u_scr[pad:pad + tm, :] = u
    wc = wc_ref[...]
    y = (wc[2:3, :] * u + wc[1:2, :] * u_scr[pad - 1:pad - 1 + tm, :]
         + wc[0:1, :] * u_scr[pad - 2:pad - 2 + tm, :])
    o_ref[...] = (a_b * y).astype(o_ref.dtype)
    u_scr[0:pad, :] = u_scr[tm:tm + pad, :]


_GLA_IN = 2 * GLA_QK + 2 * GLA_VW + LANES


def _gla_project(xb, w_ref, wg_ref, bg_ref):
    C = GLA_CHUNK
    h = _dot(xb, w_ref[...])
    q = h[:, :GLA_QK]
    k = h[:, GLA_QK:2 * GLA_QK]
    v = h[:, 2 * GLA_QK:2 * GLA_QK + GLA_VW]
    r = h[:, 2 * GLA_QK + GLA_VW:2 * GLA_QK + 2 * GLA_VW]
    g_lr = h[:, 2 * GLA_QK + 2 * GLA_VW:]
    z = _dot(g_lr.astype(_MXU), wg_ref[...]) + bg_ref[...]
    log_a = (jnp.minimum(z, 0.0) - jnp.log1p(jnp.exp(-jnp.abs(z)))) * (1.0 / GLA_TAU)

    row_in_chunk = lax.broadcasted_iota(jnp.int32, log_a.shape, 0) & (C - 1)
    b = log_a
    d = 1
    while d < C:
        b = b + jnp.where(row_in_chunk >= d, pltpu.roll(b, d, 0), 0.0)
        d *= 2
    return q, k, v, r, b


def _gla_recur(q, k, v, r, b, ng_ref, o_ref, st_scr):
    tm = q.shape[0]
    C = GLA_CHUNK
    ri = lax.broadcasted_iota(jnp.int32, (C, C), 0)
    ci = lax.broadcasted_iota(jnp.int32, (C, C), 1)
    causal = ci <= ri
    scale = GLA_DK ** -0.5
    ng = ng_ref[...]
    for c in range(tm // C):
        rows = slice(c * C, (c + 1) * C)
        bc = b[rows]
        b_last = bc[C - 1:C, :]
        kc = k[rows]
        q_e = (q[rows] * scale) * jnp.exp(bc)
        k_e = kc * jnp.exp(-bc)
        k_t = kc * jnp.exp(b_last - bc)
        decay = jnp.exp(b_last)
        for hh in range(GLA_HEADS):
            ks = slice(hh * GLA_DK, (hh + 1) * GLA_DK)
            vs = slice(hh * GLA_DV, (hh + 1) * GLA_DV)
            qh = q_e[:, ks].astype(_MXU)
            vh = v[rows, vs].astype(_MXU)
            a = jnp.where(causal, _dot_nt(qh, k_e[:, ks].astype(_MXU)), 0.0)
            st = st_scr[hh]
            o = _dot(a.astype(_MXU), vh) + _dot_nt(qh, st.astype(_MXU))
            st_scr[hh] = st * decay[:, ks] + _dot_tn(vh, k_t[:, ks].astype(_MXU))
            rr = r[rows, vs]
            o_ref[rows, vs] = (_rms_norm(o, ng) * (rr * _sigmoid(rr))).astype(o_ref.dtype)


_MLA_IN = MLA_Q_RANK + MLA_KV_RANK + LANES


def _mla_compute(xb, w_ref, qg_ref, kvg_ref, wuq_ref, wk_ref, wv_ref, ct_ref, st_ref, q_o, k_o, v_o):
    h = _dot(xb, w_ref[...])
    c_q = _rms_norm(h[:, :MLA_Q_RANK], qg_ref[...]).astype(_MXU)
    c_kv = _rms_norm(h[:, MLA_Q_RANK:MLA_Q_RANK + MLA_KV_RANK], kvg_ref[...]).astype(_MXU)
    k_rope_raw = h[:, MLA_Q_RANK + MLA_KV_RANK:]
    q = _dot(c_q, wuq_ref[...])
    k_nope = _dot(c_kv, wk_ref[...])
    v = _dot(c_kv, wv_ref[...])
    cos_t = ct_ref[...]
    sin_t = st_ref[...]
    lane = lax.broadcasted_iota(jnp.int32, cos_t.shape, 1)
    half = MLA_ROPE // 2

    def rope(xr):
        rot = jnp.where(lane < half, pltpu.roll(xr, LANES - half, 1), pltpu.roll(xr, half, 1))
        return xr * cos_t + rot * sin_t

    k_rope = rope(k_rope_raw)
    one_col = jnp.where(lane == 0, 1.0, 0.0)
    scale = MLA_QK ** -0.5 * LOG2_E
    for hh in range(MLA_HEADS):
        base = hh * MLA_QK_PAD
        q_h = jnp.concatenate([q[:, base:base + MLA_NOPE], rope(q[:, base + MLA_NOPE:base + MLA_QK_PAD])],
                              axis=1)
        q_o[hh] = (q_h * scale).astype(q_o.dtype)
        k_o[hh] = jnp.concatenate([k_nope[:, hh * MLA_NOPE:(hh + 1) * MLA_NOPE], k_rope],
                                  axis=1).astype(k_o.dtype)
        v_o[hh] = jnp.concatenate([v[:, hh * MLA_V:(hh + 1) * MLA_V], one_col], axis=1).astype(v_o.dtype)


def _mixers_kernel(lid, x_ref,
                   wc_in, wc, wg_in, wg_gate, bg_gate, ng, wm_in, qg, kvg, wuq, wuk, wuv, cos_t, sin_t,
                   ya_o, yb_o, q_o, k_o, v_o, u_scr, st_scr):
    @pl.when(pl.program_id(0) == 0)
    def _():
        u_scr[0:SUBLANES, :] = jnp.zeros((SUBLANES, SC_WIDTH), _F32)
        st_scr[...] = jnp.zeros(st_scr.shape, _F32)

    xb = x_ref[...].astype(_MXU)
    gla = _gla_project(xb, wg_in, wg_gate, bg_gate)
    _conv_compute(xb, wc_in, wc, ya_o, u_scr)
    _mla_compute(xb, wm_in, qg, kvg, wuq, wuk, wuv, cos_t, sin_t, q_o, k_o, v_o)
    _gla_recur(*gla, ng, yb_o, st_scr)


def _mixers(lid, x, w_conv_in, w_conv, w_gla_in, w_gate, b_gate, norm_g,
            w_mla_in, q_norm_g, kv_norm_g, w_uq, w_uk, w_uv, cos_t, sin_t):
    S = x.shape[0]
    tm = ROW_TILE
    H = MLA_HEADS
    return pl.pallas_call(
        _mixers_kernel,
        name="mixers",
        grid_spec=pltpu.PrefetchScalarGridSpec(
            num_scalar_prefetch=1,
            grid=(S // tm,),
            in_specs=[pl.BlockSpec((tm, D_MODEL), lambda i, l: (i, 0)),
                      pl.BlockSpec((None, D_MODEL, 3 * SC_WIDTH), lambda i, l: (l[0], 0, 0)),
                      pl.BlockSpec((None, SC_CONV, SC_WIDTH), lambda i, l: (l[0], 0, 0)),
                      pl.BlockSpec((None, D_MODEL, _GLA_IN), lambda i, l: (l[0], 0, 0)),
                      pl.BlockSpec((None, LANES, GLA_QK), lambda i, l: (l[0], 0, 0)),
                      pl.BlockSpec((None, 1, GLA_QK), lambda i, l: (l[0], 0, 0)),
                      pl.BlockSpec((None, 1, GLA_DV), lambda i, l: (l[0], 0, 0)),
                      pl.BlockSpec((None, D_MODEL, _MLA_IN), lambda i, l: (l[0], 0, 0)),
                      pl.BlockSpec((None, 1, MLA_Q_RANK), lambda i, l: (l[0], 0, 0)),
                      pl.BlockSpec((None, 1, MLA_KV_RANK), lambda i, l: (l[0], 0, 0)),
                      pl.BlockSpec((None, MLA_Q_RANK, H * MLA_QK_PAD), lambda i, l: (l[0], 0, 0)),
                      pl.BlockSpec((None, MLA_KV_RANK, H * MLA_NOPE), lambda i, l: (l[0], 0, 0)),
                      pl.BlockSpec((None, MLA_KV_RANK, H * MLA_V), lambda i, l: (l[0], 0, 0)),
                      pl.BlockSpec((tm, LANES), lambda i, l: (i, 0)),
                      pl.BlockSpec((tm, LANES), lambda i, l: (i, 0))],
            out_specs=[pl.BlockSpec((tm, SC_WIDTH), lambda i, l: (i, 0)),
                       pl.BlockSpec((tm, GLA_VW), lambda i, l: (i, 0)),
                       pl.BlockSpec((H, tm, MLA_QK_PAD), lambda i, l: (0, i, 0)),
                       pl.BlockSpec((H, tm, MLA_QK_PAD), lambda i, l: (0, i, 0)),
                       pl.BlockSpec((H, tm, MLA_V + LANES), lambda i, l: (0, i, 0))],
            scratch_shapes=[pltpu.VMEM((tm + SUBLANES, SC_WIDTH), _F32),
                            pltpu.VMEM((GLA_HEADS, GLA_DV, GLA_DK), _F32)]),
        out_shape=[jax.ShapeDtypeStruct((S, SC_WIDTH), _MXU),
                   jax.ShapeDtypeStruct((S, GLA_VW), _MXU),
                   jax.ShapeDtypeStruct((H, S, MLA_QK_PAD), _MXU),
                   jax.ShapeDtypeStruct((H, S, MLA_QK_PAD), _MXU),
                   jax.ShapeDtypeStruct((H, S, MLA_V + LANES), _MXU)],
        compiler_params=_params(),
    )(lid, x, w_conv_in, w_conv, w_gla_in, w_gate, b_gate, norm_g,
      w_mla_in, q_norm_g, kv_norm_g, w_uq, w_uk, w_uv, cos_t, sin_t)


def _attn_kernel(q_ref, k_ref, v_ref, o_ref, s_a, s_b, m_scr, acc_scr):
    tq = q_ref.shape[0]
    tk = ATTN_TK
    rb = ATTN_ROWS
    n_rb = tq // rb
    qi = pl.program_id(1)
    m_scr[...] = jnp.full(m_scr.shape, -jnp.inf, _F32)
    acc_scr[...] = jnp.zeros(acc_scr.shape, _F32)

    def scores(dst, start, first_rb=0):
        rows = slice(first_rb * rb, tq)
        dst[rows, :] = _dot_nt(q_ref[rows, :], k_ref[pl.ds(start, tk), :])

    def consume(src, start, col0=None):
        for r in range(n_rb):
            row0 = r * rb
            rows = slice(row0, row0 + rb)
            n_cols = tk
            if col0 is not None:
                n_cols = min(tk, row0 + rb - col0)
                if n_cols <= 0:
                    continue
            s = src[rows, :n_cols]
            if col0 is not None and col0 + n_cols - 1 > row0:
                ri = lax.broadcasted_iota(jnp.int32, s.shape, 0) + row0
                ci = lax.broadcasted_iota(jnp.int32, s.shape, 1) + col0
                s = jnp.where(ci <= ri, s, -jnp.inf)
            v = v_ref[pl.ds(start, n_cols), :]
            m_prev = m_scr[rows, :]
            m_new = jnp.maximum(m_prev, jnp.max(s, axis=1, keepdims=True))
            alpha = jnp.exp2(m_prev - m_new)
            p = jnp.exp2(s - m_new[:, :1]).astype(v.dtype)
            acc_scr[rows, :] = jnp.concatenate([alpha, alpha], axis=1) * acc_scr[rows, :] + _dot(p, v)
            m_scr[rows, :] = m_new

    col = lambda j: pl.multiple_of(j * tk, tk)
    n_diag = tq // tk
    scores(s_a, col(0))

    def body(i, carry):
        scores(s_b, col(2 * i + 1))
        consume(s_a, col(2 * i))
        scores(s_a, col(2 * i + 2))
        consume(s_b, col(2 * i + 1))
        return carry

    lax.fori_loop(0, qi * (n_diag // 2), body, 0)
    bufs = (s_a, s_b)
    for d in range(n_diag):
        if d + 1 < n_diag:
            scores(bufs[(d + 1) % 2], col(n_diag * qi + d + 1), first_rb=(d + 1) * tk // rb)
        consume(bufs[d % 2], col(n_diag * qi + d), col0=d * tk)
    o_ref[...] = (acc_scr[:, :MLA_V] / acc_scr[:, MLA_V:MLA_V + 1]).astype(o_ref.dtype)


def _attention(q, k, v):
    H, S, _ = q.shape
    tq = ATTN_TQ
    return pl.pallas_call(
        _attn_kernel,
        name="mla_attention",
        grid=(H, S // tq),
        in_specs=[pl.BlockSpec((None, tq, MLA_QK_PAD), lambda h, i: (h, i, 0)),
                  pl.BlockSpec((None, S, MLA_QK_PAD), lambda h, i: (h, 0, 0), pipeline_mode=pl.Buffered(1)),
                  pl.BlockSpec((None, S, MLA_V + LANES), lambda h, i: (h, 0, 0), pipeline_mode=pl.Buffered(1))],
        out_specs=pl.BlockSpec((tq, MLA_V), lambda h, i: (i, h)),
        out_shape=jax.ShapeDtypeStruct((S, MLA_VW), _MXU),
        scratch_shapes=[pltpu.VMEM((tq, ATTN_TK), _F32),
                        pltpu.VMEM((tq, ATTN_TK), _F32),
                        pltpu.VMEM((tq, LANES), _F32),
                        pltpu.VMEM((tq, MLA_V + LANES), _F32)],
        compiler_params=_params(2),
    )(q, k, v)


def _merge_kernel(lid, x_ref, ya_ref, yb_ref, yc_ref, wgt_ref, wbr_ref, wo_ref, g_ref, b_ref,
                  wr_ref, br_ref, tril_ref, x_o, xt_o, route_o, cnt_o, cnt_scr):
    tm = x_ref.shape[0]

    @pl.when(pl.program_id(0) == 0)
    def _():
        cnt_scr[...] = jnp.zeros(cnt_scr.shape, _F32)

    x = x_ref[...]
    gates = _dot(x.astype(_MXU), wgt_ref[...])
    wbr = wbr_ref
    merged = (_sigmoid(gates[:, :D_MODEL]) * _dot(ya_ref[...], wbr[0:SC_WIDTH, :])
              + _sigmoid(gates[:, D_MODEL:2 * D_MODEL])
              * _dot(yb_ref[...], wbr[SC_WIDTH:SC_WIDTH + GLA_VW, :])
              + _sigmoid(gates[:, 2 * D_MODEL:]) * _dot(yc_ref[...], wbr[SC_WIDTH + GLA_VW:, :]))
    x1 = _layer_norm(DN_ALPHA * x + _dot(merged.astype(_MXU), wo_ref[...]), g_ref[...], b_ref[...])
    x_o[...] = x1
    for c in range(TOKEN_ROWS):
        xt_o[pl.ds(c, tm, stride=TOKEN_ROWS), :] = x1[:, c * LANES:(c + 1) * LANES]

    logits = _dot(x1.astype(_MXU), wr_ref[...]) + br_ref[...]
    lane = lax.broadcasted_iota(jnp.int32, logits.shape, 1)
    neg = -jnp.inf
    gl = jnp.where(lane < N_GROUPS, logits, neg)
    g_max = jnp.max(gl, axis=1, keepdims=True)
    g_top = jnp.min(jnp.where(gl == g_max, lane, LANES), axis=1, keepdims=True)
    p_g = 1.0 / jnp.sum(jnp.where(lane < N_GROUPS, jnp.exp(logits - g_max), 0.0), axis=1, keepdims=True)
    lo = N_GROUPS + g_top * EXPERTS_PER_GROUP
    sl = jnp.where((lane >= lo) & (lane < lo + EXPERTS_PER_GROUP), logits, neg)
    v0 = jnp.max(sl, axis=1, keepdims=True)
    i0 = jnp.min(jnp.where(sl == v0, lane, LANES), axis=1, keepdims=True)
    sl = jnp.where(lane == i0, neg, sl)
    v1 = jnp.max(sl, axis=1, keepdims=True)
    i1 = jnp.min(jnp.where(sl == v1, lane, LANES), axis=1, keepdims=True)
    e1 = jnp.exp(v1 - v0)
    w0 = p_g / (1.0 + e1)
    w1 = p_g * e1 / (1.0 + e1)
    oh0 = lane == i0
    oh1 = lane == i1
    ohs = jnp.where(oh0, 1.0, jnp.where(oh1, 1.0, 0.0))
    before = _dot(tril_ref[...], ohs.astype(_MXU)) + cnt_scr[0:1, :]
    rank0 = jnp.sum(jnp.where(oh0, before, 0.0), axis=1, keepdims=True)
    rank1 = jnp.sum(jnp.where(oh1, before, 0.0), axis=1, keepdims=True)
    cnt_scr[...] = cnt_scr[...] + jnp.sum(ohs, axis=0, keepdims=True)
    cnt_o[...] = cnt_scr[...]
    route_o[...] = jnp.where(
        lane == 0, (i0 - N_GROUPS).astype(_F32),
        jnp.where(lane == 1, (i1 - N_GROUPS).astype(_F32),
                  jnp.where(lane == 2, w0,
                            jnp.where(lane == 3, w1,
                                      jnp.where(lane == 4, rank0, jnp.where(lane == 5, rank1, 0.0))))))


def _merge(lid, x, y_a, y_b, y_c, w_gt, w_br, w_o, ln_g, ln_b, w_route, b_route):
    S = x.shape[0]
    tm = ROW_TILE
    tril = jnp.tril(jnp.ones((tm, tm), _F32), -1).astype(_MXU)
    br_w = SC_WIDTH + GLA_VW + MLA_VW
    return pl.pallas_call(
        _merge_kernel,
        name="merge_route",
        grid_spec=pltpu.PrefetchScalarGridSpec(
            num_scalar_prefetch=1,
            grid=(S // tm,),
            in_specs=[pl.BlockSpec((tm, D_MODEL), lambda i, l: (i, 0)),
                      pl.BlockSpec((tm, SC_WIDTH), lambda i, l: (i, 0)),
                      pl.BlockSpec((tm, GLA_VW), lambda i, l: (i, 0)),
                      pl.BlockSpec((tm, MLA_VW), lambda i, l: (i, 0)),
                      pl.BlockSpec((None, D_MODEL, 3 * D_MODEL), lambda i, l: (l[0], 0, 0)),
                      pl.BlockSpec((None, br_w, D_MODEL), lambda i, l: (l[0], 0, 0)),
                      pl.BlockSpec((None, D_MODEL, D_MODEL), lambda i, l: (l[0], 0, 0)),
                      pl.BlockSpec((None, 1, D_MODEL), lambda i, l: (l[0], 0, 0)),
                      pl.BlockSpec((None, 1, D_MODEL), lambda i, l: (l[0], 0, 0)),
                      pl.BlockSpec((None, D_MODEL, LANES), lambda i, l: (l[0], 0, 0)),
                      pl.BlockSpec((None, 1, LANES), lambda i, l: (l[0], 0, 0)),
                      pl.BlockSpec((tm, tm), lambda i, l: (0, 0))],
            out_specs=[pl.BlockSpec((tm, D_MODEL), lambda i, l: (i, 0)),
                       pl.BlockSpec((tm * TOKEN_ROWS, LANES), lambda i, l: (i, 0)),
                       pl.BlockSpec((tm, LANES), lambda i, l: (i, 0)),
                       pl.BlockSpec((SUBLANES, LANES), lambda i, l: (0, 0))],
            scratch_shapes=[pltpu.VMEM((SUBLANES, LANES), _F32)]),
        out_shape=[jax.ShapeDtypeStruct((S, D_MODEL), _F32),
                   jax.ShapeDtypeStruct((S * TOKEN_ROWS, LANES), _F32),
                   jax.ShapeDtypeStruct((S, LANES), _F32),
                   jax.ShapeDtypeStruct((SUBLANES, LANES), _F32)],
        compiler_params=_params(),
    )(lid, x, y_a, y_b, y_c, w_gt, w_br, w_o, ln_g, ln_b, w_route, b_route, tril)


def _dispatch_plan(route, counts_row, n_tokens):
    counts = counts_row[0, N_GROUPS:N_GROUPS + N_EXPERTS].astype(jnp.int32)
    padded = (counts + MOE_BM - 1) // MOE_BM * MOE_BM
    pad_end = jnp.cumsum(padded)
    pad_start = pad_end - padded
    expert_id = route[:, :TOP_K].astype(jnp.int32)
    rank = route[:, 4:4 + TOP_K].astype(jnp.int32)
    start_of = jnp.sum(jnp.where(expert_id[..., None] == jnp.arange(N_EXPERTS, dtype=jnp.int32),
                                 pad_start.astype(jnp.int32), 0), axis=-1)
    dest = (start_of + rank).reshape(n_tokens * TOP_K)
    nb = n_tokens * TOP_K // MOE_BM + N_EXPERTS
    block_row = jnp.arange(nb, dtype=jnp.int32) * MOE_BM
    block_expert = jnp.minimum(
        jnp.sum((pad_end[None, :] <= block_row[:, None]).astype(jnp.int32), axis=1),
        N_EXPERTS - 1).astype(jnp.int32)
    n_active = (pad_end[-1] // MOE_BM).astype(jnp.int32)
    first = jnp.concatenate([jnp.ones((1,), jnp.int32),
                             (block_expert[1:] != block_expert[:-1]).astype(jnp.int32)])
    meta = jnp.concatenate([n_active.reshape(1), jnp.zeros((1,), jnp.int32)])
    ids = jnp.arange(N_EXPERTS, dtype=jnp.int32)
    present = counts > 0
    later = jnp.where(present[None, :] & (ids[None, :] > ids[:, None]), ids[None, :], N_EXPERTS)
    next_of = jnp.min(later, axis=1)
    next_of = jnp.where(next_of == N_EXPERTS, -1, next_of).astype(jnp.int32)
    parity_of = ((jnp.cumsum(present.astype(jnp.int32)) - 1) % 2).astype(jnp.int32)
    onehot_be = (block_expert[:, None] == ids[None, :])
    next_expert = jnp.sum(jnp.where(onehot_be, next_of[None, :], 0), axis=1).astype(jnp.int32)
    slot_of = jnp.sum(jnp.where(onehot_be, parity_of[None, :], 0), axis=1).astype(jnp.int32)
    return dest, counts, pad_start.astype(jnp.int32), block_expert, first, next_expert, slot_of, meta


def _token_copy(src_hbm, src_row, dst_ref, dst_row, sem):
    return pltpu.make_async_copy(src_hbm.at[pl.ds(src_row * TOKEN_ROWS, TOKEN_ROWS), :],
                                 dst_ref.at[pl.ds(dst_row * TOKEN_ROWS, TOKEN_ROWS), :], sem)


def _dispatch_kernel(cnt, pstart, meta, dest_ref, x_ref, xs_hbm, zero_scr, sem):
    i = pl.program_id(0)
    tt = DISPATCH_TILE
    blk = MOE_BM * TOKEN_ROWS
    n_blocks = xs_hbm.shape[0] // blk

    def spare_blocks(fn):
        def one(b, c):
            fn(pltpu.make_async_copy(zero_scr, xs_hbm.at[pl.ds(b * blk, blk), :], sem.at[2]))
            return c

        lax.fori_loop(meta[0], n_blocks, one, 0)

    @pl.when(i == 0)
    def _():
        zero_scr[...] = jnp.zeros(zero_scr.shape, _F32)
        spare_blocks(lambda cp: cp.start())

    def start(j, c):
        for kk in range(TOP_K):
            _token_copy(x_ref, j, xs_hbm, dest_ref[0, 0, TOP_K * j + kk], sem.at[0]).start(priority=kk)
        return c

    lax.fori_loop(0, tt, start, 0, unroll=8)

    def pad_rows(fn):
        def per_expert(e, c):
            n = cnt[e]
            n_fill = (n + MOE_BM - 1) // MOE_BM * MOE_BM - n
            row = pstart[e] + n
            for bit in range(MOE_BM.bit_length() - 1):
                run = 1 << bit

                @pl.when((n_fill & run) != 0)
                def _():
                    first = row + (n_fill & (run - 1))
                    fn(pltpu.make_async_copy(zero_scr.at[pl.ds(0, run * TOKEN_ROWS), :],
                                             xs_hbm.at[pl.ds(first * TOKEN_ROWS, run * TOKEN_ROWS), :],
                                             sem.at[1]))
            return c

        lax.fori_loop(0, N_EXPERTS, per_expert, 0)

    @pl.when(i == 0)
    def _():
        pad_rows(lambda cp: cp.start())
        pad_rows(lambda cp: cp.wait())
        spare_blocks(lambda cp: cp.wait())

    def wait(j, c):
        for kk in range(TOP_K):
            _token_copy(x_ref, 0, xs_hbm, 0, sem.at[0]).wait()
        return c

    lax.fori_loop(0, tt, wait, 0, unroll=8)


def _dispatch(counts, pad_start, meta, dest, x_tok, n_rows):
    n_tokens = x_tok.shape[0] // TOKEN_ROWS
    tt = DISPATCH_TILE
    return pl.pallas_call(
        _dispatch_kernel,
        name="moe_dispatch",
        grid_spec=pltpu.PrefetchScalarGridSpec(
            num_scalar_prefetch=3,
            grid=(n_tokens // tt,),
            in_specs=[pl.BlockSpec((1, 1, TOP_K * tt), lambda i, c, s, m: (i, 0, 0),
                                   memory_space=pltpu.SMEM),
                      pl.BlockSpec((tt * TOKEN_ROWS, LANES), lambda i, c, s, m: (i, 0))],
            out_specs=pl.BlockSpec(memory_space=pl.ANY),
            scratch_shapes=[pltpu.VMEM((MOE_BM * TOKEN_ROWS, LANES), _F32),
                            pltpu.SemaphoreType.DMA((3,))]),
        out_shape=jax.ShapeDtypeStruct((n_rows * TOKEN_ROWS, LANES), _F32),
        compiler_params=_params(),
    )(counts, pad_start, meta, dest.reshape(n_tokens // tt, 1, TOP_K * tt), x_tok)


def _from_token_tiles(ref, n):
    return jnp.concatenate([ref[pl.ds(c, n, stride=TOKEN_ROWS), :] for c in range(TOKEN_ROWS)], axis=1)


def _expert_kernel(lid, bexp, first, nxt, slot_of, meta, xs_ref, wg_hbm, wu_hbm, wd_hbm, o_ref,
                   wg_f, wu_f, wd_f, wg_b, wu_b, wd_b, sem):
    b = pl.program_id(0)
    layer = lid[0]

    def weight_copies(e, s):
        return (pltpu.make_async_copy(wg_hbm.at[layer, e], wg_f.at[s], sem.at[s, 0]),
                pltpu.make_async_copy(wu_hbm.at[layer, e], wu_f.at[s], sem.at[s, 1]),
                pltpu.make_async_copy(wd_hbm.at[layer, e], wd_f.at[s], sem.at[s, 2]))

    @pl.when(b == 0)
    def _():
        for cp in weight_copies(bexp[0], 0):
            cp.start()

    @pl.when(b >= meta[0])
    def _():
        o_ref[...] = jnp.zeros(o_ref.shape, o_ref.dtype)

    @pl.when(b < meta[0])
    def _():
        @pl.when(first[b] == 1)
        def _():
            s = slot_of[b]
            for cp in weight_copies(bexp[b], s):
                cp.wait()
            wg_b[...] = wg_f[s].astype(_MXU)
            wu_b[...] = wu_f[s].astype(_MXU)
            wd_b[...] = wd_f[s].astype(_MXU)

            @pl.when(nxt[b] >= 0)
            def _():
                for cp in weight_copies(nxt[b], 1 - s):
                    cp.start()

        xb = _from_token_tiles(xs_ref, MOE_BM).astype(_MXU)
        hg = _dot(xb, wg_b[...])
        hu = _dot(xb, wu_b[...])
        hid = (hg * _sigmoid(hg)) * hu
        y = _dot(hid.astype(_MXU), wd_b[...])
        for c in range(TOKEN_ROWS):
            o_ref[pl.ds(c, MOE_BM, stride=TOKEN_ROWS), :] = y[:, c * LANES:(c + 1) * LANES]


def _experts(lid, block_expert, first, next_expert, slot_of, meta, xs_tok, w_gate, w_up, w_down):
    rows = MOE_BM * TOKEN_ROWS
    nb = xs_tok.shape[0] // rows
    live = lambda b, m: jnp.minimum(b, m[0] - 1)
    hbm = pl.BlockSpec(memory_space=pl.ANY)
    return pl.pallas_call(
        _expert_kernel,
        name="moe_experts",
        grid_spec=pltpu.PrefetchScalarGridSpec(
            num_scalar_prefetch=6,
            grid=(nb,),
            in_specs=[pl.BlockSpec((rows, LANES), lambda b, l, e, f, n, s, m: (live(b, m), 0)),
                      hbm, hbm, hbm],
            out_specs=pl.BlockSpec((rows, LANES), lambda b, l, e, f, n, s, m: (b, 0)),
            scratch_shapes=[pltpu.VMEM((2, D_MODEL, EXPERT_HIDDEN), _F32),
                            pltpu.VMEM((2, D_MODEL, EXPERT_HIDDEN), _F32),
                            pltpu.VMEM((2, EXPERT_HIDDEN, D_MODEL), _F32),
                            pltpu.VMEM((D_MODEL, EXPERT_HIDDEN), _MXU),
                            pltpu.VMEM((D_MODEL, EXPERT_HIDDEN), _MXU),
                            pltpu.VMEM((EXPERT_HIDDEN, D_MODEL), _MXU),
                            pltpu.SemaphoreType.DMA((2, 3))]),
        out_shape=jax.ShapeDtypeStruct(xs_tok.shape, _F32),
        compiler_params=_params(),
    )(lid, block_expert, first, next_expert, slot_of, meta, xs_tok, w_gate, w_up, w_down)


def _combine_kernel(lid, pos_ref, pos_next_ref, y_hbm, x_ref, route_ref, g2_ref, b2_ref, wpg_ref, bpg_ref,
                    p_ref, wpu_ref, g3_ref, b3_ref, o_ref, ybuf, sem):
    tc = x_ref.shape[0]
    i = pl.program_id(0)
    slot = lax.rem(i, 2)

    def gather(rows_ref, s, fn):
        def body(r, c):
            for kk in range(TOP_K):
                fn(_token_copy(y_hbm, rows_ref[0, 0, TOP_K * r + kk], ybuf.at[s, kk], r, sem.at[s]), kk)
            return c

        lax.fori_loop(0, tc, body, 0, unroll=8)

    start = lambda cp, kk: cp.start(priority=kk)

    @pl.when(i == 0)
    def _():
        gather(pos_ref, 0, start)

    @pl.when(i + 1 < pl.num_programs(0))
    def _():
        gather(pos_next_ref, 1 - slot, start)

    gather(pos_ref, slot, lambda cp, kk: cp.wait())
    route = route_ref[...]
    y = (route[:, 2:3] * _from_token_tiles(ybuf.at[slot, 0], tc)
         + route[:, 3:4] * _from_token_tiles(ybuf.at[slot, 1], tc))
    x2 = _layer_norm(DN_ALPHA * x_ref[...] + y, g2_ref[...], b2_ref[...])
    gate = _sigmoid(_dot(x2.astype(_MXU), wpg_ref[...]) + bpg_ref[...])
    up = _dot(p_ref[...].astype(_MXU), wpu_ref[...])
    o_ref[...] = _layer_norm(DN_ALPHA * x2 + gate * up, g3_ref[...], b3_ref[...])


def _combine(lid, dest, y_tok, x, route, ln2_g, ln2_b, w_pg, b_pg, p, w_pu, ln3_g, ln3_b):
    S = x.shape[0]
    tc = COMBINE_TILE
    n_tiles = S // tc
    pos = dest.reshape(n_tiles, 1, TOP_K * tc)
    vec = lambda n: pl.BlockSpec((None, 1, n), lambda i, l: (l[0], 0, 0))
    return pl.pallas_call(
        _combine_kernel,
        name="combine_ple",
        grid_spec=pltpu.PrefetchScalarGridSpec(
            num_scalar_prefetch=1,
            grid=(S // tc,),
            in_specs=[pl.BlockSpec((1, 1, TOP_K * tc), lambda i, l: (i, 0, 0), memory_space=pltpu.SMEM),
                      pl.BlockSpec((1, 1, TOP_K * tc), lambda i, l: (jnp.minimum(i + 1, n_tiles - 1), 0, 0),
                                   memory_space=pltpu.SMEM),
                      pl.BlockSpec(memory_space=pl.ANY),
                      pl.BlockSpec((tc, D_MODEL), lambda i, l: (i, 0)),
                      pl.BlockSpec((tc, LANES), lambda i, l: (i, 0)),
                      vec(D_MODEL), vec(D_MODEL),
                      pl.BlockSpec((None, D_MODEL, D_MODEL), lambda i, l: (l[0], 0, 0)),
                      vec(D_MODEL),
                      pl.BlockSpec((None, tc, PLE_DIM), lambda i, l: (l[0], i, 0)),
                      pl.BlockSpec((None, PLE_DIM, D_MODEL), lambda i, l: (l[0], 0, 0)),
                      vec(D_MODEL), vec(D_MODEL)],
            out_specs=pl.BlockSpec((tc, D_MODEL), lambda i, l: (i, 0)),
            scratch_shapes=[pltpu.VMEM((2, TOP_K, tc * TOKEN_ROWS, LANES), _F32),
                            pltpu.SemaphoreType.DMA((2,))]),
        out_shape=jax.ShapeDtypeStruct((S, D_MODEL), _F32),
        compiler_params=_params(),
    )(lid, pos, pos, y_tok, x, route, ln2_g, ln2_b, w_pg, b_pg, p, w_pu, ln3_g, ln3_b)


def _split_in_proj(w_in):
    sizes = (SC_WIDTH, SC_WIDTH, SC_WIDTH, GLA_QK, GLA_QK, GLA_VW, GLA_GATE_RANK, GLA_VW,
             MLA_Q_RANK, MLA_KV_RANK, MLA_ROPE, D_MODEL, D_MODEL, D_MODEL)
    offs = [0]
    for s in sizes:
        offs.append(offs[-1] + s)
    col = lambda j: w_in[:, :, offs[j]:offs[j + 1]]
    L = w_in.shape[0]
    zeros = lambda n: jnp.zeros((L, D_MODEL, n), w_in.dtype)
    w_conv_in = jnp.concatenate([col(0), col(1), col(2)], axis=-1)
    w_gla_in = jnp.concatenate([col(3), col(4), col(5), col(7), col(6), zeros(LANES - GLA_GATE_RANK)], axis=-1)
    w_mla_in = jnp.concatenate([col(8), col(9), col(10), zeros(LANES - MLA_ROPE)], axis=-1)
    w_gates = jnp.concatenate([col(11), col(12), col(13)], axis=-1)
    return [w.astype(_MXU) for w in (w_conv_in, w_gla_in, w_mla_in, w_gates)]


def kernel(x, p, positions, ln0_g, ln0_b, w_in, w_conv, w_gla_gate, b_gla_gate, gla_norm_g, mla_q_norm_g, mla_kv_norm_g, w_uq, w_ukv, w_br, w_o, ln1_g, ln1_b, w_grp, b_grp, w_exp, b_exp, w_gate, w_up, w_down, ln2_g, ln2_b, w_ple_gate, b_ple_gate, w_ple_up, ln3_g, ln3_b):
    B, S, D = x.shape
    L = w_in.shape[0]
    assert B == 1 and D == D_MODEL
    assert all(S % t == 0 for t in (ROW_TILE, ATTN_TQ, DISPATCH_TILE, COMBINE_TILE, MOE_BM))
    assert ATTN_TQ % (2 * ATTN_TK) == 0 and ATTN_TK % ATTN_ROWS == 0
    H = MLA_HEADS

    w_conv_in, w_gla_in, w_mla_in, w_gates = _split_in_proj(w_in)
    w_gg = jnp.pad(w_gla_gate, ((0, 0), (0, LANES - GLA_GATE_RANK), (0, 0))).astype(_MXU)
    w_uq_p = jnp.pad(w_uq.reshape(L, MLA_Q_RANK, H, MLA_QK),
                     ((0, 0), (0, 0), (0, 0), (0, MLA_QK_PAD - MLA_QK))
                     ).reshape(L, MLA_Q_RANK, H * MLA_QK_PAD).astype(_MXU)
    w_ukv4 = w_ukv.reshape(L, MLA_KV_RANK, H, MLA_NOPE + MLA_V)
    w_uk = w_ukv4[..., :MLA_NOPE].reshape(L, MLA_KV_RANK, H * MLA_NOPE).astype(_MXU)
    w_uv = w_ukv4[..., MLA_NOPE:].reshape(L, MLA_KV_RANK, H * MLA_V).astype(_MXU)
    w_route = jnp.concatenate(
        [w_grp, w_exp.reshape(L, D, N_EXPERTS),
         jnp.zeros((L, D, LANES - N_GROUPS - N_EXPERTS), _F32)], axis=-1).astype(_MXU)
    b_route = jnp.concatenate(
        [b_grp, b_exp.reshape(L, N_EXPERTS),
         jnp.zeros((L, LANES - N_GROUPS - N_EXPERTS), _F32)], axis=-1).reshape(L, 1, LANES)
    w_br_c = w_br.astype(_MXU)
    w_o_c = w_o.astype(_MXU)
    w_pg_c = w_ple_gate.astype(_MXU)
    w_pu_c = w_ple_up.astype(_MXU)
    row = lambda a: a.reshape(L, 1, a.shape[-1])

    cos_t, sin_t = _rope_tables(positions.reshape(S))
    x0 = _entry_norm(x.reshape(S, D), ln0_g, ln0_b)
    p3 = p.reshape(L, S, PLE_DIM)

    def layer(xc, i):
        lid = jnp.reshape(i, (1,)).astype(jnp.int32)
        y_a, y_b, q, k, v = _mixers(lid, xc, w_conv_in, w_conv, w_gla_in, w_gg, row(b_gla_gate),
                                    row(gla_norm_g), w_mla_in, row(mla_q_norm_g), row(mla_kv_norm_g),
                                    w_uq_p, w_uk, w_uv, cos_t, sin_t)
        y_c = _attention(q, k, v)
        x1, x1_tok, route, counts_row = _merge(lid, xc, y_a, y_b, y_c, w_gates, w_br_c, w_o_c,
                                               row(ln1_g), row(ln1_b), w_route, b_route)
        (dest, counts, pad_start, block_expert, first, next_expert, slot_of,
         meta) = _dispatch_plan(route, counts_row, S)
        n_rows = (S * TOP_K // MOE_BM + N_EXPERTS) * MOE_BM
        xs_tok = _dispatch(counts, pad_start, meta, dest, x1_tok, n_rows)
        y_tok = _experts(lid, block_expert, first, next_expert, slot_of, meta, xs_tok, w_gate, w_up, w_down)
        x3 = _combine(lid, dest, y_tok, x1, route, row(ln2_g), row(ln2_b), w_pg_c, row(b_ple_gate),
                      p3, w_pu_c, row(ln3_g), row(ln3_b))
        return x3, None

    out, _ = lax.scan(layer, x0, jnp.arange(L, dtype=jnp.int32))
    return out.reshape(B, S, D)
```
